```python
import math
import jax, jax.numpy as jnp
from jax import lax
import numpy as np

D_MODEL = 1024
BATCH = 8
SEQ = 2048
DEPTH = 4
DEC_BATCH = 32
DEC_SEQ = 4
PAST_LEN = 8192
PAGE_SIZE = 128

N_MIXERS = 3
N_POOL_LAYERS = (DEPTH + N_MIXERS - 1) // N_MIXERS
N_ATTN_LAYERS = (DEPTH + N_MIXERS - 2) // N_MIXERS
N_CONV_LAYERS = (DEPTH + N_MIXERS - 3) // N_MIXERS

POOL_WINDOWS = (2, 4, 8, 16)
POOL_GROUPS = len(POOL_WINDOWS)
POOL_GW = D_MODEL // POOL_GROUPS
POOL_STATE = max(POOL_WINDOWS) - 1

ATTN_GROUPS = ((128, 1), (512, 4), (2048, 16))
N_ATTN_GROUPS = len(ATTN_GROUPS)
HEAD_DIM = 64
H_G = D_MODEL // HEAD_DIM
QB = 128
N_BUCKETS = 32
MAX_DISTANCE = 2048

CONV_WIDTH = 31
D_FF = 4 * D_MODEL
EPS = 1e-6
NEG_INF = -1e30

kernel_name = "hybrid_pool_dilattn_conformer_decode_step"


def rmsnorm(x, g):
    x32 = x.astype(jnp.float32)
    y = x32 * lax.rsqrt(jnp.mean(x32 * x32, axis=-1, keepdims=True) + EPS) * g.astype(jnp.float32)
    return y.astype(x.dtype)


def squared_relu_mlp(h, w_up, w_down):
    a = jax.nn.relu(jnp.einsum('btd,df->btf', h, w_up))
    return jnp.einsum('btf,fd->btd', a * a, w_down)


def pool_mixer(h_new, h_prev, pos0, w_grp, scale):
    B, T, _ = h_new.shape
    n_prev = h_prev.shape[1]
    h_ext = jnp.concatenate([h_prev, h_new], axis=1).astype(jnp.float32)
    h_ext = h_ext.reshape(B, n_prev + T, POOL_GROUPS, POOL_GW)
    csum = jnp.concatenate([jnp.zeros((B, 1, POOL_GROUPS, POOL_GW), jnp.float32),
                            jnp.cumsum(h_ext, axis=1)], axis=1)
    win = jnp.array(POOL_WINDOWS, jnp.int32)
    t = jnp.arange(T, dtype=jnp.int32)
    end = n_prev + t + 1
    start = jnp.maximum(end[:, None] - win[None, :], 0)
    grp = jnp.arange(POOL_GROUPS)[None, :]
    wsum = csum[:, end] - csum[:, start, grp]
    count = jnp.minimum(pos0 + t[:, None] + 1, win[None, :]).astype(jnp.float32)
    pooled = wsum / count[None, :, :, None] - h_ext[:, n_prev:]
    y = jnp.einsum('btgc,gcd->btgd', pooled, w_grp.astype(jnp.float32)).reshape(B, T, D_MODEL)
    return (y * scale.astype(jnp.float32)).astype(h_new.dtype)


def t5_bucket(dist):
    max_exact = N_BUCKETS // 2
    df = jnp.maximum(dist, 1).astype(jnp.float32)
    large = max_exact + (jnp.log(df / max_exact) / math.log(MAX_DISTANCE / max_exact)
                         * (N_BUCKETS - max_exact)).astype(jnp.int32)
    large = jnp.minimum(large, N_BUCKETS - 1)
    return jnp.where(dist < max_exact, dist, large)


def softmax_lse(s, mask):
    s = jnp.where(mask, s, NEG_INF)
    m = jnp.max(s, axis=-1, keepdims=True)
    e = jnp.exp(s - m)
    l = jnp.sum(e, axis=-1, keepdims=True)
    return e / l, (m + jnp.log(l))[..., 0]


def dilated_attn_prompt(q, k, v, bias_tab, dil, n_steps):
    B, S, H, Dh = q.shape
    L = S // dil
    NB = -(-L // QB)
    Lp = NB * QB

    def sub(a):
        return a.reshape(B, L, dil, H, Dh).transpose(0, 2, 1, 3, 4)

    qs = jnp.pad(sub(q), ((0, 0), (0, 0), (0, Lp - L), (0, 0), (0, 0))).reshape(B, dil, NB, QB, H, Dh)

    def kblocks(a):
        a = jnp.pad(sub(a), ((0, 0), (0, 0), (QB, Lp - L), (0, 0), (0, 0))).reshape(B, dil, NB + 1, QB, H, Dh)
        return jnp.concatenate([a[:, :, :-1], a[:, :, 1:]], axis=3)

    kb, vb = kblocks(k), kblocks(v)
    qi = jnp.arange(QB)[:, None]
    kj = jnp.arange(2 * QB)[None, :]
    steps = qi - kj + QB
    blk = jnp.arange(NB)[:, None, None]
    mask = (steps >= 0) & (steps <= n_steps) & (blk * QB + kj >= QB)
    bias = bias_tab[t5_bucket(dil * jnp.clip(steps, 0, n_steps))].transpose(2, 0, 1)
    s = jnp.einsum('brnqhc,brnkhc->brnhqk', qs, kb, preferred_element_type=jnp.float32) * (HEAD_DIM ** -0.5)
    s = s + bias.astype(jnp.float32)
    p, lse = softmax_lse(s, mask[None, None, :, None])
    o = jnp.einsum('brnhqk,brnkhc->brnqhc', p, vb.astype(jnp.float32))
    o = o.reshape(B, dil, Lp, H, Dh)[:, :, :L].transpose(0, 2, 1, 3, 4).reshape(B, S, H, Dh)
    lse = lse.transpose(0, 1, 2, 4, 3).reshape(B, dil, Lp, H)[:, :, :L].transpose(0, 2, 1, 3).reshape(B, S, H)
    return o, lse


def dilated_attn_sample(q, k_ext, v_ext, n_cache, bias_tab, dil, n_steps):
    T = q.shape[1]
    idx = n_cache + jnp.arange(T)[:, None] - dil * jnp.arange(n_steps + 1)[None, :]
    valid = idx >= 0
    idx = jnp.maximum(idx, 0)
    kg = k_ext[:, idx]
    vg = v_ext[:, idx]
    bias = bias_tab[t5_bucket(dil * jnp.arange(n_steps + 1))].T
    s = jnp.einsum('bthc,btkhc->bthk', q, kg, preferred_element_type=jnp.float32) * (HEAD_DIM ** -0.5)
    s = s + bias.astype(jnp.float32)[None, None]
    p, lse = softmax_lse(s, valid[None, :, None, :])
    o = jnp.einsum('bthk,btkhc->bthc', p, vg.astype(jnp.float32))
    return o, lse


def merge_groups(outs, lses, w_o, dtype):
    wts = jax.nn.softmax(jnp.stack(lses, axis=0), axis=0)
    o = jnp.sum(wts[..., None] * jnp.stack(outs, axis=0), axis=0)
    B, T = o.shape[:2]
    return jnp.einsum('bte,ed->btd', o.reshape(B, T, H_G * HEAD_DIM).astype(dtype), w_o)


def project_qkv(h, w_qkv):
    B, T, _ = h.shape
    return jnp.einsum('btd,de->bte', h, w_qkv).reshape(B, T, N_ATTN_GROUPS, 3, H_G, HEAD_DIM)


def attn_prompt(h, w_qkv, w_o, rel_bias):
    T = h.shape[1]
    qkv = project_qkv(h, w_qkv)
    outs, lses, kv_new = [], [], []
    for g, (win, dil) in enumerate(ATTN_GROUPS):
        o, l = dilated_attn_prompt(qkv[:, :, g, 0], qkv[:, :, g, 1], qkv[:, :, g, 2],
                                   rel_bias[:, g * H_G:(g + 1) * H_G], dil, win // dil)
        outs.append(o)
        lses.append(l)
        kv_new.append(qkv[:, T - min(win, T):, g, 1:3])
    return merge_groups(outs, lses, w_o, h.dtype), kv_new


def attn_sample(h, caches, w_qkv, w_o, rel_bias):
    T = h.shape[1]
    qkv = project_qkv(h, w_qkv)
    outs, lses, kv_new = [], [], []
    for g, (win, dil) in enumerate(ATTN_GROUPS):
        cache = caches[g]
        n_cache = cache.shape[1]
        kv_ext = jnp.concatenate([cache, qkv[:, :, g, 1:3].astype(cache.dtype)], axis=1)
        o, l = dilated_attn_sample(qkv[:, :, g, 0], kv_ext[:, :, 0], kv_ext[:, :, 1], n_cache,
                                   rel_bias[:, g * H_G:(g + 1) * H_G], dil, win // dil)
        outs.append(o)
        lses.append(l)
        keep = min(win, n_cache + T)
        kv_new.append(kv_ext[:, n_cache + T - keep:])
    return merge_groups(outs, lses, w_o, h.dtype), kv_new


def conv_mixer(h_new, u_prev, w1, b1, wdw, bdw, ln_g, ln_b, w2, b2):
    a = jnp.einsum('btd,de->bte', h_new, w1) + b1
    u = a[..., :D_MODEL] * jax.nn.sigmoid(a[..., D_MODEL:])
    u_ext = jnp.concatenate([u_prev.astype(u.dtype), u], axis=1)
    y = lax.conv_general_dilated(u_ext, wdw[:, None, :].astype(u.dtype), window_strides=(1,), padding='VALID',
                                 dimension_numbers=('NWC', 'WIO', 'NWC'), feature_group_count=D_MODEL) + bdw
    y32 = y.astype(jnp.float32)
    mu = jnp.mean(y32, axis=-1, keepdims=True)
    var = jnp.mean(jnp.square(y32 - mu), axis=-1, keepdims=True)
    yn = (y32 - mu) * lax.rsqrt(var + EPS) * ln_g.astype(jnp.float32) + ln_b.astype(jnp.float32)
    z = jnp.einsum('btd,de->bte', jax.nn.silu(yn).astype(h_new.dtype), w2) + b2
    return z, u_ext[:, -(CONV_WIDTH - 1):]


def setup_inputs(seed: int = 0) -> dict:
    key = jax.random.key(seed)
    ks = jax.random.split(key, 24)

    def nrm(k, shape, scale):
        return jax.random.normal(k, shape, jnp.float32) * scale

    return {
        'x_prompt': nrm(ks[0], (BATCH, SEQ, D_MODEL), 1.0),
        'x_sample': nrm(ks[1], (DEC_BATCH, DEC_SEQ, D_MODEL), 1.0),
        'state_pool': nrm(ks[2], (N_POOL_LAYERS, DEC_BATCH, POOL_STATE, D_MODEL), 1.0),
        'cache_kv_g0': nrm(ks[3], (N_ATTN_LAYERS, DEC_BATCH, min(ATTN_GROUPS[0][0], PAST_LEN), 2, H_G, HEAD_DIM), 1.0),
        'cache_kv_g1': nrm(ks[4], (N_ATTN_LAYERS, DEC_BATCH, min(ATTN_GROUPS[1][0], PAST_LEN), 2, H_G, HEAD_DIM), 1.0),
        'cache_kv_g2': nrm(ks[5], (N_ATTN_LAYERS, DEC_BATCH, min(ATTN_GROUPS[2][0], PAST_LEN), 2, H_G, HEAD_DIM), 1.0),
        'state_conv': nrm(ks[6], (N_CONV_LAYERS, DEC_BATCH, CONV_WIDTH - 1, D_MODEL), 0.5),
        'norm_gains': 1.0 + nrm(ks[7], (DEPTH, 4, D_MODEL), 0.02),
        'rel_bias': nrm(ks[8], (N_BUCKETS, N_ATTN_GROUPS * H_G), 0.1),
        'pool_w': nrm(ks[9], (N_POOL_LAYERS, POOL_GROUPS, POOL_GW, POOL_GW), POOL_GW ** -0.5),
        'pool_scale': 1.0 + nrm(ks[10], (N_POOL_LAYERS, D_MODEL), 0.1),
        'attn_w_qkv': nrm(ks[11], (N_ATTN_LAYERS, D_MODEL, N_ATTN_GROUPS * 3 * H_G * HEAD_DIM), D_MODEL ** -0.5),
        'attn_w_o': nrm(ks[12], (N_ATTN_LAYERS, H_G * HEAD_DIM, D_MODEL), (H_G * HEAD_DIM) ** -0.5),
        'conv_w_pw1': nrm(ks[13], (N_CONV_LAYERS, D_MODEL, 2 * D_MODEL), D_MODEL ** -0.5),
        'conv_b_pw1': nrm(ks[14], (N_CONV_LAYERS, 2 * D_MODEL), 0.02),
        'conv_w_dw': nrm(ks[15], (N_CONV_LAYERS, CONV_WIDTH, D_MODEL), CONV_WIDTH ** -0.5),
        'conv_b_dw': nrm(ks[16], (N_CONV_LAYERS, D_MODEL), 0.02),
        'conv_ln_g': 1.0 + nrm(ks[17], (N_CONV_LAYERS, D_MODEL), 0.02),
        'conv_ln_b': nrm(ks[18], (N_CONV_LAYERS, D_MODEL), 0.02),
        'conv_w_pw2': nrm(ks[19], (N_CONV_LAYERS, D_MODEL, D_MODEL), D_MODEL ** -0.5),
        'conv_b_pw2': nrm(ks[20], (N_CONV_LAYERS, D_MODEL), 0.02),
        'ffn_w_up': nrm(ks[21], (DEPTH, D_MODEL, D_FF), D_MODEL ** -0.5),
        'ffn_w_down': nrm(ks[22], (DEPTH, D_FF, D_MODEL), D_FF ** -0.5),
    }


def reference(x_prompt, x_sample, state_pool, cache_kv_g0, cache_kv_g1, cache_kv_g2, state_conv,
              norm_gains, rel_bias, pool_w, pool_scale, attn_w_qkv, attn_w_o,
              conv_w_pw1, conv_b_pw1, conv_w_dw, conv_b_dw, conv_ln_g, conv_ln_b, conv_w_pw2, conv_b_pw2,
              ffn_w_up, ffn_w_down):
    xp, xs = x_prompt, x_sample
    pool_p, pool_s, conv_p, conv_s = [], [], [], []
    kv_p = [[] for _ in ATTN_GROUPS]
    kv_s = [[] for _ in ATTN_GROUPS]
    for i in range(DEPTH):
        kind = i % N_MIXERS
        j = i // N_MIXERS
        hp = rmsnorm(xp, norm_gains[i, 0])
        hs = rmsnorm(xs, norm_gains[i, 0])
        if kind == 0:
            mp = pool_mixer(hp, hp[:, :0], 0, pool_w[j], pool_scale[j])
            ms = pool_mixer(hs, state_pool[j].astype(hs.dtype), PAST_LEN, pool_w[j], pool_scale[j])
            pool_p.append(hp[:, -POOL_STATE:])
            pool_s.append(jnp.concatenate([state_pool[j].astype(hs.dtype), hs], axis=1)[:, -POOL_STATE:])
        elif kind == 1:
            mp, new_p = attn_prompt(hp, attn_w_qkv[j], attn_w_o[j], rel_bias)
            ms, new_s = attn_sample(hs, (cache_kv_g0[j], cache_kv_g1[j], cache_kv_g2[j]),
                                    attn_w_qkv[j], attn_w_o[j], rel_bias)
            for g in range(N_ATTN_GROUPS):
                kv_p[g].append(new_p[g])
                kv_s[g].append(new_s[g])
        else:
            cw = (conv_w_pw1[j], conv_b_pw1[j], conv_w_dw[j], conv_b_dw[j],
                  conv_ln_g[j], conv_ln_b[j], conv_w_pw2[j], conv_b_pw2[j])
            mp, up = conv_mixer(hp, jnp.zeros((hp.shape[0], CONV_WIDTH - 1, D_MODEL), hp.dtype), *cw)
            ms, us = conv_mixer(hs, state_conv[j], *cw)
            conv_p.append(up)
            conv_s.append(us)
        xp = xp + rmsnorm(mp, norm_gains[i, 1])
        xs = xs + rmsnorm(ms, norm_gains[i, 1])
        xp = xp + rmsnorm(squared_relu_mlp(rmsnorm(xp, norm_gains[i, 2]), ffn_w_up[i], ffn_w_down[i]), norm_gains[i, 3])
        xs = xs + rmsnorm(squared_relu_mlp(rmsnorm(xs, norm_gains[i, 2]), ffn_w_up[i], ffn_w_down[i]), norm_gains[i, 3])
    return (xp, xs,
            jnp.stack(pool_p), jnp.stack(pool_s),
            jnp.stack(kv_p[0]), jnp.stack(kv_s[0]),
            jnp.stack(kv_p[1]), jnp.stack(kv_s[1]),
            jnp.stack(kv_p[2]), jnp.stack(kv_s[2]),
            jnp.stack(conv_p), jnp.stack(conv_s))
```

```python
import functools
import math

import jax
import jax.numpy as jnp
import numpy as np
from jax import lax
from jax.experimental import pallas as pl
from jax.experimental.pallas import tpu as pltpu

F32 = jnp.float32
BF16 = jnp.bfloat16

EPS = 1e-6
NEG_INF = -1e30
POOL_WINDOWS = (2, 4, 8, 16)
POOL_STATE = max(POOL_WINDOWS) - 1
PAST_LEN = 8192
ATTN_GROUPS = ((128, 1), (512, 4), (2048, 16))
HEAD_DIM = 64
QB = 128
N_BUCKETS = 32
MAX_DISTANCE = 2048
CONV_WIDTH = 31
CARRY_ROWS = 32
POOL_CARRY = 16

VMEM_LIMIT = 56 * 1024 * 1024


def _params(*sem):
    return pltpu.CompilerParams(dimension_semantics=sem, vmem_limit_bytes=VMEM_LIMIT)


def _resident(shape):
    nd = len(shape)
    return pl.BlockSpec(shape, lambda *_: (0,) * nd, pipeline_mode=pl.Buffered(1))


def _rms(x, g):
    return x * lax.rsqrt(jnp.mean(x * x, axis=-1, keepdims=True) + EPS) * g


def _dot(a, b):
    return jnp.dot(a, b, preferred_element_type=F32)


def _ffn_body(x_ref, g_ref, wup_ref, wdn_ref, o_ref, *, f_chunk):
    x = x_ref[...]
    h = _rms(x, g_ref[2:3]).astype(BF16)
    acc = jnp.zeros_like(x)
    for c in range(wup_ref.shape[1] // f_chunk):
        a = jnp.maximum(_dot(h, wup_ref[:, c * f_chunk:(c + 1) * f_chunk]), 0.0)
        acc = acc + _dot((a * a).astype(BF16), wdn_ref[c * f_chunk:(c + 1) * f_chunk, :])
    o_ref[...] = x + _rms(acc, g_ref[3:4])


def _ffn(x, gains, w_up, w_down, tm):
    n, d = x.shape
    f = w_up.shape[1]
    return pl.pallas_call(
        functools.partial(_ffn_body, f_chunk=1024),
        grid=(n // tm,),
        in_specs=[pl.BlockSpec((tm, d), lambda i: (i, 0)), _resident(gains.shape),
                  _resident((d, f)), _resident((f, d))],
        out_specs=pl.BlockSpec((tm, d), lambda i: (i, 0)),
        out_shape=jax.ShapeDtypeStruct((n, d), F32),
        compiler_params=_params("parallel"),
        name="ffn",
    )(x, gains, w_up, w_down)


def _pool_p_body(x_ref, g_ref, pw_ref, ps_ref, o_ref, hl_ref, carry_ref, *, ts):
    j = pl.program_id(1)

    @pl.when(j == 0)
    def _():
        carry_ref[...] = jnp.zeros_like(carry_ref)

    x = x_ref[0]
    h = _rms(x, g_ref[0:1])
    ext = jnp.concatenate([carry_ref[...], h], axis=0)
    sums = []
    s = ext
    for k in (1, 2, 4, 8):
        s = s + pltpu.roll(s, k, axis=0)
        sums.append(s)
    pos = j * ts + lax.broadcasted_iota(jnp.int32, (ts, 1), 0)
    gw = pw_ref.shape[1]
    ys = []
    for gi, (w, s) in enumerate(zip(POOL_WINDOWS, sums)):
        cols = slice(gi * gw, (gi + 1) * gw)
        inv_cnt = 1.0 / jnp.minimum(pos + 1, w).astype(F32)
        pooled = s[POOL_CARRY:, cols] * inv_cnt - h[:, cols]
        ys.append(_dot(pooled.astype(BF16), pw_ref[gi]))
    y = jnp.concatenate(ys, axis=1) * ps_ref[...]
    o_ref[0] = x + _rms(y, g_ref[1:2])
    carry_ref[...] = h[ts - POOL_CARRY:, :]
    hl_ref[0] = h[ts - POOL_CARRY:, :]


def _pool_p(x, gains, pw, ps, ts):
    b, s, d = x.shape
    return pl.pallas_call(
        functools.partial(_pool_p_body, ts=ts),
        grid=(b, s // ts),
        in_specs=[pl.BlockSpec((1, ts, d), lambda i, j: (i, j, 0)), _resident(gains.shape),
                  _resident(pw.shape), _resident(ps.shape)],
        out_specs=[pl.BlockSpec((1, ts, d), lambda i, j: (i, j, 0)),
                   pl.BlockSpec((1, POOL_CARRY, d), lambda i, j: (i, 0, 0))],
        out_shape=[jax.ShapeDtypeStruct((b, s, d), F32),
                   jax.ShapeDtypeStruct((b, POOL_CARRY, d), F32)],
        scratch_shapes=[pltpu.VMEM((POOL_CARRY, d), F32)],
        compiler_params=_params("parallel", "arbitrary"),
        name="pool_prompt",
    )(x, gains, pw, ps)


def _pool_s_body(x_ref, st_ref, g_ref, pw_ref, ps_ref, o_ref, ns_ref, *, nb, nt, pos0):
    x = x_ref[...]
    h = _rms(x, g_ref[0:1])
    ext = [st_ref[i * nb:(i + 1) * nb, :] for i in range(POOL_STATE)]
    ext += [h[t * nb:(t + 1) * nb, :] for t in range(nt)]
    gw = pw_ref.shape[1]
    for t in range(nt):
        ys = []
        for gi, w in enumerate(POOL_WINDOWS):
            cols = slice(gi * gw, (gi + 1) * gw)
            end = POOL_STATE + t
            acc = ext[end][:, cols]
            for i in range(1, min(w, end + 1)):
                acc = acc + ext[end - i][:, cols]
            pooled = acc / float(min(pos0 + t + 1, w)) - ext[end][:, cols]
            ys.append(_dot(pooled.astype(BF16), pw_ref[gi]))
        y = jnp.concatenate(ys, axis=1) * ps_ref[...]
        o_ref[t * nb:(t + 1) * nb, :] = x[t * nb:(t + 1) * nb, :] + _rms(y, g_ref[1:2])
    for i in range(POOL_STATE):
        ns_ref[i * nb:(i + 1) * nb, :] = ext[nt + i]


def _pool_s(x, state, gains, pw, ps, nb, nt, pos0):
    return pl.pallas_call(
        functools.partial(_pool_s_body, nb=nb, nt=nt, pos0=pos0),
        out_shape=[jax.ShapeDtypeStruct(x.shape, F32), jax.ShapeDtypeStruct(state.shape, F32)],
        compiler_params=pltpu.CompilerParams(vmem_limit_bytes=VMEM_LIMIT),
        name="pool_sample",
    )(x, state, gains, pw, ps)


def _qkv_body(x_ref, g_ref, w_ref, o_ref, *, n_chunk):
    h = _rms(x_ref[...], g_ref[0:1]).astype(BF16)
    for c in range(w_ref.shape[1] // n_chunk):
        cols = slice(c * n_chunk, (c + 1) * n_chunk)
        o_ref[:, cols] = _dot(h, w_ref[:, cols])


def _qkv(x, gains, w, tm):
    n, d = x.shape
    e = w.shape[1]
    return pl.pallas_call(
        functools.partial(_qkv_body, n_chunk=1024),
        grid=(n // tm,),
        in_specs=[pl.BlockSpec((tm, d), lambda i: (i, 0)), _resident(gains.shape), _resident((d, e))],
        out_specs=pl.BlockSpec((tm, e), lambda i: (i, 0)),
        out_shape=jax.ShapeDtypeStruct((n, e), F32),
        compiler_params=_params("parallel"),
        name="qkv",
    )(x, gains, w)


def _attn_p_body(*refs, n_steps, has_prev, n_heads):
    if has_prev:
        q_ref, kp_ref, kc_ref, vp_ref, vc_ref, bias_ref, o_ref, lse_ref = refs
        k = jnp.concatenate([kp_ref[0], kc_ref[0]], axis=0).astype(BF16)
        v = jnp.concatenate([vp_ref[0], vc_ref[0]], axis=0).astype(BF16)
    else:
        q_ref, kc_ref, vc_ref, bias_ref, o_ref, lse_ref = refs
        k = kc_ref[0].astype(BF16)
        v = vc_ref[0].astype(BF16)
    n = pl.program_id(2)
    q = (q_ref[0] * (HEAD_DIM ** -0.5)).astype(BF16)
    nk = k.shape[0]
    off = QB if has_prev else 0
    qi = lax.broadcasted_iota(jnp.int32, (QB, nk), 0)
    kj = lax.broadcasted_iota(jnp.int32, (QB, nk), 1)
    steps = qi - kj + off
    mask = (steps >= 0) & (steps <= n_steps)
    if has_prev:
        mask = mask & (n * QB + kj >= QB)
    head = lax.broadcasted_iota(jnp.int32, (QB, n_heads), 1)
    lse = jnp.zeros((QB, n_heads), F32)
    for h in range(n_heads):
        sl = slice(h * HEAD_DIM, (h + 1) * HEAD_DIM)
        s = lax.dot_general(q[:, sl], k[:, sl], (((1,), (1,)), ((), ())), preferred_element_type=F32)
        s = jnp.where(mask, s + bias_ref[h], NEG_INF)
        m = jnp.max(s, axis=-1, keepdims=True)
        e = jnp.exp(s - m)
        l = jnp.sum(e, axis=-1, keepdims=True)
        o_ref[0, :, sl] = _dot(e.astype(BF16), v[:, sl]) / l
        lse = jnp.where(head == h, m + jnp.log(l), lse)
    lse_ref[0, 0] = lse


def _attn_p(qkv, bias, g, dil, n_steps, batch, seq, n_heads):
    e = n_heads * HEAD_DIM
    n_col = qkv.shape[1] // e
    sub = seq // dil
    nb = sub // QB
    has_prev = nb > 1
    view = qkv.reshape(batch, sub, dil * n_col * e)

    def col(c):
        return lambda b, r, n: (b, n, r * n_col + g * 3 + c)

    def col_prev(c):
        return lambda b, r, n: (b, jnp.maximum(n - 1, 0), r * n_col + g * 3 + c)

    blk = (1, QB, e)
    if has_prev:
        in_specs = [pl.BlockSpec(blk, col(0)), pl.BlockSpec(blk, col_prev(1)), pl.BlockSpec(blk, col(1)),
                    pl.BlockSpec(blk, col_prev(2)), pl.BlockSpec(blk, col(2))]
        args = (view,) * 5
    else:
        in_specs = [pl.BlockSpec(blk, col(0)), pl.BlockSpec(blk, col(1)), pl.BlockSpec(blk, col(2))]
        args = (view,) * 3
    o, lse = pl.pallas_call(
        functools.partial(_attn_p_body, n_steps=n_steps, has_prev=has_prev, n_heads=n_heads),
        grid=(batch, dil, nb),
        in_specs=in_specs + [_resident(bias.shape)],
        out_specs=[pl.BlockSpec(blk, lambda b, r, n: (b, n, r)),
                   pl.BlockSpec((1, 1, QB, n_heads), lambda b, r, n: (b, r, n, 0))],
        out_shape=[jax.ShapeDtypeStruct((batch, sub, dil * e), F32),
                   jax.ShapeDtypeStruct((batch, dil, sub, n_heads), F32)],
        compiler_params=_params("parallel", "parallel", "arbitrary"),
        name=f"attn_prompt_g{g}",
    )(*args, bias)
    return o.reshape(batch * seq, e), lse.transpose(0, 2, 1, 3).reshape(batch * seq, n_heads)


def _expand_heads(w, e_ref):
    hi = w.astype(BF16)
    lo = (w - hi.astype(F32)).astype(BF16)
    return _dot(hi, e_ref[...]) + _dot(lo, e_ref[...])


def _merge_groups(os_, lses, e_ref):
    m = functools.reduce(jnp.maximum, lses)
    es = [jnp.exp(l - m) for l in lses]
    den = functools.reduce(lambda a, b: a + b, es)
    acc = None
    for o, e in zip(os_, es):
        term = _expand_heads(e / den, e_ref) * o
        acc = term if acc is None else acc + term
    return acc


def _merge_body(x_ref, o0_ref, o1_ref, o2_ref, l0_ref, l1_ref, l2_ref, e_ref, wo_ref, g_ref, out_ref):
    o = _merge_groups([o0_ref[...], o1_ref[...], o2_ref[...]],
                      [l0_ref[...], l1_ref[...], l2_ref[...]], e_ref)
    y = _dot(o.astype(BF16), wo_ref[...])
    out_ref[...] = x_ref[...] + _rms(y, g_ref[1:2])


def _merge(x, os_, lses, expand, w_o, gains, tm):
    n, d = x.shape
    e, nh = os_[0].shape[1], lses[0].shape[1]
    row = lambda i: (i, 0)
    return pl.pallas_call(
        _merge_body,
        grid=(n // tm,),
        in_specs=[pl.BlockSpec((tm, d), row)] + [pl.BlockSpec((tm, e), row)] * 3
                 + [pl.BlockSpec((tm, nh), row)] * 3
                 + [_resident(expand.shape), _resident(w_o.shape), _resident(gains.shape)],
        out_specs=pl.BlockSpec((tm, d), row),
        out_shape=jax.ShapeDtypeStruct((n, d), F32),
        compiler_params=_params("parallel"),
        name="attn_merge",
    )(x, *os_, *lses, expand, w_o, gains)


def _proj_s_body(x_ref, o_ref, wo_ref, g_ref, out_ref):
    y = _dot(o_ref[...].astype(BF16), wo_ref[...])
    out_ref[...] = x_ref[...] + _rms(y, g_ref[1:2])


def _proj_s(x, o, w_o, gains):
    return pl.pallas_call(
        _proj_s_body,
        out_shape=jax.ShapeDtypeStruct(x.shape, F32),
        compiler_params=pltpu.CompilerParams(vmem_limit_bytes=VMEM_LIMIT),
        name="attn_proj_sample",
    )(x, o, w_o, gains)


def _head_scores(k, qrow, et_ref):
    prod = k * qrow
    hi = prod.astype(BF16)
    lo = (prod - hi.astype(F32)).astype(BF16)
    return _dot(hi, et_ref[...]) + _dot(lo, et_ref[...])


def _attn_s_body(qkv_ref, c0_ref, c1_ref, c2_ref, b0c_ref, b0n_ref, b1c_ref, b1n_ref, b2c_ref, b2n_ref,
                 e_ref, et_ref, o_ref, *, nt, e_dim):
    cache_refs = (c0_ref, c1_ref, c2_ref)
    strided_bias = {1: (b1c_ref, b1n_ref), 2: (b2c_ref, b2n_ref)}
    n0 = c0_ref.shape[1]
    row0 = lax.broadcasted_iota(jnp.int32, (n0, 1), 0)
    rown = lax.broadcasted_iota(jnp.int32, (nt, 1), 0)
    for t in range(nt):
        outs, lses = [], []
        for g, (_, dil) in enumerate(ATTN_GROUPS):
            base = g * 3 * e_dim
            q = qkv_ref[0, t:t + 1, base:base + e_dim] * (HEAD_DIM ** -0.5)
            k_new = qkv_ref[0, :, base + e_dim:base + 2 * e_dim]
            v_new = qkv_ref[0, :, base + 2 * e_dim:base + 3 * e_dim]
            cref = cache_refs[g]
            if dil == 1:
                k_c, v_c = cref[0, :, 0:e_dim], cref[0, :, e_dim:2 * e_dim]
                s_c = jnp.where(row0 >= t, _head_scores(k_c, q, et_ref) + b0c_ref[t], NEG_INF)
                s_n = jnp.where(rown <= t, _head_scores(k_new, q, et_ref) + b0n_ref[t], NEG_INF)
                v_n = v_new
            else:
                k_c = cref[0, :, t * 2 * e_dim:t * 2 * e_dim + e_dim]
                v_c = cref[0, :, t * 2 * e_dim + e_dim:(t + 1) * 2 * e_dim]
                bc_ref, bn_ref = strided_bias[g]
                s_c = _head_scores(k_c, q, et_ref) + bc_ref[...]
                s_n = _head_scores(k_new[t:t + 1], q, et_ref) + bn_ref[...]
                v_n = v_new[t:t + 1]
            m = jnp.maximum(jnp.max(s_c, axis=0, keepdims=True), jnp.max(s_n, axis=0, keepdims=True))
            e_c = jnp.exp(s_c - m)
            e_n = jnp.exp(s_n - m)
            l = jnp.sum(e_c, axis=0, keepdims=True) + jnp.sum(e_n, axis=0, keepdims=True)
            p_c = _dot((e_c / l).astype(BF16), e_ref[...])
            p_n = _dot((e_n / l).astype(BF16), e_ref[...])
            outs.append(jnp.sum(p_c * v_c, axis=0, keepdims=True) + jnp.sum(p_n * v_n, axis=0, keepdims=True))
            lses.append(m + jnp.log(l))
        o_ref[0, t:t + 1, :] = _merge_groups(outs, lses, e_ref)


def _attn_s(qkv, caches, biases, expand, expand_t, nt, e_dim):
    b = qkv.shape[0]
    views, specs = [], []
    for (win, dil), c in zip(ATTN_GROUPS, caches):
        assert c.shape[1] == win and (dil == 1 or nt <= dil), "cache must hold a full window"
        rows = c.shape[1] // dil
        views.append(c.reshape(b, rows, dil * 2 * e_dim))
        specs.append(pl.BlockSpec((1, rows, min(dil, nt) * 2 * e_dim), lambda i: (i, 0, 0)))
    return pl.pallas_call(
        functools.partial(_attn_s_body, nt=nt, e_dim=e_dim),
        grid=(b,),
        in_specs=[pl.BlockSpec((1, nt, qkv.shape[2]), lambda i: (i, 0, 0))] + specs
                 + [_resident(a.shape) for a in biases] + [_resident(expand.shape), _resident(expand_t.shape)],
        out_specs=pl.BlockSpec((1, nt, e_dim), lambda i: (i, 0, 0)),
        out_shape=jax.ShapeDtypeStruct((b, nt, e_dim), F32),
        compiler_params=_params("parallel"),
        name="attn_sample",
    )(qkv, *views, *biases, expand, expand_t)


def _conv_tail(y, x, lng_ref, lnb_ref, w2_ref, b2_ref, g_ref):
    mu = jnp.mean(y, axis=-1, keepdims=True)
    yc = y - mu
    var = jnp.mean(yc * yc, axis=-1, keepdims=True)
    yn = yc * lax.rsqrt(var + EPS) * lng_ref[...] + lnb_ref[...]
    z = _dot((yn * jax.nn.sigmoid(yn)).astype(BF16), w2_ref[...]) + b2_ref[...]
    return x + _rms(z, g_ref[1:2])


def _glu(x, g_ref, w1_ref, b1_ref):
    d = x.shape[1]
    a = _dot(_rms(x, g_ref[0:1]).astype(BF16), w1_ref[...]) + b1_ref[...]
    return a[:, :d] * jax.nn.sigmoid(a[:, d:])


def _conv_p_body(x_ref, g_ref, w1_ref, b1_ref, wdw_ref, bdw_ref, lng_ref, lnb_ref, w2_ref, b2_ref,
                 o_ref, ul_ref, uext_ref, *, ts):
    j = pl.program_id(1)

    @pl.when(j == 0)
    def _():
        uext_ref[0:CARRY_ROWS, :] = jnp.zeros((CARRY_ROWS, uext_ref.shape[1]), F32)

    x = x_ref[0]
    uext_ref[CARRY_ROWS:, :] = _glu(x, g_ref, w1_ref, b1_ref)
    lead = CARRY_ROWS - (CONV_WIDTH - 1)
    y = None
    for k in range(CONV_WIDTH):
        term = wdw_ref[k:k + 1, :] * uext_ref[lead + k:lead + k + ts, :]
        y = term if y is None else y + term
    y = y + bdw_ref[...]
    o_ref[0] = _conv_tail(y, x, lng_ref, lnb_ref, w2_ref, b2_ref, g_ref)
    last = uext_ref[ts:ts + CARRY_ROWS, :]
    uext_ref[0:CARRY_ROWS, :] = last
    ul_ref[0] = last


def _conv_p(x, gains, w1, b1, wdw, bdw, lng, lnb, w2, b2, ts):
    b, s, d = x.shape
    consts = (gains, w1, b1, wdw, bdw, lng, lnb, w2, b2)
    return pl.pallas_call(
        functools.partial(_conv_p_body, ts=ts),
        grid=(b, s // ts),
        in_specs=[pl.BlockSpec((1, ts, d), lambda i, j: (i, j, 0))] + [_resident(a.shape) for a in consts],
        out_specs=[pl.BlockSpec((1, ts, d), lambda i, j: (i, j, 0)),
                   pl.BlockSpec((1, CARRY_ROWS, d), lambda i, j: (i, 0, 0))],
        out_shape=[jax.ShapeDtypeStruct((b, s, d), F32), jax.ShapeDtypeStruct((b, CARRY_ROWS, d), F32)],
        scratch_shapes=[pltpu.VMEM((CARRY_ROWS + ts, d), F32)],
        compiler_params=_params("parallel", "arbitrary"),
        name="conv_prompt",
    )(x, *consts)


def _conv_s_body(x_ref, st_ref, g_ref, w1_ref, b1_ref, wdw_ref, bdw_ref, lng_ref, lnb_ref, w2_ref, b2_ref,
                 o_ref, ns_ref, *, nb, nt):
    x = x_ref[...]
    u = _glu(x, g_ref, w1_ref, b1_ref)
    n_prev = CONV_WIDTH - 1
    ext = [st_ref[i * nb:(i + 1) * nb, :] for i in range(n_prev)]
    ext += [u[t * nb:(t + 1) * nb, :] for t in range(nt)]
    ys = []
    for t in range(nt):
        y = None
        for k in range(CONV_WIDTH):
            term = wdw_ref[k:k + 1, :] * ext[t + k]
            y = term if y is None else y + term
        ys.append(y + bdw_ref[...])
    o_ref[...] = _conv_tail(jnp.concatenate(ys, axis=0), x, lng_ref, lnb_ref, w2_ref, b2_ref, g_ref)
    for i in range(n_prev):
        ns_ref[i * nb:(i + 1) * nb, :] = ext[nt + i]


def _conv_s(x, state, gains, w1, b1, wdw, bdw, lng, lnb, w2, b2, nb, nt):
    return pl.pallas_call(
        functools.partial(_conv_s_body, nb=nb, nt=nt),
        out_shape=[jax.ShapeDtypeStruct(x.shape, F32), jax.ShapeDtypeStruct(state.shape, F32)],
        compiler_params=pltpu.CompilerParams(vmem_limit_bytes=VMEM_LIMIT),
        name="conv_sample",
    )(x, state, gains, w1, b1, wdw, bdw, lng, lnb, w2, b2)


def _t5_bucket(dist):
    max_exact = N_BUCKETS // 2
    df = jnp.maximum(dist, 1).astype(F32)
    large = max_exact + (jnp.log(df / max_exact) / math.log(MAX_DISTANCE / max_exact)
                         * (N_BUCKETS - max_exact)).astype(jnp.int32)
    large = jnp.minimum(large, N_BUCKETS - 1)
    return jnp.where(dist < max_exact, dist, large)


def _bias_by_step(rel_bias, g, dil, n_steps, n_heads):
    tab = rel_bias[:, g * n_heads:(g + 1) * n_heads]
    return tab[_t5_bucket(dil * jnp.arange(n_steps + 1, dtype=jnp.int32))]


def _to_time_major(a):
    b, t, d = a.shape
    return a.transpose(1, 0, 2).reshape(t * b, d)


def _from_time_major(a, b):
    return a.reshape(-1, b, a.shape[1]).transpose(1, 0, 2)


def kernel(x_prompt, x_sample, state_pool, cache_kv_g0, cache_kv_g1, cache_kv_g2, state_conv, norm_gains, rel_bias, pool_w, pool_scale, attn_w_qkv, attn_w_o, conv_w_pw1, conv_b_pw1, conv_w_dw, conv_b_dw, conv_ln_g, conv_ln_b, conv_w_pw2, conv_b_pw2, ffn_w_up, ffn_w_down):
    batch, seq, d = x_prompt.shape
    dec_b, dec_t, _ = x_sample.shape
    depth = norm_gains.shape[0]
    n_heads = attn_w_o.shape[1] // HEAD_DIM
    e_dim = n_heads * HEAD_DIM
    caches_in = (cache_kv_g0, cache_kv_g1, cache_kv_g2)
    n_tok = batch * seq

    xp = x_prompt
    xs = _to_time_major(x_sample)

    expand = jnp.asarray(np.kron(np.eye(n_heads), np.ones((1, HEAD_DIM))), BF16)
    expand_t = jnp.asarray(np.kron(np.eye(n_heads), np.ones((HEAD_DIM, 1))), BF16)

    pool_p, pool_s, conv_p, conv_s = [], [], [], []
    kv_p = [[] for _ in ATTN_GROUPS]
    kv_s = [[] for _ in ATTN_GROUPS]
    for i in range(depth):
        kind, j = i % 3, i // 3
        gains = norm_gains[i]
        if kind == 0:
            pw, ps = pool_w[j].astype(BF16), pool_scale[j][None, :]
            xp, h_last = _pool_p(xp, gains, pw, ps, ts=512)
            pool_p.append(h_last[:, POOL_CARRY - POOL_STATE:])
            xs, new_state = _pool_s(xs, _to_time_major(state_pool[j]), gains, pw, ps,
                                    nb=dec_b, nt=dec_t, pos0=PAST_LEN)
            pool_s.append(_from_time_major(new_state, dec_b))
        elif kind == 1:
            w_qkv, w_o = attn_w_qkv[j].astype(BF16), attn_w_o[j].astype(BF16)
            qkv = _qkv(xp.reshape(n_tok, d), gains, w_qkv, tm=256)
            os_, lses = [], []
            for g, (win, dil) in enumerate(ATTN_GROUPS):
                n_steps = win // dil
                by_step = _bias_by_step(rel_bias, g, dil, n_steps, n_heads)
                nk = 2 * QB if seq // dil > QB else QB
                steps = np.arange(QB)[:, None] - np.arange(nk)[None, :] + (nk - QB)
                bias = by_step[np.clip(steps, 0, n_steps)].transpose(2, 0, 1)
                o, lse = _attn_p(qkv, bias, g, dil, n_steps, batch, seq, n_heads)
                os_.append(o)
                lses.append(lse)
                keep = min(win, seq)
                kv_p[g].append(qkv.reshape(batch, seq, 3, 3, n_heads, HEAD_DIM)[:, seq - keep:, g, 1:3])
            xp = _merge(xp.reshape(n_tok, d), os_, lses, expand, w_o, gains, tm=512).reshape(batch, seq, d)
            qkv_s = _qkv(xs, gains, w_qkv, tm=dec_t * dec_b)
            qkv_s = _from_time_major(qkv_s, dec_b)
            biases = []
            for g, (win, dil) in enumerate(ATTN_GROUPS):
                n_steps = win // dil
                by_step = _bias_by_step(rel_bias, g, dil, n_steps, n_heads)
                n_cache = caches_in[g].shape[2]
                if dil == 1:
                    t = np.arange(dec_t)
                    st_c = n_cache + t[:, None] - np.arange(n_cache)[None, :]
                    st_n = t[:, None] - t[None, :]
                    biases += [by_step[np.clip(st_c, 0, n_steps)], by_step[np.clip(st_n, 0, n_steps)]]
                else:
                    rows = n_cache // dil
                    st_c = np.clip(rows - np.arange(rows), 0, n_steps)
                    biases += [by_step[st_c], by_step[0:1]]
            caches = [c[j].reshape(dec_b, c.shape[2], 2 * e_dim) for c in caches_in]
            o_s = _attn_s(qkv_s, caches, biases, expand, expand_t, dec_t, e_dim)
            xs = _proj_s(xs, _to_time_major(o_s), w_o, gains)
            for g, (win, dil) in enumerate(ATTN_GROUPS):
                new_kv = qkv_s.reshape(dec_b, dec_t, 3, 3, n_heads, HEAD_DIM)[:, :, g, 1:3]
                ext = jnp.concatenate([caches_in[g][j], new_kv], axis=1)
                keep = min(win, ext.shape[1])
                kv_s[g].append(ext[:, ext.shape[1] - keep:])
        else:
            cw = (conv_w_pw1[j].astype(BF16), conv_b_pw1[j][None, :], conv_w_dw[j], conv_b_dw[j][None, :],
                  conv_ln_g[j][None, :], conv_ln_b[j][None, :], conv_w_pw2[j].astype(BF16), conv_b_pw2[j][None, :])
            xp, u_last = _conv_p(xp, gains, *cw, ts=512)
            conv_p.append(u_last[:, CARRY_ROWS - (CONV_WIDTH - 1):])
            xs, new_state = _conv_s(xs, _to_time_major(state_conv[j]), gains, *cw, nb=dec_b, nt=dec_t)
            conv_s.append(_from_time_major(new_state, dec_b))
        w_up, w_down = ffn_w_up[i].astype(BF16), ffn_w_down[i].astype(BF16)
        xp = _ffn(xp.reshape(n_tok, d), gains, w_up, w_down, tm=512).reshape(batch, seq, d)
        xs = _ffn(xs, gains, w_up, w_down, tm=dec_t * dec_b)
    return (xp, _from_time_major(xs, dec_b),
            jnp.stack(pool_p), jnp.stack(pool_s),
            jnp.stack(kv_p[0]), jnp.stack(kv_s[0]),
            jnp.stack(kv_p[1]), jnp.stack(kv_s[1]),
            jnp.stack(kv_p[2]), jnp.stack(kv_s[2]),
            jnp.stack(conv_p), jnp.stack(conv_s))
```

```python
import functools
import math

import jax
import jax.numpy as jnp
import numpy as np
from jax import lax
from jax.experimental import pallas as pl
from jax.experimental.pallas import tpu as pltpu

F32 = jnp.float32
BF16 = jnp.bfloat16

EPS = 1e-6
NEG_INF = -1e30
POOL_WINDOWS = (2, 4, 8, 16)
POOL_STATE = max(POOL_WINDOWS) - 1
PAST_LEN = 8192
ATTN_GROUPS = ((128, 1), (512, 4), (2048, 16))
HEAD_DIM = 64
QB = 128
N_BUCKETS = 32
MAX_DISTANCE = 2048
CONV_WIDTH = 31
CARRY_ROWS = 32
POOL_CARRY = 16
LANES = 128

VMEM_LIMIT = 56 * 1024 * 1024

NT_DIMS = (((1,), (1,)), ((), ()))


def _params(*sem):
    return pltpu.CompilerParams(dimension_semantics=sem, vmem_limit_bytes=VMEM_LIMIT)


def _resident(shape):
    nd = len(shape)
    return pl.BlockSpec(shape, lambda *_: (0,) * nd, pipeline_mode=pl.Buffered(1))


def _rms(x, g):
    return x * lax.rsqrt(jnp.mean(x * x, axis=-1, keepdims=True) + EPS) * g


def _dot(a, b):
    return jnp.dot(a, b, preferred_element_type=F32)


def _dot_nt(a, b):
    return lax.dot_general(a, b, NT_DIMS, preferred_element_type=F32)


def _ffn_body(x_ref, g_ref, wup_ref, wdn_ref, o_ref, *, f_chunk):
    x = x_ref[...]
    h = _rms(x, g_ref[2:3]).astype(BF16)
    acc = jnp.zeros_like(x)
    for c in range(wup_ref.shape[1] // f_chunk):
        a = jnp.maximum(_dot(h, wup_ref[:, c * f_chunk:(c + 1) * f_chunk]), 0.0)
        acc = acc + _dot((a * a).astype(BF16), wdn_ref[c * f_chunk:(c + 1) * f_chunk, :])
    o_ref[...] = x + _rms(acc, g_ref[3:4])


def _ffn(x, gains, w_up, w_down, tm):
    n, d = x.shape
    f = w_up.shape[1]
    return pl.pallas_call(
        functools.partial(_ffn_body, f_chunk=1024),
        grid=(n // tm,),
        in_specs=[pl.BlockSpec((tm, d), lambda i: (i, 0)), _resident(gains.shape),
                  _resident((d, f)), _resident((f, d))],
        out_specs=pl.BlockSpec((tm, d), lambda i: (i, 0)),
        out_shape=jax.ShapeDtypeStruct((n, d), F32),
        compiler_params=_params("parallel"),
        name="ffn",
    )(x, gains, w_up, w_down)


def _pool_p_body(x_ref, g_ref, pw_ref, ps_ref, o_ref, hl_ref, carry_ref, *, ts):
    j = pl.program_id(1)

    @pl.when(j == 0)
    def _():
        carry_ref[...] = jnp.zeros_like(carry_ref)

    x = x_ref[0]
    h = _rms(x, g_ref[0:1])
    ext = jnp.concatenate([carry_ref[...], h], axis=0)
    sums = []
    s = ext
    for k in (1, 2, 4, 8):
        s = s + pltpu.roll(s, k, axis=0)
        sums.append(s)
    pos = j * ts + lax.broadcasted_iota(jnp.int32, (ts, 1), 0)
    gw = pw_ref.shape[1]
    ys = []
    for gi, (w, s) in enumerate(zip(POOL_WINDOWS, sums)):
        cols = slice(gi * gw, (gi + 1) * gw)
        inv_cnt = 1.0 / jnp.minimum(pos + 1, w).astype(F32)
        pooled = s[POOL_CARRY:, cols] * inv_cnt - h[:, cols]
        ys.append(_dot(pooled.astype(BF16), pw_ref[gi]))
    y = jnp.concatenate(ys, axis=1) * ps_ref[...]
    o_ref[0] = x + _rms(y, g_ref[1:2])
    carry_ref[...] = h[ts - POOL_CARRY:, :]
    hl_ref[0] = h[ts - POOL_CARRY:, :]


def _pool_p(x, gains, pw, ps, ts):
    b, s, d = x.shape
    return pl.pallas_call(
        functools.partial(_pool_p_body, ts=ts),
        grid=(b, s // ts),
        in_specs=[pl.BlockSpec((1, ts, d), lambda i, j: (i, j, 0)), _resident(gains.shape),
                  _resident(pw.shape), _resident(ps.shape)],
        out_specs=[pl.BlockSpec((1, ts, d), lambda i, j: (i, j, 0)),
                   pl.BlockSpec((1, POOL_CARRY, d), lambda i, j: (i, 0, 0))],
        out_shape=[jax.ShapeDtypeStruct((b, s, d), F32),
                   jax.ShapeDtypeStruct((b, POOL_CARRY, d), F32)],
        scratch_shapes=[pltpu.VMEM((POOL_CARRY, d), F32)],
        compiler_params=_params("parallel", "arbitrary"),
        name="pool_prompt",
    )(x, gains, pw, ps)


def _pool_s_body(x_ref, st_ref, g_ref, pw_ref, ps_ref, o_ref, ns_ref, *, nb, nt, pos0):
    x = x_ref[...]
    h = _rms(x, g_ref[0:1])
    ext = [st_ref[i * nb:(i + 1) * nb, :] for i in range(POOL_STATE)]
    ext += [h[t * nb:(t + 1) * nb, :] for t in range(nt)]
    gw = pw_ref.shape[1]
    for t in range(nt):
        ys = []
        for gi, w in enumerate(POOL_WINDOWS):
            cols = slice(gi * gw, (gi + 1) * gw)
            end = POOL_STATE + t
            acc = ext[end][:, cols]
            for i in range(1, min(w, end + 1)):
                acc = acc + ext[end - i][:, cols]
            pooled = acc / float(min(pos0 + t + 1, w)) - ext[end][:, cols]
            ys.append(_dot(pooled.astype(BF16), pw_ref[gi]))
        y = jnp.concatenate(ys, axis=1) * ps_ref[...]
        o_ref[t * nb:(t + 1) * nb, :] = x[t * nb:(t + 1) * nb, :] + _rms(y, g_ref[1:2])
    for i in range(POOL_STATE):
        ns_ref[i * nb:(i + 1) * nb, :] = ext[nt + i]


def _pool_s(x, state, gains, pw, ps, nb, nt, pos0):
    return pl.pallas_call(
        functools.partial(_pool_s_body, nb=nb, nt=nt, pos0=pos0),
        out_shape=[jax.ShapeDtypeStruct(x.shape, F32), jax.ShapeDtypeStruct(state.shape, F32)],
        compiler_params=pltpu.CompilerParams(vmem_limit_bytes=VMEM_LIMIT),
        name="pool_sample",
    )(x, state, gains, pw, ps)


def _qkv_p_body(x_ref, g_ref, w_ref, *refs, dils, tm):
    out_refs, ybuf_ref = refs[:-1], refs[-1]
    e = out_refs[0].shape[-1]
    h = _rms(x_ref[0], g_ref[0:1]).astype(BF16)
    for g, dil in enumerate(dils):
        for c in range(3):
            idx = g * 3 + c
            y = _dot(h, w_ref[:, idx * e:(idx + 1) * e])
            if c == 0:
                y = y * (HEAD_DIM ** -0.5)
            if dil == 1:
                out_refs[idx][0, 0] = y.astype(BF16)
            else:
                slot = idx % 2
                for cc in range(e // LANES):
                    ybuf_ref[slot, cc] = y[:, cc * LANES:(cc + 1) * LANES]
                for r in range(dil):
                    out_refs[idx][0, r] = jnp.concatenate(
                        [ybuf_ref[slot, cc, pl.ds(r, tm // dil, stride=dil), :] for cc in range(e // LANES)],
                        axis=1).astype(BF16)


def _qkv_p(x, gains, w, dils, e, tm):
    b, s, d = x.shape
    out_specs, out_shape = [], []
    for dil in dils:
        for _ in range(3):
            out_specs.append(pl.BlockSpec((1, dil, tm // dil, e), lambda i, j: (i, 0, j, 0)))
            out_shape.append(jax.ShapeDtypeStruct((b, dil, s // dil, e), BF16))
    return pl.pallas_call(
        functools.partial(_qkv_p_body, dils=dils, tm=tm),
        grid=(b, s // tm),
        in_specs=[pl.BlockSpec((1, tm, d), lambda i, j: (i, j, 0)), _resident(gains.shape), _resident(w.shape)],
        out_specs=out_specs,
        out_shape=out_shape,
        scratch_shapes=[pltpu.VMEM((2, e // LANES, tm, LANES), F32)],
        compiler_params=_params("parallel", "parallel"),
        name="qkv_prompt",
    )(x, gains, w)


def _kvt_body(x_ref, g_ref, wt_ref, o_ref):
    h = _rms(x_ref[0], g_ref[0:1]).astype(BF16)
    o_ref[0] = _dot_nt(wt_ref[...], h)


def _kvt(x, gains, wt, row_block, n_rows, keep, tm):
    b, s, d = x.shape
    first = (s - keep) // tm
    return pl.pallas_call(
        _kvt_body,
        grid=(b, keep // tm),
        in_specs=[pl.BlockSpec((1, tm, d), lambda i, j: (i, first + j, 0)), _resident(gains.shape),
                  pl.BlockSpec((n_rows, d), lambda i, j: (row_block, 0), pipeline_mode=pl.Buffered(1))],
        out_specs=pl.BlockSpec((1, n_rows, tm), lambda i, j: (i, 0, j)),
        out_shape=jax.ShapeDtypeStruct((b, n_rows, keep), F32),
        compiler_params=_params("parallel", "parallel"),
        name="kv_transposed",
    )(x, gains, wt)


def _q_s_body(x_ref, g_ref, wq_ref, o_ref):
    h = _rms(x_ref[...], g_ref[0:1]).astype(BF16)
    o_ref[...] = _dot(h, wq_ref[...]) * (HEAD_DIM ** -0.5)


def _q_s(x, gains, wq):
    return pl.pallas_call(
        _q_s_body,
        out_shape=jax.ShapeDtypeStruct((x.shape[0], wq.shape[1]), F32),
        compiler_params=pltpu.CompilerParams(vmem_limit_bytes=VMEM_LIMIT),
        name="q_sample",
    )(x, gains, wq)


def _attn_p_body(*refs, n_steps, has_prev, n_heads):
    if has_prev:
        q_ref, kp_ref, kc_ref, vp_ref, vc_ref, bias_ref, o_ref, lse_ref = refs
        k = jnp.concatenate([kp_ref[0, 0], kc_ref[0, 0]], axis=0)
        v = jnp.concatenate([vp_ref[0, 0], vc_ref[0, 0]], axis=0)
    else:
        q_ref, kc_ref, vc_ref, bias_ref, o_ref, lse_ref = refs
        k = kc_ref[0, 0]
        v = vc_ref[0, 0]
    n = pl.program_id(2)
    q = q_ref[0, 0]
    nk = k.shape[0]
    off = QB if has_prev else 0
    qi = lax.broadcasted_iota(jnp.int32, (QB, nk), 0)
    kj = lax.broadcasted_iota(jnp.int32, (QB, nk), 1)
    steps = qi - kj + off
    mask = (steps >= 0) & (steps <= n_steps)
    if has_prev:
        mask = mask & (n * QB + kj >= QB)
    head = lax.broadcasted_iota(jnp.int32, (QB, LANES), 1)
    lse = jnp.zeros((QB, LANES), F32)
    for h in range(n_heads):
        sl = slice(h * HEAD_DIM, (h + 1) * HEAD_DIM)
        s = _dot_nt(q[:, sl], k[:, sl])
        s = jnp.where(mask, s + bias_ref[h], NEG_INF)
        m = jnp.max(s, axis=-1, keepdims=True)
        e = jnp.exp(s - m)
        l = jnp.sum(e, axis=-1, keepdims=True)
        o_ref[0, 0, :, sl] = _dot(e.astype(BF16), v[:, sl]) / l
        lse = jnp.where(head == h, m + jnp.log(l), lse)
    lse_ref[0, 0] = lse


def _attn_p(q, k, v, bias, g, n_steps, n_heads):
    batch, dil, sub, e = q.shape
    nb = sub // QB
    has_prev = nb > 1
    blk = (1, 1, QB, e)
    cur = lambda b, r, n: (b, r, n, 0)
    prev = lambda b, r, n: (b, r, jnp.maximum(n - 1, 0), 0)
    if has_prev:
        in_specs = [pl.BlockSpec(blk, cur), pl.BlockSpec(blk, prev), pl.BlockSpec(blk, cur),
                    pl.BlockSpec(blk, prev), pl.BlockSpec(blk, cur)]
        args = (q, k, k, v, v)
    else:
        in_specs = [pl.BlockSpec(blk, cur)] * 3
        args = (q, k, v)
    return pl.pallas_call(
        functools.partial(_attn_p_body, n_steps=n_steps, has_prev=has_prev, n_heads=n_heads),
        grid=(batch, dil, nb),
        in_specs=in_specs + [_resident(bias.shape)],
        out_specs=[pl.BlockSpec(blk, cur), pl.BlockSpec((1, 1, QB, LANES), cur)],
        out_shape=[jax.ShapeDtypeStruct((batch, dil, sub, e), F32),
                   jax.ShapeDtypeStruct((batch, dil, sub, LANES), F32)],
        compiler_params=_params("parallel", "parallel", "arbitrary"),
        name=f"attn_prompt_g{g}",
    )(*args, bias)


def _expand_heads(w, e_ref):
    hi = w.astype(BF16)
    lo = (w - hi.astype(F32)).astype(BF16)
    return _dot(hi, e_ref[...]) + _dot(lo, e_ref[...])


def _merge_groups(os_, lses, e_ref):
    m = functools.reduce(jnp.maximum, lses)
    es = [jnp.exp(l - m) for l in lses]
    den = functools.reduce(lambda a, b: a + b, es)
    acc = None
    for o, e in zip(os_, es):
        term = _expand_heads(e / den, e_ref) * o
        acc = term if acc is None else acc + term
    return acc


def _merge_body(x_ref, *refs, dils, tm):
    ng = len(dils)
    o_refs, l_refs = refs[:ng], refs[ng:2 * ng]
    e_ref, wo_ref, g_ref, out_ref, obuf_ref, lbuf_ref = refs[2 * ng:]
    os_, lses = [], []
    for gi, dil in enumerate(dils):
        if dil == 1:
            os_.append(o_refs[gi][0, 0])
            lses.append(l_refs[gi][0, 0])
        else:
            n_cc = obuf_ref.shape[1]
            for r in range(dil):
                rows = pl.ds(r, tm // dil, stride=dil)
                for cc in range(n_cc):
                    obuf_ref[gi, cc, rows, :] = o_refs[gi][0, r, :, cc * LANES:(cc + 1) * LANES]
                lbuf_ref[gi, rows, :] = l_refs[gi][0, r]
            os_.append(jnp.concatenate([obuf_ref[gi, cc] for cc in range(n_cc)], axis=1))
            lses.append(lbuf_ref[gi])
    o = _merge_groups(os_, lses, e_ref)
    y = _dot(o.astype(BF16), wo_ref[...])
    out_ref[0] = x_ref[0] + _rms(y, g_ref[1:2])


def _merge(x, os_, lses, expand, w_o, gains, tm):
    b, s, d = x.shape
    dils = tuple(o.shape[1] for o in os_)
    e, nh = os_[0].shape[-1], lses[0].shape[-1]
    grp = lambda i, j: (i, 0, j, 0)
    return pl.pallas_call(
        functools.partial(_merge_body, dils=dils, tm=tm),
        grid=(b, s // tm),
        in_specs=[pl.BlockSpec((1, tm, d), lambda i, j: (i, j, 0))]
                 + [pl.BlockSpec((1, dil, tm // dil, e), grp) for dil in dils]
                 + [pl.BlockSpec((1, dil, tm // dil, nh), grp) for dil in dils]
                 + [_resident(expand.shape), _resident(w_o.shape), _resident(gains.shape)],
        out_specs=pl.BlockSpec((1, tm, d), lambda i, j: (i, j, 0)),
        out_shape=jax.ShapeDtypeStruct((b, s, d), F32),
        scratch_shapes=[pltpu.VMEM((len(dils), e // LANES, tm, LANES), F32),
                        pltpu.VMEM((len(dils), tm, nh), F32)],
        compiler_params=_params("parallel", "parallel"),
        name="attn_merge",
    )(x, *os_, *lses, expand, w_o, gains)


def _attn_s_body(q_ref, c_ref, knew_ref, vnew_ref, bc_ref, bn_ref, o_ref, lse_ref, cout_ref,
                 *, dil, n_steps, nt, hc):
    b = pl.program_id(0)
    n = c_ref.shape[-1]
    q = q_ref[0].astype(BF16)
    delta = n + lax.broadcasted_iota(jnp.int32, (nt, n), 0) - lax.broadcasted_iota(jnp.int32, (nt, n), 1)
    valid_c = (lax.rem(delta, dil) == 0) & (delta <= dil * n_steps)
    lane = lax.broadcasted_iota(jnp.int32, (nt, LANES), 1)
    dn = lax.broadcasted_iota(jnp.int32, (nt, LANES), 0) - lax.rem(lane, nt)
    valid_n = (lane // nt == b) & (dn >= 0) & (lax.rem(dn, dil) == 0) & (dn <= dil * n_steps)
    tail = lax.broadcasted_iota(jnp.int32, (HEAD_DIM, LANES), 1) >= LANES - nt
    shift = LANES - nt - b * nt
    head = lax.broadcasted_iota(jnp.int32, (nt, hc), 1)
    lse = jnp.zeros((nt, hc), F32)
    for hh in range(hc):
        sl = slice(hh * HEAD_DIM, (hh + 1) * HEAD_DIM)
        kt, vt = c_ref[0, 0, hh], c_ref[0, 1, hh]
        kn, vn = knew_ref[sl, :], vnew_ref[sl, :]
        s_c = jnp.where(valid_c, _dot(q[:, sl], kt.astype(BF16)) + bc_ref[hh], NEG_INF)
        s_n = jnp.where(valid_n, _dot(q[:, sl], kn.astype(BF16)) + bn_ref[hh], NEG_INF)
        m = jnp.maximum(jnp.max(s_c, axis=-1, keepdims=True), jnp.max(s_n, axis=-1, keepdims=True))
        e_c = jnp.exp(s_c - m)
        e_n = jnp.exp(s_n - m)
        l = jnp.sum(e_c, axis=-1, keepdims=True) + jnp.sum(e_n, axis=-1, keepdims=True)
        o = _dot_nt((e_c / l).astype(BF16), vt.astype(BF16)) + _dot_nt((e_n / l).astype(BF16), vn.astype(BF16))
        o_ref[0, :, sl] = o
        lse = jnp.where(head == hh, m + jnp.log(l), lse)
        for kv, (old, new) in enumerate(((kt, kn), (vt, vn))):
            rolled = pltpu.roll(old, n - nt, axis=1)
            placed = pltpu.roll(new, shift, axis=1)
            if n > LANES:
                cout_ref[0, kv, hh, :, 0:n - LANES] = rolled[:, 0:n - LANES]
            cout_ref[0, kv, hh, :, n - LANES:n] = jnp.where(tail, placed, rolled[:, n - LANES:n])
    lse_ref[0, 0] = lse


def _attn_s(q, cache, kvt_new, bias_c, bias_n, g, dil, n_steps, nt, hc):
    b, _, n_heads, _, n = cache.shape
    assert n == dil * n_steps and kvt_new.shape[1] == LANES == b * nt
    e = n_heads * HEAD_DIM
    nc = n_heads // hc
    cw = hc * HEAD_DIM
    return pl.pallas_call(
        functools.partial(_attn_s_body, dil=dil, n_steps=n_steps, nt=nt, hc=hc),
        grid=(b, nc),
        in_specs=[pl.BlockSpec((1, nt, cw), lambda i, c: (i, 0, g * nc + c)),
                  pl.BlockSpec((1, 2, hc, HEAD_DIM, n), lambda i, c: (i, 0, c, 0, 0)),
                  pl.BlockSpec((cw, LANES), lambda i, c: ((g * 2) * nc + c, 0)),
                  pl.BlockSpec((cw, LANES), lambda i, c: ((g * 2 + 1) * nc + c, 0)),
                  pl.BlockSpec((hc, nt, n), lambda i, c: (c, 0, 0)),
                  pl.BlockSpec((hc, nt, LANES), lambda i, c: (c, 0, 0))],
        out_specs=[pl.BlockSpec((1, nt, cw), lambda i, c: (i, 0, c)),
                   pl.BlockSpec((1, 1, nt, hc), lambda i, c: (i, c, 0, 0)),
                   pl.BlockSpec((1, 2, hc, HEAD_DIM, n), lambda i, c: (i, 0, c, 0, 0))],
        out_shape=[jax.ShapeDtypeStruct((b, nt, e), F32),
                   jax.ShapeDtypeStruct((b, nc, nt, hc), F32),
                   jax.ShapeDtypeStruct(cache.shape, F32)],
        compiler_params=_params("parallel", "arbitrary"),
        name=f"attn_sample_g{g}",
    )(q, cache, kvt_new, kvt_new, bias_c, bias_n)


def _conv_tail(y, x, lng_ref, lnb_ref, w2_ref, b2_ref, g_ref):
    mu = jnp.mean(y, axis=-1, keepdims=True)
    yc = y - mu
    var = jnp.mean(yc * yc, axis=-1, keepdims=True)
    yn = yc * lax.rsqrt(var + EPS) * lng_ref[...] + lnb_ref[...]
    z = _dot((yn * jax.nn.sigmoid(yn)).astype(BF16), w2_ref[...]) + b2_ref[...]
    return x + _rms(z, g_ref[1:2])


def _glu(x, g_ref, w1_ref, b1_ref):
    d = x.shape[1]
    a = _dot(_rms(x, g_ref[0:1]).astype(BF16), w1_ref[...]) + b1_ref[...]
    return a[:, :d] * jax.nn.sigmoid(a[:, d:])


def _conv_p_body(x_ref, g_ref, w1_ref, b1_ref, wdw_ref, bdw_ref, lng_ref, lnb_ref, w2_ref, b2_ref,
                 o_ref, ul_ref, uext_ref, *, ts):
    j = pl.program_id(1)

    @pl.when(j == 0)
    def _():
        uext_ref[0:CARRY_ROWS, :] = jnp.zeros((CARRY_ROWS, uext_ref.shape[1]), F32)

    x = x_ref[0]
    uext_ref[CARRY_ROWS:, :] = _glu(x, g_ref, w1_ref, b1_ref)
    lead = CARRY_ROWS - (CONV_WIDTH - 1)
    y = None
    for k in range(CONV_WIDTH):
        term = wdw_ref[k:k + 1, :] * uext_ref[lead + k:lead + k + ts, :]
        y = term if y is None else y + term
    y = y + bdw_ref[...]
    o_ref[0] = _conv_tail(y, x, lng_ref, lnb_ref, w2_ref, b2_ref, g_ref)
    last = uext_ref[ts:ts + CARRY_ROWS, :]
    uext_ref[0:CARRY_ROWS, :] = last
    ul_ref[0] = last


def _conv_p(x, gains, w1, b1, wdw, bdw, lng, lnb, w2, b2, ts):
    b, s, d = x.shape
    consts = (gains, w1, b1, wdw, bdw, lng, lnb, w2, b2)
    return pl.pallas_call(
        functools.partial(_conv_p_body, ts=ts),
        grid=(b, s // ts),
        in_specs=[pl.BlockSpec((1, ts, d), lambda i, j: (i, j, 0))] + [_resident(a.shape) for a in consts],
        out_specs=[pl.BlockSpec((1, ts, d), lambda i, j: (i, j, 0)),
                   pl.BlockSpec((1, CARRY_ROWS, d), lambda i, j: (i, 0, 0))],
        out_shape=[jax.ShapeDtypeStruct((b, s, d), F32), jax.ShapeDtypeStruct((b, CARRY_ROWS, d), F32)],
        scratch_shapes=[pltpu.VMEM((CARRY_ROWS + ts, d), F32)],
        compiler_params=_params("parallel", "arbitrary"),
        name="conv_prompt",
    )(x, *consts)


def _conv_s_body(x_ref, st_ref, g_ref, w1_ref, b1_ref, wdw_ref, bdw_ref, lng_ref, lnb_ref, w2_ref, b2_ref,
                 o_ref, ns_ref, *, nb, nt):
    x = x_ref[...]
    u = _glu(x, g_ref, w1_ref, b1_ref)
    n_prev = CONV_WIDTH - 1
    ext = [st_ref[i * nb:(i + 1) * nb, :] for i in range(n_prev)]
    ext += [u[t * nb:(t + 1) * nb, :] for t in range(nt)]
    ys = []
    for t in range(nt):
        y = None
        for k in range(CONV_WIDTH):
            term = wdw_ref[k:k + 1, :] * ext[t + k]
            y = term if y is None else y + term
        ys.append(y + bdw_ref[...])
    o_ref[...] = _conv_tail(jnp.concatenate(ys, axis=0), x, lng_ref, lnb_ref, w2_ref, b2_ref, g_ref)
    for i in range(n_prev):
        ns_ref[i * nb:(i + 1) * nb, :] = ext[nt + i]


def _conv_s(x, state, gains, w1, b1, wdw, bdw, lng, lnb, w2, b2, nb, nt):
    return pl.pallas_call(
        functools.partial(_conv_s_body, nb=nb, nt=nt),
        out_shape=[jax.ShapeDtypeStruct(x.shape, F32), jax.ShapeDtypeStruct(state.shape, F32)],
        compiler_params=pltpu.CompilerParams(vmem_limit_bytes=VMEM_LIMIT),
        name="conv_sample",
    )(x, state, gains, w1, b1, wdw, bdw, lng, lnb, w2, b2)


def _t5_bucket(dist):
    max_exact = N_BUCKETS // 2
    df = jnp.maximum(dist, 1).astype(F32)
    large = max_exact + (jnp.log(df / max_exact) / math.log(MAX_DISTANCE / max_exact)
                         * (N_BUCKETS - max_exact)).astype(jnp.int32)
    large = jnp.minimum(large, N_BUCKETS - 1)
    return jnp.where(dist < max_exact, dist, large)


def _bias_by_step(rel_bias, g, dil, n_steps, n_heads):
    tab = rel_bias[:, g * n_heads:(g + 1) * n_heads]
    return tab[_t5_bucket(dil * jnp.arange(n_steps + 1, dtype=jnp.int32))]


def _time_major(a):
    b, t, d = a.shape
    return a.transpose(1, 0, 2).reshape(t * b, d)


def _batch_major(a, b):
    return a.reshape(-1, b, a.shape[1]).transpose(1, 0, 2)


def _kv_window_layout(c):
    return c.transpose(0, 2, 3, 4, 1)


def _kv_window_unlayout(c):
    return c.transpose(0, 4, 1, 2, 3)


def kernel(x_prompt, x_sample, state_pool, cache_kv_g0, cache_kv_g1, cache_kv_g2, state_conv, norm_gains, rel_bias, pool_w, pool_scale, attn_w_qkv, attn_w_o, conv_w_pw1, conv_b_pw1, conv_w_dw, conv_b_dw, conv_ln_g, conv_ln_b, conv_w_pw2, conv_b_pw2, ffn_w_up, ffn_w_down):
    batch, seq, d = x_prompt.shape
    dec_b, dec_t, _ = x_sample.shape
    depth = norm_gains.shape[0]
    n_heads = attn_w_o.shape[1] // HEAD_DIM
    e_dim = n_heads * HEAD_DIM
    n_groups = len(ATTN_GROUPS)
    dils = tuple(dil for _, dil in ATTN_GROUPS)
    caches_in = (cache_kv_g0, cache_kv_g1, cache_kv_g2)
    n_tok = batch * seq
    n_dec = dec_b * dec_t

    xp = x_prompt
    xs = _time_major(x_sample)

    expand = jnp.asarray(np.kron(np.eye(LANES, n_heads), np.ones((1, HEAD_DIM))), BF16)

    pool_p, pool_s, conv_p, conv_s = [], [], [], []
    kv_p = [[] for _ in ATTN_GROUPS]
    kv_s = [[] for _ in ATTN_GROUPS]
    for i in range(depth):
        kind, j = i % 3, i // 3
        gains = norm_gains[i]
        if kind == 0:
            pw, ps = pool_w[j].astype(BF16), pool_scale[j][None, :]
            xp, h_last = _pool_p(xp, gains, pw, ps, ts=512)
            pool_p.append(h_last[:, POOL_CARRY - POOL_STATE:])
            xs, new_state = _pool_s(xs, _time_major(state_pool[j]), gains, pw, ps,
                                    nb=dec_b, nt=dec_t, pos0=PAST_LEN)
            pool_s.append(_batch_major(new_state, dec_b))
        elif kind == 1:
            w4 = attn_w_qkv[j].reshape(d, n_groups, 3, e_dim)
            w_qkv = attn_w_qkv[j].astype(BF16)
            w_q = w4[:, :, 0].reshape(d, n_groups * e_dim).astype(BF16)
            wt_kv = w4[:, :, 1:3].reshape(d, n_groups * 2 * e_dim).T.astype(BF16)
            w_o = attn_w_o[j].astype(BF16)
            by_step = [_bias_by_step(rel_bias, g, dil, win // dil, n_heads)
                       for g, (win, dil) in enumerate(ATTN_GROUPS)]
            qkv = _qkv_p(xp, gains, w_qkv, dils, e_dim, tm=512)
            os_, lses = [], []
            for g, (win, dil) in enumerate(ATTN_GROUPS):
                n_steps = win // dil
                nk = 2 * QB if seq // dil > QB else QB
                steps = np.arange(QB)[:, None] - np.arange(nk)[None, :] + (nk - QB)
                bias = by_step[g][np.clip(steps, 0, n_steps)].transpose(2, 0, 1)
                o, lse = _attn_p(qkv[3 * g], qkv[3 * g + 1], qkv[3 * g + 2], bias, g, n_steps, n_heads)
                os_.append(o)
                lses.append(lse)
                keep = min(win, seq)
                kvt = _kvt(xp, gains, wt_kv, g, 2 * e_dim, keep, tm=min(keep, 512))
                kv_p[g].append(_kv_window_unlayout(kvt.reshape(batch, 2, n_heads, HEAD_DIM, keep)))
            xp = _merge(xp, os_, lses, expand, w_o, gains, tm=512)
            xs_b = _batch_major(xs, dec_b)
            xs_flat = xs_b.reshape(n_dec, d)
            q_s = _q_s(xs_flat, gains, w_q).reshape(dec_b, dec_t, n_groups * e_dim)
            kvt_new = _kvt(xs_flat[None], gains, wt_kv, 0, n_groups * 2 * e_dim, n_dec, tm=n_dec)[0]
            os_, lses = [], []
            for g, (win, dil) in enumerate(ATTN_GROUPS):
                n_steps = win // dil
                cache = _kv_window_layout(caches_in[g][j])
                n = cache.shape[-1]
                t = np.arange(dec_t)
                step_c = (n + t[:, None] - np.arange(n)[None, :]) // dil
                step_n = (t[:, None] - (np.arange(LANES) % dec_t)[None, :]) // dil
                bias_c = by_step[g][np.clip(step_c, 0, n_steps)].transpose(2, 0, 1)
                bias_n = by_step[g][np.clip(step_n, 0, n_steps)].transpose(2, 0, 1)
                o, lse, new_cache = _attn_s(q_s, cache, kvt_new, bias_c, bias_n, g, dil, n_steps, dec_t, hc=4)
                os_.append(o.reshape(1, 1, n_dec, e_dim))
                lse = lse.transpose(0, 2, 1, 3).reshape(1, 1, n_dec, n_heads)
                lses.append(jnp.pad(lse, ((0, 0), (0, 0), (0, 0), (0, LANES - n_heads))))
                kv_s[g].append(_kv_window_unlayout(new_cache))
            xs_flat = _merge(xs_flat[None], os_, lses, expand, w_o, gains, tm=n_dec)[0]
            xs = _time_major(xs_flat.reshape(dec_b, dec_t, d))
        else:
            cw = (conv_w_pw1[j].astype(BF16), conv_b_pw1[j][None, :], conv_w_dw[j], conv_b_dw[j][None, :],
                  conv_ln_g[j][None, :], conv_ln_b[j][None, :], conv_w_pw2[j].astype(BF16), conv_b_pw2[j][None, :])
            xp, u_last = _conv_p(xp, gains, *cw, ts=512)
            conv_p.append(u_last[:, CARRY_ROWS - (CONV_WIDTH - 1):])
            xs, new_state = _conv_s(xs, _time_major(state_conv[j]), gains, *cw, nb=dec_b, nt=dec_t)
            conv_s.append(_batch_major(new_state, dec_b))
        w_up, w_down = ffn_w_up[i].astype(BF16), ffn_w_down[i].astype(BF16)
        xp = _ffn(xp.reshape(n_tok, d), gains, w_up, w_down, tm=512).reshape(batch, seq, d)
        xs = _ffn(xs, gains, w_up, w_down, tm=n_dec)
    return (xp, _batch_major(xs, dec_b),
            jnp.stack(pool_p), jnp.stack(pool_s),
            jnp.stack(kv_p[0]), jnp.stack(kv_s[0]),
            jnp.stack(kv_p[1]), jnp.stack(kv_s[1]),
            jnp.stack(kv_p[2]), jnp.stack(kv_s[2]),
            jnp.stack(conv_p), jnp.stack(conv_s))
```

```python
import functools
import math

import jax
import jax.numpy as jnp
import numpy as np
from jax import lax
from jax.experimental import pallas as pl
from jax.experimental.pallas import tpu as pltpu

F32 = jnp.float32
BF16 = jnp.bfloat16

EPS = 1e-6
NEG_INF = -1e30
POOL_WINDOWS = (2, 4, 8, 16)
POOL_STATE = max(POOL_WINDOWS) - 1
PAST_LEN = 8192
ATTN_GROUPS = ((128, 1), (512, 4), (2048, 16))
HEAD_DIM = 64
QB = 128
ATTN_AHEAD = 3
N_BUCKETS = 32
MAX_DISTANCE = 2048
CONV_WIDTH = 31
CARRY_ROWS = 32
POOL_CARRY = 16
LANES = 128
SUBLANES = 8
CONV_CHUNK_ROWS = 64
CONV_COPY_ROWS = 128

VMEM_LIMIT = 56 * 1024 * 1024
SAMPLE_WINDOW_BLOCK_BYTES = 8 * 1024 * 1024

NT_DIMS = (((1,), (1,)), ((), ()))


def _params(*sem):
    return pltpu.CompilerParams(dimension_semantics=sem, vmem_limit_bytes=VMEM_LIMIT)


def _resident(shape):
    nd = len(shape)
    return pl.BlockSpec(shape, lambda *_: (0,) * nd, pipeline_mode=pl.Buffered(1))


def _rms(x, g):
    return x * lax.rsqrt(jnp.mean(x * x, axis=-1, keepdims=True) + EPS) * g


def _dot(a, b):
    return jnp.dot(a, b, preferred_element_type=F32)


def _dot_nt(a, b):
    return lax.dot_general(a, b, NT_DIMS, preferred_element_type=F32)


def _ffn_body(x_ref, g_ref, wup_ref, wdn_ref, o_ref, *, f_chunk):
    x = x_ref[...]
    h = _rms(x, g_ref[2:3]).astype(BF16)
    acc = jnp.zeros_like(x)
    for c in range(wup_ref.shape[1] // f_chunk):
        a = jnp.maximum(_dot(h, wup_ref[:, c * f_chunk:(c + 1) * f_chunk]), 0.0)
        acc = acc + _dot((a * a).astype(BF16), wdn_ref[c * f_chunk:(c + 1) * f_chunk, :])
    o_ref[...] = x + _rms(acc, g_ref[3:4])


def _ffn(x, gains, w_up, w_down, tm):
    n, d = x.shape
    f = w_up.shape[1]
    return pl.pallas_call(
        functools.partial(_ffn_body, f_chunk=1024),
        grid=(n // tm,),
        in_specs=[pl.BlockSpec((tm, d), lambda i: (i, 0)), _resident(gains.shape),
                  _resident((d, f)), _resident((f, d))],
        out_specs=pl.BlockSpec((tm, d), lambda i: (i, 0)),
        out_shape=jax.ShapeDtypeStruct((n, d), F32),
        compiler_params=_params("parallel"),
        name="ffn",
    )(x, gains, w_up, w_down)


def _pool_p_body(x_ref, g_ref, pw_ref, ps_ref, o_ref, hl_ref, carry_ref, *, ts):
    j = pl.program_id(1)

    @pl.when(j == 0)
    def _():
        carry_ref[...] = jnp.zeros_like(carry_ref)

    x = x_ref[0]
    h = _rms(x, g_ref[0:1])
    ext = jnp.concatenate([carry_ref[...], h], axis=0)
    sums = []
    s = ext
    for k in (1, 2, 4, 8):
        s = s + pltpu.roll(s, k, axis=0)
        sums.append(s)
    pos = j * ts + lax.broadcasted_iota(jnp.int32, (ts, 1), 0)
    gw = pw_ref.shape[1]
    ys = []
    for gi, (w, s) in enumerate(zip(POOL_WINDOWS, sums)):
        cols = slice(gi * gw, (gi + 1) * gw)
        inv_cnt = 1.0 / jnp.minimum(pos + 1, w).astype(F32)
        pooled = s[POOL_CARRY:, cols] * inv_cnt - h[:, cols]
        ys.append(_dot(pooled.astype(BF16), pw_ref[gi]))
    y = jnp.concatenate(ys, axis=1) * ps_ref[...]
    o_ref[0] = x + _rms(y, g_ref[1:2])
    carry_ref[...] = h[ts - POOL_CARRY:, :]
    hl_ref[0] = h[ts - POOL_CARRY:, :]


def _pool_p(x, gains, pw, ps, ts):
    b, s, d = x.shape
    return pl.pallas_call(
        functools.partial(_pool_p_body, ts=ts),
        grid=(b, s // ts),
        in_specs=[pl.BlockSpec((1, ts, d), lambda i, j: (i, j, 0)), _resident(gains.shape),
                  _resident(pw.shape), _resident(ps.shape)],
        out_specs=[pl.BlockSpec((1, ts, d), lambda i, j: (i, j, 0)),
                   pl.BlockSpec((1, POOL_CARRY, d), lambda i, j: (i, 0, 0))],
        out_shape=[jax.ShapeDtypeStruct((b, s, d), F32),
                   jax.ShapeDtypeStruct((b, POOL_CARRY, d), F32)],
        scratch_shapes=[pltpu.VMEM((POOL_CARRY, d), F32)],
        compiler_params=_params("parallel", "arbitrary"),
        name="pool_prompt",
    )(x, gains, pw, ps)


def _pool_s_body(x_ref, st_ref, g_ref, pw_ref, ps_ref, o_ref, ns_ref, *, nb, nt, pos0):
    x = x_ref[...]
    h = _rms(x, g_ref[0:1])
    ext = [st_ref[i * nb:(i + 1) * nb, :] for i in range(POOL_STATE)]
    ext += [h[t * nb:(t + 1) * nb, :] for t in range(nt)]
    gw = pw_ref.shape[1]
    for t in range(nt):
        ys = []
        for gi, w in enumerate(POOL_WINDOWS):
            cols = slice(gi * gw, (gi + 1) * gw)
            end = POOL_STATE + t
            acc = ext[end][:, cols]
            for i in range(1, min(w, end + 1)):
                acc = acc + ext[end - i][:, cols]
            pooled = acc / float(min(pos0 + t + 1, w)) - ext[end][:, cols]
            ys.append(_dot(pooled.astype(BF16), pw_ref[gi]))
        y = jnp.concatenate(ys, axis=1) * ps_ref[...]
        o_ref[t * nb:(t + 1) * nb, :] = x[t * nb:(t + 1) * nb, :] + _rms(y, g_ref[1:2])
    for i in range(POOL_STATE):
        ns_ref[i * nb:(i + 1) * nb, :] = ext[nt + i]


def _pool_s(x, state, gains, pw, ps, nb, nt, pos0):
    return pl.pallas_call(
        functools.partial(_pool_s_body, nb=nb, nt=nt, pos0=pos0),
        out_shape=[jax.ShapeDtypeStruct(x.shape, F32), jax.ShapeDtypeStruct(state.shape, F32)],
        compiler_params=pltpu.CompilerParams(vmem_limit_bytes=VMEM_LIMIT),
        name="pool_sample",
    )(x, state, gains, pw, ps)


def _qkv_p_body(x_ref, g_ref, w_ref, *refs, dils, tm):
    out_refs, ybuf_ref = refs[:-1], refs[-1]
    e = out_refs[0].shape[-1]
    h = _rms(x_ref[0], g_ref[0:1]).astype(BF16)
    for g, dil in enumerate(dils):
        for c in range(3):
            idx = g * 3 + c
            y = _dot(h, w_ref[:, idx * e:(idx + 1) * e])
            if c == 0:
                y = y * (HEAD_DIM ** -0.5)
            if dil == 1:
                out_refs[idx][0, 0] = y.astype(BF16)
            else:
                slot = idx % 2
                for cc in range(e // LANES):
                    ybuf_ref[slot, cc] = y[:, cc * LANES:(cc + 1) * LANES]
                for r in range(dil):
                    out_refs[idx][0, r] = jnp.concatenate(
                        [ybuf_ref[slot, cc, pl.ds(r, tm // dil, stride=dil), :] for cc in range(e // LANES)],
                        axis=1).astype(BF16)


def _qkv_p(x, gains, w, dils, e, tm):
    b, s, d = x.shape
    out_specs, out_shape = [], []
    for dil in dils:
        for _ in range(3):
            out_specs.append(pl.BlockSpec((1, dil, tm // dil, e), lambda i, j: (i, 0, j, 0)))
            out_shape.append(jax.ShapeDtypeStruct((b, dil, s // dil, e), BF16))
    return pl.pallas_call(
        functools.partial(_qkv_p_body, dils=dils, tm=tm),
        grid=(b, s // tm),
        in_specs=[pl.BlockSpec((1, tm, d), lambda i, j: (i, j, 0)), _resident(gains.shape), _resident(w.shape)],
        out_specs=out_specs,
        out_shape=out_shape,
        scratch_shapes=[pltpu.VMEM((2, e // LANES, tm, LANES), F32)],
        compiler_params=_params("parallel", "parallel"),
        name="qkv_prompt",
    )(x, gains, w)


def _kvt_body(x_ref, g_ref, wt_ref, o_ref):
    h = _rms(x_ref[0], g_ref[0:1]).astype(BF16)
    o_ref[0] = _dot_nt(wt_ref[...], h)


def _kvt(x, gains, wt, row_block, n_rows, keep, tm):
    b, s, d = x.shape
    first = (s - keep) // tm
    return pl.pallas_call(
        _kvt_body,
        grid=(b, keep // tm),
        in_specs=[pl.BlockSpec((1, tm, d), lambda i, j: (i, first + j, 0)), _resident(gains.shape),
                  pl.BlockSpec((n_rows, d), lambda i, j: (row_block, 0), pipeline_mode=pl.Buffered(1))],
        out_specs=pl.BlockSpec((1, n_rows, tm), lambda i, j: (i, 0, j)),
        out_shape=jax.ShapeDtypeStruct((b, n_rows, keep), F32),
        compiler_params=_params("parallel", "parallel"),
        name="kv_transposed",
    )(x, gains, wt)


def _q_s_body(x_ref, g_ref, wq_ref, o_ref):
    h = _rms(x_ref[...], g_ref[0:1]).astype(BF16)
    o_ref[...] = _dot(h, wq_ref[...]) * (HEAD_DIM ** -0.5)


def _q_s(x, gains, wq):
    return pl.pallas_call(
        _q_s_body,
        out_shape=jax.ShapeDtypeStruct((x.shape[0], wq.shape[1]), F32),
        compiler_params=pltpu.CompilerParams(vmem_limit_bytes=VMEM_LIMIT),
        name="q_sample",
    )(x, gains, wq)


def _attn_p_body(*refs, has_prev, n_heads):
    if has_prev:
        q_ref, kp_ref, kc_ref, vp_ref, vc_ref, bias_ref, ones_ref, o_ref, m_ref, l_ref = refs
        first = (pl.program_id(2) == 0).astype(jnp.int32)
        kv_refs = ((kp_ref, vp_ref), (kc_ref, vc_ref))
    else:
        q_ref, kc_ref, vc_ref, bias_ref, ones_ref, o_ref, m_ref, l_ref = refs
        first = 0
        kv_refs = ((kc_ref, vc_ref),)
    per = LANES // HEAD_DIM
    n_grp = n_heads // per
    lane = lax.broadcasted_iota(jnp.int32, (QB, LANES), 1)
    lane_head = lax.broadcasted_iota(jnp.int32, (1, LANES), 1) // HEAD_DIM

    def scores(grp):
        cols = slice(grp * LANES, (grp + 1) * LANES)
        q = q_ref[0, 0, :, cols]
        qq = jnp.concatenate([q * jnp.where(lane_head == hh, 1.0, 0.0).astype(BF16) for hh in range(per)], axis=0)
        k = jnp.concatenate([k_ref[0, 0, :, cols] for k_ref, _ in kv_refs], axis=0)
        return _dot_nt(qq, k) + bias_ref[first, grp]

    m_all = jnp.zeros((QB, LANES), F32)
    l_all = jnp.ones((QB, LANES), F32)
    queue = [scores(g) for g in range(min(ATTN_AHEAD, n_grp))]
    for grp in range(n_grp):
        s = queue.pop(0)
        if grp + ATTN_AHEAD < n_grp:
            queue.append(scores(grp + ATTN_AHEAD))
        cols = slice(grp * LANES, (grp + 1) * LANES)
        m = jnp.max(s, axis=-1, keepdims=True)
        e = jnp.exp(s - m).astype(BF16)
        v1 = jnp.concatenate([jnp.concatenate([v_ref[0, 0, :, cols], ones_ref[...]], axis=1) for _, v_ref in kv_refs],
                             axis=0)
        ol = _dot(e, v1)
        o, l = ol[:, :LANES], ol[:, LANES:]
        out = o[0:QB]
        for hh in range(per):
            rows = slice(hh * QB, (hh + 1) * QB)
            if hh > 0:
                out = jnp.where(lane_head == hh, o[rows], out)
            m_all = jnp.where(lane == grp * per + hh, m[rows], m_all)
            l_all = jnp.where(lane == grp * per + hh, l[rows], l_all)
        o_ref[0, 0, :, cols] = out
    m_ref[0, 0] = m_all
    l_ref[0, 0] = l_all


def _attn_p(q, k, v, bias, g, n_heads):
    batch, dil, sub, e = q.shape
    nb = sub // QB
    has_prev = nb > 1
    assert bias.shape[0] == (2 if has_prev else 1)
    ones = jnp.ones((QB, LANES), BF16)
    blk = (1, 1, QB, e)
    cur = lambda b, r, n: (b, r, n, 0)
    prev = lambda b, r, n: (b, r, jnp.maximum(n - 1, 0), 0)
    if has_prev:
        in_specs = [pl.BlockSpec(blk, cur), pl.BlockSpec(blk, prev), pl.BlockSpec(blk, cur),
                    pl.BlockSpec(blk, prev), pl.BlockSpec(blk, cur)]
        args = (q, k, k, v, v)
    else:
        in_specs = [pl.BlockSpec(blk, cur)] * 3
        args = (q, k, v)
    return pl.pallas_call(
        functools.partial(_attn_p_body, has_prev=has_prev, n_heads=n_heads),
        grid=(batch, dil, nb),
        in_specs=in_specs + [_resident(bias.shape), _resident(ones.shape)],
        out_specs=[pl.BlockSpec(blk, cur)] + [pl.BlockSpec((1, 1, QB, LANES), cur)] * 2,
        out_shape=[jax.ShapeDtypeStruct((batch, dil, sub, e), F32)]
                  + [jax.ShapeDtypeStruct((batch, dil, sub, LANES), F32)] * 2,
        compiler_params=_params("parallel", "parallel", "arbitrary"),
        name=f"attn_prompt_g{g}",
    )(*args, bias, ones)


def _expand_heads(w, e_ref):
    hi = w.astype(BF16)
    lo = (w - hi.astype(F32)).astype(BF16)
    return _dot(hi, e_ref[...]) + _dot(lo, e_ref[...])


def _merge_groups(os_, ms, ls, e_ref):
    top = functools.reduce(jnp.maximum, ms)
    es = [jnp.exp(m - top) for m in ms]
    den = functools.reduce(lambda a, b: a + b, [e * l for e, l in zip(es, ls)])
    acc = None
    for o, e in zip(os_, es):
        term = _expand_heads(e / den, e_ref) * o
        acc = term if acc is None else acc + term
    return acc


def _merge_body(x_ref, *refs, dils, tm):
    ng = len(dils)
    o_refs, m_refs, l_refs = refs[:ng], refs[ng:2 * ng], refs[2 * ng:3 * ng]
    e_ref, wo_ref, g_ref, out_ref, obuf_ref, sbuf_ref = refs[3 * ng:]
    os_, ms, ls = [], [], []
    for gi, dil in enumerate(dils):
        if dil == 1:
            os_.append(o_refs[gi][0, 0])
            ms.append(m_refs[gi][0, 0])
            ls.append(l_refs[gi][0, 0])
        else:
            n_cc = obuf_ref.shape[1]
            for r in range(dil):
                rows = pl.ds(r, tm // dil, stride=dil)
                for cc in range(n_cc):
                    obuf_ref[gi, cc, rows, :] = o_refs[gi][0, r, :, cc * LANES:(cc + 1) * LANES]
                sbuf_ref[0, gi, rows, :] = m_refs[gi][0, r]
                sbuf_ref[1, gi, rows, :] = l_refs[gi][0, r]
            os_.append(jnp.concatenate([obuf_ref[gi, cc] for cc in range(n_cc)], axis=1))
            ms.append(sbuf_ref[0, gi])
            ls.append(sbuf_ref[1, gi])
    o = _merge_groups(os_, ms, ls, e_ref)
    y = _dot(o.astype(BF16), wo_ref[...])
    out_ref[0] = x_ref[0] + _rms(y, g_ref[1:2])


def _merge(x, os_, ms, ls, expand, w_o, gains, tm):
    b, s, d = x.shape
    dils = tuple(o.shape[1] for o in os_)
    e = os_[0].shape[-1]
    grp = lambda i, j: (i, 0, j, 0)
    return pl.pallas_call(
        functools.partial(_merge_body, dils=dils, tm=tm),
        grid=(b, s // tm),
        in_specs=[pl.BlockSpec((1, tm, d), lambda i, j: (i, j, 0))]
                 + [pl.BlockSpec((1, dil, tm // dil, e), grp) for dil in dils]
                 + [pl.BlockSpec((1, dil, tm // dil, LANES), grp) for dil in dils] * 2
                 + [_resident(expand.shape), _resident(w_o.shape), _resident(gains.shape)],
        out_specs=pl.BlockSpec((1, tm, d), lambda i, j: (i, j, 0)),
        out_shape=jax.ShapeDtypeStruct((b, s, d), F32),
        scratch_shapes=[pltpu.VMEM((len(dils), e // LANES, tm, LANES), F32),
                        pltpu.VMEM((2, len(dils), tm, LANES), F32)],
        compiler_params=_params("parallel", "parallel"),
        name="attn_merge",
    )(x, *os_, *ms, *ls, expand, w_o, gains)


def _attn_s_body(q_ref, c_ref, knew_ref, vnew_ref, bc_ref, bn_ref, o_ref, lse_ref, cout_ref,
                 *, dil, n_steps, nt, hc):
    b = pl.program_id(0)
    n = c_ref.shape[-1]
    q = q_ref[0].astype(BF16)
    lane = lax.broadcasted_iota(jnp.int32, (nt, LANES), 1)
    dn = lax.broadcasted_iota(jnp.int32, (nt, LANES), 0) - lax.rem(lane, nt)
    valid_n = (lane // nt == b) & (dn >= 0) & (lax.rem(dn, dil) == 0) & (dn <= dil * n_steps)
    tail = lax.broadcasted_iota(jnp.int32, (HEAD_DIM, LANES), 1) >= LANES - nt
    shift = LANES - nt - b * nt
    head = lax.broadcasted_iota(jnp.int32, (nt, hc), 1)
    heads = [slice(hh * HEAD_DIM, (hh + 1) * HEAD_DIM) for hh in range(hc)]
    s_c = [_dot(q[:, sl], c_ref[0, 0, hh].astype(BF16)) + bc_ref[hh] for hh, sl in enumerate(heads)]
    s_n = [jnp.where(valid_n, _dot(q[:, sl], knew_ref[sl, :].astype(BF16)) + bn_ref[hh], NEG_INF)
           for hh, sl in enumerate(heads)]
    ms = [jnp.maximum(jnp.max(a, axis=-1, keepdims=True), jnp.max(c, axis=-1, keepdims=True))
          for a, c in zip(s_c, s_n)]
    e_c = [jnp.exp(a - m) for a, m in zip(s_c, ms)]
    e_n = [jnp.exp(c - m) for c, m in zip(s_n, ms)]
    ls = [jnp.sum(a, axis=-1, keepdims=True) + jnp.sum(c, axis=-1, keepdims=True) for a, c in zip(e_c, e_n)]
    lse = jnp.zeros((nt, hc), F32)
    for hh, sl in enumerate(heads):
        o = (_dot_nt((e_c[hh] / ls[hh]).astype(BF16), c_ref[0, 1, hh].astype(BF16))
             + _dot_nt((e_n[hh] / ls[hh]).astype(BF16), vnew_ref[sl, :].astype(BF16)))
        o_ref[0, :, sl] = o
        lse = jnp.where(head == hh, ms[hh] + jnp.log(ls[hh]), lse)
    lse_ref[0, 0] = lse
    for hh, sl in enumerate(heads):
        for kv, new_ref in enumerate((knew_ref, vnew_ref)):
            rolled = pltpu.roll(c_ref[0, kv, hh], n - nt, axis=1)
            placed = pltpu.roll(new_ref[sl, :], shift, axis=1)
            if n > LANES:
                cout_ref[0, kv, hh, :, 0:n - LANES] = rolled[:, 0:n - LANES]
            cout_ref[0, kv, hh, :, n - LANES:n] = jnp.where(tail, placed, rolled[:, n - LANES:n])


def _attn_s(q, cache, kvt_new, bias_c, bias_n, g, dil, n_steps, nt, hc):
    b, _, n_heads, _, n = cache.shape
    assert n == dil * n_steps and kvt_new.shape[1] == LANES == b * nt
    e = n_heads * HEAD_DIM
    nc = n_heads // hc
    cw = hc * HEAD_DIM
    return pl.pallas_call(
        functools.partial(_attn_s_body, dil=dil, n_steps=n_steps, nt=nt, hc=hc),
        grid=(b, nc),
        in_specs=[pl.BlockSpec((1, nt, cw), lambda i, c: (i, 0, g * nc + c)),
                  pl.BlockSpec((1, 2, hc, HEAD_DIM, n), lambda i, c: (i, 0, c, 0, 0)),
                  pl.BlockSpec((cw, LANES), lambda i, c: ((g * 2) * nc + c, 0)),
                  pl.BlockSpec((cw, LANES), lambda i, c: ((g * 2 + 1) * nc + c, 0)),
                  pl.BlockSpec((hc, nt, n), lambda i, c: (c, 0, 0)),
                  pl.BlockSpec((hc, nt, LANES), lambda i, c: (c, 0, 0))],
        out_specs=[pl.BlockSpec((1, nt, cw), lambda i, c: (i, 0, c)),
                   pl.BlockSpec((1, 1, nt, hc), lambda i, c: (i, c, 0, 0)),
                   pl.BlockSpec((1, 2, hc, HEAD_DIM, n), lambda i, c: (i, 0, c, 0, 0))],
        out_shape=[jax.ShapeDtypeStruct((b, nt, e), F32),
                   jax.ShapeDtypeStruct((b, nc, nt, hc), F32),
                   jax.ShapeDtypeStruct(cache.shape, F32)],
        compiler_params=_params("parallel", "arbitrary"),
        name=f"attn_sample_g{g}",
    )(q, cache, kvt_new, kvt_new, bias_c, bias_n)


def _conv_tail(y, x, lng_ref, lnb_ref, w2_ref, b2_ref, g_ref):
    mu = jnp.mean(y, axis=-1, keepdims=True)
    yc = y - mu
    var = jnp.mean(yc * yc, axis=-1, keepdims=True)
    yn = yc * lax.rsqrt(var + EPS) * lng_ref[...] + lnb_ref[...]
    z = _dot((yn * jax.nn.sigmoid(yn)).astype(BF16), w2_ref[...]) + b2_ref[...]
    return x + _rms(z, g_ref[1:2])


def _glu(x, g_ref, w1_ref, b1_ref):
    d = x.shape[1]
    a = _dot(_rms(x, g_ref[0:1]).astype(BF16), w1_ref[...]) + b1_ref[...]
    return a[:, :d] * jax.nn.sigmoid(a[:, d:])


def _conv_p_body(x_ref, g_ref, w1_ref, b1_ref, wdw_ref, bdw_ref, lng_ref, lnb_ref, w2_ref, b2_ref,
                 o_ref, ul_ref, ush_ref, y_ref, *, ts):
    j = pl.program_id(1)
    d = y_ref.shape[1]
    rows_ext = CARRY_ROWS + ts

    @pl.when(j == 0)
    def _():
        ush_ref[0, 0:CARRY_ROWS, :] = jnp.zeros((CARRY_ROWS, d), F32)

    x = x_ref[0]
    ush_ref[0, CARRY_ROWS:, :] = _glu(x, g_ref, w1_ref, b1_ref)
    n_sh = rows_ext - SUBLANES
    for m in range(1, SUBLANES):
        for r0 in range(0, n_sh, CONV_COPY_ROWS):
            r1 = min(r0 + CONV_COPY_ROWS, n_sh)
            ush_ref[m, r0:r1, :] = ush_ref[0, r0 + m:r1 + m, :]
    lead = CARRY_ROWS - (CONV_WIDTH - 1)

    def chunk(c, carry):
        r0 = pl.multiple_of(c * CONV_CHUNK_ROWS, CONV_CHUNK_ROWS)
        for cb in range(d // LANES):
            cols = slice(cb * LANES, (cb + 1) * LANES)
            acc = jnp.broadcast_to(bdw_ref[:, cols], (CONV_CHUNK_ROWS, LANES))
            for k in range(CONV_WIDTH):
                a, m = divmod(lead + k, SUBLANES)
                acc = acc + wdw_ref[k:k + 1, cols] * ush_ref[m, pl.ds(r0 + a * SUBLANES, CONV_CHUNK_ROWS), cols]
            y_ref[pl.ds(r0, CONV_CHUNK_ROWS), cols] = acc
        return carry

    lax.fori_loop(0, ts // CONV_CHUNK_ROWS, chunk, 0)
    o_ref[0] = _conv_tail(y_ref[...], x, lng_ref, lnb_ref, w2_ref, b2_ref, g_ref)
    last = ush_ref[0, ts:ts + CARRY_ROWS, :]
    ush_ref[0, 0:CARRY_ROWS, :] = last
    ul_ref[0] = last


def _conv_p(x, gains, w1, b1, wdw, bdw, lng, lnb, w2, b2, ts):
    b, s, d = x.shape
    consts = (gains, w1, b1, wdw, bdw, lng, lnb, w2, b2)
    return pl.pallas_call(
        functools.partial(_conv_p_body, ts=ts),
        grid=(b, s // ts),
        in_specs=[pl.BlockSpec((1, ts, d), lambda i, j: (i, j, 0))] + [_resident(a.shape) for a in consts],
        out_specs=[pl.BlockSpec((1, ts, d), lambda i, j: (i, j, 0)),
                   pl.BlockSpec((1, CARRY_ROWS, d), lambda i, j: (i, 0, 0))],
        out_shape=[jax.ShapeDtypeStruct((b, s, d), F32), jax.ShapeDtypeStruct((b, CARRY_ROWS, d), F32)],
        scratch_shapes=[pltpu.VMEM((SUBLANES, CARRY_ROWS + ts, d), F32), pltpu.VMEM((ts, d), F32)],
        compiler_params=_params("parallel", "arbitrary"),
        name="conv_prompt",
    )(x, *consts)


def _conv_s_body(x_ref, st_ref, g_ref, w1_ref, b1_ref, wdw_ref, bdw_ref, lng_ref, lnb_ref, w2_ref, b2_ref,
                 o_ref, ns_ref, *, nb, nt):
    x = x_ref[...]
    u = _glu(x, g_ref, w1_ref, b1_ref)
    n_prev = CONV_WIDTH - 1
    ext = [st_ref[i * nb:(i + 1) * nb, :] for i in range(n_prev)]
    ext += [u[t * nb:(t + 1) * nb, :] for t in range(nt)]
    ys = []
    for t in range(nt):
        y = None
        for k in range(CONV_WIDTH):
            term = wdw_ref[k:k + 1, :] * ext[t + k]
            y = term if y is None else y + term
        ys.append(y + bdw_ref[...])
    o_ref[...] = _conv_tail(jnp.concatenate(ys, axis=0), x, lng_ref, lnb_ref, w2_ref, b2_ref, g_ref)
    for i in range(n_prev):
        ns_ref[i * nb:(i + 1) * nb, :] = ext[nt + i]


def _conv_s(x, state, gains, w1, b1, wdw, bdw, lng, lnb, w2, b2, nb, nt):
    return pl.pallas_call(
        functools.partial(_conv_s_body, nb=nb, nt=nt),
        out_shape=[jax.ShapeDtypeStruct(x.shape, F32), jax.ShapeDtypeStruct(state.shape, F32)],
        compiler_params=pltpu.CompilerParams(vmem_limit_bytes=VMEM_LIMIT),
        name="conv_sample",
    )(x, state, gains, w1, b1, wdw, bdw, lng, lnb, w2, b2)


def _t5_bucket(dist):
    max_exact = N_BUCKETS // 2
    df = jnp.maximum(dist, 1).astype(F32)
    large = max_exact + (jnp.log(df / max_exact) / math.log(MAX_DISTANCE / max_exact)
                         * (N_BUCKETS - max_exact)).astype(jnp.int32)
    large = jnp.minimum(large, N_BUCKETS - 1)
    return jnp.where(dist < max_exact, dist, large)


def _bias_by_step(rel_bias, g, dil, n_steps, n_heads):
    tab = rel_bias[:, g * n_heads:(g + 1) * n_heads]
    bucket = _t5_bucket(dil * jnp.arange(n_steps + 1, dtype=jnp.int32))
    onehot = (bucket[:, None] == jnp.arange(N_BUCKETS, dtype=jnp.int32)[None, :]).astype(F32)
    return jnp.dot(onehot, tab, precision=lax.Precision.HIGHEST)


def _band_bias(by_step, n_steps, nk):
    off = nk - QB
    period = 2 * nk
    heads = by_step.shape[1]
    base = jnp.concatenate([by_step[::-1].T, jnp.full((heads, period - n_steps - 1), NEG_INF, F32)], axis=1)
    w = jnp.roll(base, off - n_steps, axis=1)
    r = period - 1
    flat = jnp.tile(w, (1, -(-(QB * r) // period)))[:, :QB * r]
    band = flat.reshape(heads, QB, r)[:, :, :nk]
    if off > 0:
        hidden = (np.arange(nk) < off)[None, None, :]
        band = jnp.stack([band, jnp.where(hidden, NEG_INF, band)])
    else:
        band = band[None]
    per = LANES // HEAD_DIM
    return band.reshape(band.shape[0], heads // per, per * QB, nk)


def _window_bias(by_step, dil, n_steps, nt):
    base = jnp.repeat(by_step[n_steps:0:-1].T, dil, axis=1)
    bias = jnp.stack([jnp.roll(base, t, axis=1) for t in range(nt)], axis=1)
    n = dil * n_steps
    delta = n + np.arange(nt)[:, None] - np.arange(n)[None, :]
    attended = (delta % dil == 0) & (delta <= dil * n_steps)
    return jnp.where(attended[None], bias, NEG_INF)


def _new_key_bias(by_step, dil, n_steps, nt):
    zero = jnp.zeros_like(by_step[0])
    rows = []
    for t in range(nt):
        cols = []
        for tp in range(nt):
            s, rem = divmod(t - tp, dil)
            cols.append(by_step[s] if (t >= tp and rem == 0 and s <= n_steps) else zero)
        rows.append(jnp.stack(cols, axis=1))
    return jnp.tile(jnp.stack(rows, axis=1), (1, 1, LANES // nt))


def _time_major(a):
    b, t, d = a.shape
    return a.transpose(1, 0, 2).reshape(t * b, d)


def _batch_major(a, b):
    return a.reshape(-1, b, a.shape[1]).transpose(1, 0, 2)


def _kv_window_layout(c):
    return c.transpose(0, 2, 3, 4, 1)


def _kv_window_unlayout(c):
    return c.transpose(0, 4, 1, 2, 3)


def kernel(x_prompt, x_sample, state_pool, cache_kv_g0, cache_kv_g1, cache_kv_g2, state_conv, norm_gains, rel_bias, pool_w, pool_scale, attn_w_qkv, attn_w_o, conv_w_pw1, conv_b_pw1, conv_w_dw, conv_b_dw, conv_ln_g, conv_ln_b, conv_w_pw2, conv_b_pw2, ffn_w_up, ffn_w_down):
    batch, seq, d = x_prompt.shape
    dec_b, dec_t, _ = x_sample.shape
    depth = norm_gains.shape[0]
    n_heads = attn_w_o.shape[1] // HEAD_DIM
    e_dim = n_heads * HEAD_DIM
    n_groups = len(ATTN_GROUPS)
    dils = tuple(dil for _, dil in ATTN_GROUPS)
    caches_in = (cache_kv_g0, cache_kv_g1, cache_kv_g2)
    n_tok = batch * seq
    n_dec = dec_b * dec_t

    xp = x_prompt
    xs = _time_major(x_sample)

    expand = jnp.asarray(np.kron(np.eye(LANES, n_heads), np.ones((1, HEAD_DIM))), BF16)

    pool_p, pool_s, conv_p, conv_s = [], [], [], []
    kv_p = [[] for _ in ATTN_GROUPS]
    kv_s = [[] for _ in ATTN_GROUPS]
    for i in range(depth):
        kind, j = i % 3, i // 3
        gains = norm_gains[i]
        if kind == 0:
            pw, ps = pool_w[j].astype(BF16), pool_scale[j][None, :]
            xp, h_last = _pool_p(xp, gains, pw, ps, ts=512)
            pool_p.append(h_last[:, POOL_CARRY - POOL_STATE:])
            xs, new_state = _pool_s(xs, _time_major(state_pool[j]), gains, pw, ps,
                                    nb=dec_b, nt=dec_t, pos0=PAST_LEN)
            pool_s.append(_batch_major(new_state, dec_b))
        elif kind == 1:
            w4 = attn_w_qkv[j].reshape(d, n_groups, 3, e_dim)
            w_qkv = attn_w_qkv[j].astype(BF16)
            w_q = w4[:, :, 0].reshape(d, n_groups * e_dim).astype(BF16)
            wt_kv = w4[:, :, 1:3].reshape(d, n_groups * 2 * e_dim).T.astype(BF16)
            w_o = attn_w_o[j].astype(BF16)
            by_step = [_bias_by_step(rel_bias, g, dil, win // dil, n_heads)
                       for g, (win, dil) in enumerate(ATTN_GROUPS)]
            qkv = _qkv_p(xp, gains, w_qkv, dils, e_dim, tm=512)
            os_, ms, ls = [], [], []
            for g, (win, dil) in enumerate(ATTN_GROUPS):
                n_steps = win // dil
                nk = 2 * QB if seq // dil > QB else QB
                bias = _band_bias(by_step[g], n_steps, nk)
                o, m, l = _attn_p(qkv[3 * g], qkv[3 * g + 1], qkv[3 * g + 2], bias, g, n_heads)
                os_.append(o)
                ms.append(m)
                ls.append(l)
                keep = min(win, seq)
                kvt = _kvt(xp, gains, wt_kv, g, 2 * e_dim, keep, tm=min(keep, 512))
                kv_p[g].append(_kv_window_unlayout(kvt.reshape(batch, 2, n_heads, HEAD_DIM, keep)))
            xp = _merge(xp, os_, ms, ls, expand, w_o, gains, tm=512)
            xs_b = _batch_major(xs, dec_b)
            xs_flat = xs_b.reshape(n_dec, d)
            q_s = _q_s(xs_flat, gains, w_q).reshape(dec_b, dec_t, n_groups * e_dim)
            kvt_new = _kvt(xs_flat[None], gains, wt_kv, 0, n_groups * 2 * e_dim, n_dec, tm=n_dec)[0]
            os_, ms, ls = [], [], []
            for g, (win, dil) in enumerate(ATTN_GROUPS):
                n_steps = win // dil
                cache = _kv_window_layout(caches_in[g][j])
                bias_c = _window_bias(by_step[g], dil, n_steps, dec_t)
                bias_n = _new_key_bias(by_step[g], dil, n_steps, dec_t)
                hc = max(1, min(n_heads, SAMPLE_WINDOW_BLOCK_BYTES // (2 * HEAD_DIM * win * 4)))
                o, lse, new_cache = _attn_s(q_s, cache, kvt_new, bias_c, bias_n, g, dil, n_steps, dec_t, hc=hc)
                os_.append(o.reshape(1, 1, n_dec, e_dim))
                lse = lse.transpose(0, 2, 1, 3).reshape(1, 1, n_dec, n_heads)
                ms.append(jnp.pad(lse, ((0, 0), (0, 0), (0, 0), (0, LANES - n_heads))))
                ls.append(jnp.ones((1, 1, n_dec, LANES), F32))
                kv_s[g].append(_kv_window_unlayout(new_cache))
            xs_flat = _merge(xs_flat[None], os_, ms, ls, expand, w_o, gains, tm=n_dec)[0]
            xs = _time_major(xs_flat.reshape(dec_b, dec_t, d))
        else:
            cw = (conv_w_pw1[j].astype(BF16), conv_b_pw1[j][None, :], conv_w_dw[j], conv_b_dw[j][None, :],
                  conv_ln_g[j][None, :], conv_ln_b[j][None, :], conv_w_pw2[j].astype(BF16), conv_b_pw2[j][None, :])
            xp, u_last = _conv_p(xp, gains, *cw, ts=512)
            conv_p.append(u_last[:, CARRY_ROWS - (CONV_WIDTH - 1):])
            xs, new_state = _conv_s(xs, _time_major(state_conv[j]), gains, *cw, nb=dec_b, nt=dec_t)
            conv_s.append(_batch_major(new_state, dec_b))
        w_up, w_down = ffn_w_up[i].astype(BF16), ffn_w_down[i].astype(BF16)
        xp = _ffn(xp.reshape(n_tok, d), gains, w_up, w_down, tm=512).reshape(batch, seq, d)
        xs = _ffn(xs, gains, w_up, w_down, tm=n_dec)
    return (xp, _batch_major(xs, dec_b),
            jnp.stack(pool_p), jnp.stack(pool_s),
            jnp.stack(kv_p[0]), jnp.stack(kv_s[0]),
            jnp.stack(kv_p[1]), jnp.stack(kv_s[1]),
            jnp.stack(kv_p[2]), jnp.stack(kv_s[2]),
            jnp.stack(conv_p), jnp.stack(conv_s))
```

```python
import functools
import math

import jax
import jax.numpy as jnp
import numpy as np
from jax import lax
from jax.experimental import pallas as pl
from jax.experimental.pallas import tpu as pltpu

F32 = jnp.float32
BF16 = jnp.bfloat16

EPS = 1e-6
NEG_INF = -1e30
POOL_WINDOWS = (2, 4, 8, 16)
POOL_STATE = max(POOL_WINDOWS) - 1
PAST_LEN = 8192
ATTN_GROUPS = ((128, 1), (512, 4), (2048, 16))
HEAD_DIM = 64
QB = 128
ATTN_AHEAD = 3
N_BUCKETS = 32
MAX_DISTANCE = 2048
CONV_WIDTH = 31
CARRY_ROWS = 32
POOL_CARRY = 16
LANES = 128
SUBLANES = 8
CONV_CHUNK_ROWS = 64
CONV_COPY_ROWS = 128

VMEM_LIMIT = 56 * 1024 * 1024
FFN_CHUNK = 512

NT_DIMS = (((1,), (1,)), ((), ()))


def _params(*sem):
    return pltpu.CompilerParams(dimension_semantics=sem, vmem_limit_bytes=VMEM_LIMIT)


def _resident(shape):
    nd = len(shape)
    return pl.BlockSpec(shape, lambda *_: (0,) * nd, pipeline_mode=pl.Buffered(1))


def _rms(x, g):
    return x * lax.rsqrt(jnp.mean(x * x, axis=-1, keepdims=True) + EPS) * g


def _dot(a, b):
    return jnp.dot(a, b, preferred_element_type=F32)


def _dot_nt(a, b):
    return lax.dot_general(a, b, NT_DIMS, preferred_element_type=F32)


def _ffn_body(x_ref, g_ref, wup_ref, wdn_ref, o_ref, *, f_chunk):
    x = x_ref[...]
    h = _rms(x, g_ref[2:3]).astype(BF16)
    acc = jnp.zeros_like(x)
    for c in range(wup_ref.shape[1] // f_chunk):
        a = jnp.maximum(_dot(h, wup_ref[:, c * f_chunk:(c + 1) * f_chunk]), 0.0)
        acc = acc + _dot((a * a).astype(BF16), wdn_ref[c * f_chunk:(c + 1) * f_chunk, :])
    o_ref[...] = x + _rms(acc, g_ref[3:4])


def _ffn(x, gains, w_up, w_down, tm):
    n, d = x.shape
    f = w_up.shape[1]
    return pl.pallas_call(
        functools.partial(_ffn_body, f_chunk=1024),
        grid=(n // tm,),
        in_specs=[pl.BlockSpec((tm, d), lambda i: (i, 0)), _resident(gains.shape),
                  _resident((d, f)), _resident((f, d))],
        out_specs=pl.BlockSpec((tm, d), lambda i: (i, 0)),
        out_shape=jax.ShapeDtypeStruct((n, d), F32),
        compiler_params=_params("parallel"),
        name="ffn",
    )(x, gains, w_up, w_down)


def _pool_p_body(x_ref, g_ref, pw_ref, ps_ref, o_ref, hl_ref, carry_ref, *, ts):
    j = pl.program_id(1)

    @pl.when(j == 0)
    def _():
        carry_ref[...] = jnp.zeros_like(carry_ref)

    x = x_ref[0]
    h = _rms(x, g_ref[0:1])
    ext = jnp.concatenate([carry_ref[...], h], axis=0)
    sums = []
    s = ext
    for k in (1, 2, 4, 8):
        s = s + pltpu.roll(s, k, axis=0)
        sums.append(s)
    pos = j * ts + lax.broadcasted_iota(jnp.int32, (ts, 1), 0)
    gw = pw_ref.shape[1]
    ys = []
    for gi, (w, s) in enumerate(zip(POOL_WINDOWS, sums)):
        cols = slice(gi * gw, (gi + 1) * gw)
        inv_cnt = 1.0 / jnp.minimum(pos + 1, w).astype(F32)
        pooled = s[POOL_CARRY:, cols] * inv_cnt - h[:, cols]
        ys.append(_dot(pooled.astype(BF16), pw_ref[gi]))
    y = jnp.concatenate(ys, axis=1) * ps_ref[...]
    o_ref[0] = x + _rms(y, g_ref[1:2])
    carry_ref[...] = h[ts - POOL_CARRY:, :]
    hl_ref[0] = h[ts - POOL_CARRY:, :]


def _pool_p(x, gains, pw, ps, ts):
    b, s, d = x.shape
    return pl.pallas_call(
        functools.partial(_pool_p_body, ts=ts),
        grid=(b, s // ts),
        in_specs=[pl.BlockSpec((1, ts, d), lambda i, j: (i, j, 0)), _resident(gains.shape),
                  _resident(pw.shape), _resident(ps.shape)],
        out_specs=[pl.BlockSpec((1, ts, d), lambda i, j: (i, j, 0)),
                   pl.BlockSpec((1, POOL_CARRY, d), lambda i, j: (i, 0, 0))],
        out_shape=[jax.ShapeDtypeStruct((b, s, d), F32),
                   jax.ShapeDtypeStruct((b, POOL_CARRY, d), F32)],
        scratch_shapes=[pltpu.VMEM((POOL_CARRY, d), F32)],
        compiler_params=_params("parallel", "arbitrary"),
        name="pool_prompt",
    )(x, gains, pw, ps)


def _pool_s_body(x_ref, st_ref, g_ref, pw_ref, ps_ref, o_ref, ns_ref, *, nb, nt, pos0):
    x = x_ref[...]
    h = _rms(x, g_ref[0:1])
    ext = [st_ref[i * nb:(i + 1) * nb, :] for i in range(POOL_STATE)]
    ext += [h[t * nb:(t + 1) * nb, :] for t in range(nt)]
    gw = pw_ref.shape[1]
    for t in range(nt):
        ys = []
        for gi, w in enumerate(POOL_WINDOWS):
            cols = slice(gi * gw, (gi + 1) * gw)
            end = POOL_STATE + t
            acc = ext[end][:, cols]
            for i in range(1, min(w, end + 1)):
                acc = acc + ext[end - i][:, cols]
            pooled = acc / float(min(pos0 + t + 1, w)) - ext[end][:, cols]
            ys.append(_dot(pooled.astype(BF16), pw_ref[gi]))
        y = jnp.concatenate(ys, axis=1) * ps_ref[...]
        o_ref[t * nb:(t + 1) * nb, :] = x[t * nb:(t + 1) * nb, :] + _rms(y, g_ref[1:2])
    for i in range(POOL_STATE):
        ns_ref[i * nb:(i + 1) * nb, :] = ext[nt + i]


def _pool_s(x, state, gains, pw, ps, nb, nt, pos0):
    return pl.pallas_call(
        functools.partial(_pool_s_body, nb=nb, nt=nt, pos0=pos0),
        out_shape=[jax.ShapeDtypeStruct(x.shape, F32), jax.ShapeDtypeStruct(state.shape, F32)],
        compiler_params=pltpu.CompilerParams(vmem_limit_bytes=VMEM_LIMIT),
        name="pool_sample",
    )(x, state, gains, pw, ps)


def _qkv_p_body(x_ref, g_ref, w_ref, *refs, dils, tm):
    out_refs, ybuf_ref = refs[:-1], refs[-1]
    e = out_refs[0].shape[-1]
    h = _rms(x_ref[0], g_ref[0:1]).astype(BF16)
    for g, dil in enumerate(dils):
        for c in range(3):
            idx = g * 3 + c
            y = _dot(h, w_ref[:, idx * e:(idx + 1) * e])
            if c == 0:
                y = y * (HEAD_DIM ** -0.5)
            if dil == 1:
                out_refs[idx][0, 0] = y.astype(BF16)
            else:
                slot = idx % 2
                for cc in range(e // LANES):
                    ybuf_ref[slot, cc] = y[:, cc * LANES:(cc + 1) * LANES]
                for r in range(dil):
                    out_refs[idx][0, r] = jnp.concatenate(
                        [ybuf_ref[slot, cc, pl.ds(r, tm // dil, stride=dil), :] for cc in range(e // LANES)],
                        axis=1).astype(BF16)


def _qkv_p(x, gains, w, dils, e, tm):
    b, s, d = x.shape
    out_specs, out_shape = [], []
    for dil in dils:
        for _ in range(3):
            out_specs.append(pl.BlockSpec((1, dil, tm // dil, e), lambda i, j: (i, 0, j, 0)))
            out_shape.append(jax.ShapeDtypeStruct((b, dil, s // dil, e), BF16))
    return pl.pallas_call(
        functools.partial(_qkv_p_body, dils=dils, tm=tm),
        grid=(b, s // tm),
        in_specs=[pl.BlockSpec((1, tm, d), lambda i, j: (i, j, 0)), _resident(gains.shape), _resident(w.shape)],
        out_specs=out_specs,
        out_shape=out_shape,
        scratch_shapes=[pltpu.VMEM((2, e // LANES, tm, LANES), F32)],
        compiler_params=_params("parallel", "parallel"),
        name="qkv_prompt",
    )(x, gains, w)


def _kvt_body(x_ref, g_ref, wt_ref, o_ref):
    h = _rms(x_ref[0], g_ref[0:1]).astype(BF16)
    o_ref[0] = _dot_nt(wt_ref[...], h)


def _kvt(x, gains, wt, row_block, n_rows, keep, tm):
    b, s, d = x.shape
    first = (s - keep) // tm
    return pl.pallas_call(
        _kvt_body,
        grid=(b, keep // tm),
        in_specs=[pl.BlockSpec((1, tm, d), lambda i, j: (i, first + j, 0)), _resident(gains.shape),
                  pl.BlockSpec((n_rows, d), lambda i, j: (row_block, 0), pipeline_mode=pl.Buffered(1))],
        out_specs=pl.BlockSpec((1, n_rows, tm), lambda i, j: (i, 0, j)),
        out_shape=jax.ShapeDtypeStruct((b, n_rows, keep), F32),
        compiler_params=_params("parallel", "parallel"),
        name="kv_transposed",
    )(x, gains, wt)


def _q_s_body(x_ref, g_ref, wq_ref, o_ref):
    h = _rms(x_ref[...], g_ref[0:1]).astype(BF16)
    o_ref[...] = _dot(h, wq_ref[...]) * (HEAD_DIM ** -0.5)


def _q_s(x, gains, wq):
    return pl.pallas_call(
        _q_s_body,
        out_shape=jax.ShapeDtypeStruct((x.shape[0], wq.shape[1]), F32),
        compiler_params=pltpu.CompilerParams(vmem_limit_bytes=VMEM_LIMIT),
        name="q_sample",
    )(x, gains, wq)


def _attn_p_body(*refs, has_prev, n_heads):
    if has_prev:
        q_ref, kp_ref, kc_ref, vp_ref, vc_ref, bias_ref, ones_ref, o_ref, m_ref, l_ref = refs
        first = (pl.program_id(2) == 0).astype(jnp.int32)
        kv_refs = ((kp_ref, vp_ref), (kc_ref, vc_ref))
    else:
        q_ref, kc_ref, vc_ref, bias_ref, ones_ref, o_ref, m_ref, l_ref = refs
        first = 0
        kv_refs = ((kc_ref, vc_ref),)
    per = LANES // HEAD_DIM
    n_grp = n_heads // per
    lane = lax.broadcasted_iota(jnp.int32, (QB, LANES), 1)
    lane_head = lax.broadcasted_iota(jnp.int32, (1, LANES), 1) // HEAD_DIM

    def scores(grp):
        cols = slice(grp * LANES, (grp + 1) * LANES)
        q = q_ref[0, 0, :, cols]
        qq = jnp.concatenate([q * jnp.where(lane_head == hh, 1.0, 0.0).astype(BF16) for hh in range(per)], axis=0)
        k = jnp.concatenate([k_ref[0, 0, :, cols] for k_ref, _ in kv_refs], axis=0)
        return _dot_nt(qq, k) + bias_ref[first, grp]

    m_all = jnp.zeros((QB, LANES), F32)
    l_all = jnp.ones((QB, LANES), F32)
    queue = [scores(g) for g in range(min(ATTN_AHEAD, n_grp))]
    for grp in range(n_grp):
        s = queue.pop(0)
        if grp + ATTN_AHEAD < n_grp:
            queue.append(scores(grp + ATTN_AHEAD))
        cols = slice(grp * LANES, (grp + 1) * LANES)
        m = jnp.max(s, axis=-1, keepdims=True)
        e = jnp.exp(s - m).astype(BF16)
        v1 = jnp.concatenate([jnp.concatenate([v_ref[0, 0, :, cols], ones_ref[...]], axis=1) for _, v_ref in kv_refs],
                             axis=0)
        ol = _dot(e, v1)
        o, l = ol[:, :LANES], ol[:, LANES:]
        out = o[0:QB]
        for hh in range(per):
            rows = slice(hh * QB, (hh + 1) * QB)
            if hh > 0:
                out = jnp.where(lane_head == hh, o[rows], out)
            m_all = jnp.where(lane == grp * per + hh, m[rows], m_all)
            l_all = jnp.where(lane == grp * per + hh, l[rows], l_all)
        o_ref[0, 0, :, cols] = out
    m_ref[0, 0] = m_all
    l_ref[0, 0] = l_all


def _attn_p(q, k, v, bias, g, n_heads):
    batch, dil, sub, e = q.shape
    nb = sub // QB
    has_prev = nb > 1
    assert bias.shape[0] == (2 if has_prev else 1)
    ones = jnp.ones((QB, LANES), BF16)
    blk = (1, 1, QB, e)
    cur = lambda b, r, n: (b, r, n, 0)
    prev = lambda b, r, n: (b, r, jnp.maximum(n - 1, 0), 0)
    if has_prev:
        in_specs = [pl.BlockSpec(blk, cur), pl.BlockSpec(blk, prev), pl.BlockSpec(blk, cur),
                    pl.BlockSpec(blk, prev), pl.BlockSpec(blk, cur)]
        args = (q, k, k, v, v)
    else:
        in_specs = [pl.BlockSpec(blk, cur)] * 3
        args = (q, k, v)
    return pl.pallas_call(
        functools.partial(_attn_p_body, has_prev=has_prev, n_heads=n_heads),
        grid=(batch, dil, nb),
        in_specs=in_specs + [_resident(bias.shape), _resident(ones.shape)],
        out_specs=[pl.BlockSpec(blk, cur)] + [pl.BlockSpec((1, 1, QB, LANES), cur)] * 2,
        out_shape=[jax.ShapeDtypeStruct((batch, dil, sub, e), F32)]
                  + [jax.ShapeDtypeStruct((batch, dil, sub, LANES), F32)] * 2,
        compiler_params=_params("parallel", "parallel", "arbitrary"),
        name=f"attn_prompt_g{g}",
    )(*args, bias, ones)


def _expand_heads(w, e_ref):
    hi = w.astype(BF16)
    lo = (w - hi.astype(F32)).astype(BF16)
    return _dot(hi, e_ref[...]) + _dot(lo, e_ref[...])


def _merge_groups(os_, ms, ls, e_ref):
    top = functools.reduce(jnp.maximum, ms)
    es = [jnp.exp(m - top) for m in ms]
    den = functools.reduce(lambda a, b: a + b, [e * l for e, l in zip(es, ls)])
    acc = None
    for o, e in zip(os_, es):
        term = _expand_heads(e / den, e_ref) * o
        acc = term if acc is None else acc + term
    return acc


def _merge_body(x_ref, *refs, dils, tm):
    ng = len(dils)
    o_refs, m_refs, l_refs = refs[:ng], refs[ng:2 * ng], refs[2 * ng:3 * ng]
    e_ref, wo_ref, g_ref, out_ref, obuf_ref, sbuf_ref = refs[3 * ng:]
    os_, ms, ls = [], [], []
    for gi, dil in enumerate(dils):
        if dil == 1:
            os_.append(o_refs[gi][0, 0])
            ms.append(m_refs[gi][0, 0])
            ls.append(l_refs[gi][0, 0])
        else:
            n_cc = obuf_ref.shape[1]
            for r in range(dil):
                rows = pl.ds(r, tm // dil, stride=dil)
                for cc in range(n_cc):
                    obuf_ref[gi, cc, rows, :] = o_refs[gi][0, r, :, cc * LANES:(cc + 1) * LANES]
                sbuf_ref[0, gi, rows, :] = m_refs[gi][0, r]
                sbuf_ref[1, gi, rows, :] = l_refs[gi][0, r]
            os_.append(jnp.concatenate([obuf_ref[gi, cc] for cc in range(n_cc)], axis=1))
            ms.append(sbuf_ref[0, gi])
            ls.append(sbuf_ref[1, gi])
    o = _merge_groups(os_, ms, ls, e_ref)
    y = _dot(o.astype(BF16), wo_ref[...])
    out_ref[0] = x_ref[0] + _rms(y, g_ref[1:2])


def _merge(x, os_, ms, ls, expand, w_o, gains, tm):
    b, s, d = x.shape
    dils = tuple(o.shape[1] for o in os_)
    e = os_[0].shape[-1]
    grp = lambda i, j: (i, 0, j, 0)
    return pl.pallas_call(
        functools.partial(_merge_body, dils=dils, tm=tm),
        grid=(b, s // tm),
        in_specs=[pl.BlockSpec((1, tm, d), lambda i, j: (i, j, 0))]
                 + [pl.BlockSpec((1, dil, tm // dil, e), grp) for dil in dils]
                 + [pl.BlockSpec((1, dil, tm // dil, LANES), grp) for dil in dils] * 2
                 + [_resident(expand.shape), _resident(w_o.shape), _resident(gains.shape)],
        out_specs=pl.BlockSpec((1, tm, d), lambda i, j: (i, j, 0)),
        out_shape=jax.ShapeDtypeStruct((b, s, d), F32),
        scratch_shapes=[pltpu.VMEM((len(dils), e // LANES, tm, LANES), F32),
                        pltpu.VMEM((2, len(dils), tm, LANES), F32)],
        compiler_params=_params("parallel", "parallel"),
        name="attn_merge",
    )(x, *os_, *ms, *ls, expand, w_o, gains)


class _WindowAttnChunk:
    def __init__(self, b, q, knew, vnew, c_ref, bias_c, bias_n, cout_ref, *, dil, n_steps, nt, hc):
        self.b, self.q, self.knew, self.vnew, self.c_ref, self.cout_ref = b, q.astype(BF16), knew, vnew, c_ref, cout_ref
        self.bias_c, self.bias_n, self.dil, self.n_steps, self.nt = bias_c, bias_n, dil, n_steps, nt
        self.heads = [slice(hh * HEAD_DIM, (hh + 1) * HEAD_DIM) for hh in range(hc)]
        self.lane = lax.broadcasted_iota(jnp.int32, (nt, LANES), 1)

    def logits(self):
        nt, dil = self.nt, self.dil
        dn = lax.broadcasted_iota(jnp.int32, (nt, LANES), 0) - lax.rem(self.lane, nt)
        valid_n = (self.lane // nt == self.b) & (dn >= 0) & (lax.rem(dn, dil) == 0) & (dn <= dil * self.n_steps)
        self.s_c = [_dot(self.q[:, sl], self.c_ref[0, 0, hh].astype(BF16)) + self.bias_c[hh]
                    for hh, sl in enumerate(self.heads)]
        self.s_n = [jnp.where(valid_n, _dot(self.q[:, sl], self.knew[sl, :].astype(BF16)) + self.bias_n[hh], NEG_INF)
                    for hh, sl in enumerate(self.heads)]

    def softmax(self):
        ms = [jnp.maximum(jnp.max(a, axis=-1, keepdims=True), jnp.max(c, axis=-1, keepdims=True))
              for a, c in zip(self.s_c, self.s_n)]
        e_c = [jnp.exp(a - m) for a, m in zip(self.s_c, ms)]
        e_n = [jnp.exp(c - m) for c, m in zip(self.s_n, ms)]
        ls = [jnp.sum(a, axis=-1, keepdims=True) + jnp.sum(c, axis=-1, keepdims=True) for a, c in zip(e_c, e_n)]
        self.p_c = [(e / l).astype(BF16) for e, l in zip(e_c, ls)]
        self.p_n = [(e / l).astype(BF16) for e, l in zip(e_n, ls)]
        self.lse = jnp.zeros((self.nt, LANES), F32)
        for hh, (m, l) in enumerate(zip(ms, ls)):
            self.lse = jnp.where(self.lane == hh, m + jnp.log(l), self.lse)

    def output(self):
        os_ = [_dot_nt(self.c_ref[0, 1, hh].astype(BF16), self.p_c[hh])
               + _dot_nt(self.vnew[sl, :].astype(BF16), self.p_n[hh]) for hh, sl in enumerate(self.heads)]
        return jnp.concatenate(os_, axis=0), self.lse

    def shift(self):
        n, nt = self.c_ref.shape[-1], self.nt
        tail = lax.broadcasted_iota(jnp.int32, (HEAD_DIM, LANES), 1) >= LANES - nt
        lanes_to_tail = LANES - nt - self.b * nt
        for hh, sl in enumerate(self.heads):
            for kv, new in enumerate((self.knew, self.vnew)):
                rolled = pltpu.roll(self.c_ref[0, kv, hh], n - nt, axis=1)
                placed = pltpu.roll(new[sl, :], lanes_to_tail, axis=1)
                if n > LANES:
                    self.cout_ref[0, kv, hh, :, 0:n - LANES] = rolled[:, 0:n - LANES]
                self.cout_ref[0, kv, hh, :, n - LANES:n] = jnp.where(tail, placed, rolled[:, n - LANES:n])


def _ffn_attn_s_body(x_ref, g_ref, wup_ref, wdn_ref, q_ref, kvt_ref, *refs, groups, nt, hc, f_chunk, n_chunks):
    ng = len(groups)
    c_refs, bc_refs, bn_refs = refs[:ng], refs[ng:2 * ng], refs[2 * ng:3 * ng]
    y_ref = refs[3 * ng]
    o_refs, lse_refs, cout_refs = (refs[3 * ng + 1 + k * ng:3 * ng + 1 + (k + 1) * ng] for k in range(3))
    h_ref, acc_ref = refs[-2:]
    i, j = pl.program_id(0), pl.program_id(1)

    @pl.when(j == 0)
    def _():
        h_ref[...] = _rms(x_ref[...], g_ref[2:3]).astype(BF16)
        acc_ref[...] = jnp.zeros_like(acc_ref)

    cw = hc * HEAD_DIM
    e = cw * n_chunks
    attn = []
    for g, (dil, n_steps) in enumerate(groups):
        knew = kvt_ref[pl.ds(pl.multiple_of((2 * g) * e + j * cw, cw), cw), :]
        vnew = kvt_ref[pl.ds(pl.multiple_of((2 * g + 1) * e + j * cw, cw), cw), :]
        bias_c = [bc_refs[g][j * hc + hh] for hh in range(hc)]
        bias_n = [bn_refs[g][j * hc + hh] for hh in range(hc)]
        attn.append(_WindowAttnChunk(i, q_ref[i, g, j], knew, vnew, c_refs[g], bias_c, bias_n, cout_refs[g],
                                     dil=dil, n_steps=n_steps, nt=nt, hc=hc))
    up = jnp.maximum(_dot(h_ref[...], wup_ref[j]), 0.0)
    for a in attn:
        a.logits()
    acc_ref[...] += _dot((up * up).astype(BF16), wdn_ref[pl.ds(pl.multiple_of(j * f_chunk, f_chunk), f_chunk), :])
    for a in attn:
        a.softmax()
    for g, a in enumerate(attn):
        o_refs[g][0, j], lse_refs[g][0, j] = a.output()
    for a in attn:
        a.shift()

    @pl.when(j == n_chunks - 1)
    def _():
        y_ref[...] = x_ref[...] + _rms(acc_ref[...], g_ref[3:4])


def _ffn_attn_s(x, gains, w_up, w_down, q, kvt_new, caches, bias_cs, bias_ns, groups, nt, tm):
    n, d = x.shape
    n_chunks, _, f_chunk = w_up.shape
    b, _, n_heads, _, _ = caches[0].shape
    hc = n_heads // n_chunks
    cw = hc * HEAD_DIM
    assert n // tm == b and kvt_new.shape[1] == LANES == b * nt and hc * n_chunks == n_heads
    assert all(c.shape[-1] == dil * n_steps for c, (dil, n_steps) in zip(caches, groups))
    row = lambda i, j: (i, 0)
    win = lambda i, j: (i, 0, j, 0, 0)
    per_row = lambda i, j: (i, 0, 0, 0)
    win_specs = [pl.BlockSpec((1, 2, hc, HEAD_DIM, c.shape[-1]), win) for c in caches]
    ng = len(caches)
    outs = pl.pallas_call(
        functools.partial(_ffn_attn_s_body, groups=groups, nt=nt, hc=hc, f_chunk=f_chunk, n_chunks=n_chunks),
        grid=(b, n_chunks),
        in_specs=[pl.BlockSpec((tm, d), row), _resident(gains.shape), _resident(w_up.shape),
                  _resident(w_down.shape), _resident(q.shape), _resident(kvt_new.shape)]
                 + win_specs + [_resident(a.shape) for a in bias_cs] + [_resident(a.shape) for a in bias_ns],
        out_specs=[pl.BlockSpec((tm, d), row)]
                  + [pl.BlockSpec((1, n_chunks, cw, nt), per_row)] * ng
                  + [pl.BlockSpec((1, n_chunks, nt, LANES), per_row)] * ng + win_specs,
        out_shape=[jax.ShapeDtypeStruct((n, d), F32)]
                  + [jax.ShapeDtypeStruct((b, n_chunks, cw, nt), F32)] * ng
                  + [jax.ShapeDtypeStruct((b, n_chunks, nt, LANES), F32)] * ng
                  + [jax.ShapeDtypeStruct(c.shape, F32) for c in caches],
        scratch_shapes=[pltpu.VMEM((tm, d), BF16), pltpu.VMEM((tm, d), F32)],
        compiler_params=_params("parallel", "arbitrary"),
        name="ffn_with_sample_attn",
    )(x, gains, w_up, w_down, q, kvt_new, *caches, *bias_cs, *bias_ns)
    return outs[0], outs[1:1 + ng], outs[1 + ng:1 + 2 * ng], outs[1 + 2 * ng:]


def _conv_tail(y, x, lng_ref, lnb_ref, w2_ref, b2_ref, g_ref):
    mu = jnp.mean(y, axis=-1, keepdims=True)
    yc = y - mu
    var = jnp.mean(yc * yc, axis=-1, keepdims=True)
    yn = yc * lax.rsqrt(var + EPS) * lng_ref[...] + lnb_ref[...]
    z = _dot((yn * jax.nn.sigmoid(yn)).astype(BF16), w2_ref[...]) + b2_ref[...]
    return x + _rms(z, g_ref[1:2])


def _glu(x, g_ref, w1_ref, b1_ref):
    d = x.shape[1]
    a = _dot(_rms(x, g_ref[0:1]).astype(BF16), w1_ref[...]) + b1_ref[...]
    return a[:, :d] * jax.nn.sigmoid(a[:, d:])


def _conv_p_body(x_ref, g_ref, w1_ref, b1_ref, wdw_ref, bdw_ref, lng_ref, lnb_ref, w2_ref, b2_ref,
                 o_ref, ul_ref, ush_ref, y_ref, *, ts):
    j = pl.program_id(1)
    d = y_ref.shape[1]
    rows_ext = CARRY_ROWS + ts

    @pl.when(j == 0)
    def _():
        ush_ref[0, 0:CARRY_ROWS, :] = jnp.zeros((CARRY_ROWS, d), F32)

    x = x_ref[0]
    ush_ref[0, CARRY_ROWS:, :] = _glu(x, g_ref, w1_ref, b1_ref)
    n_sh = rows_ext - SUBLANES
    for m in range(1, SUBLANES):
        for r0 in range(0, n_sh, CONV_COPY_ROWS):
            r1 = min(r0 + CONV_COPY_ROWS, n_sh)
            ush_ref[m, r0:r1, :] = ush_ref[0, r0 + m:r1 + m, :]
    lead = CARRY_ROWS - (CONV_WIDTH - 1)

    def chunk(c, carry):
        r0 = pl.multiple_of(c * CONV_CHUNK_ROWS, CONV_CHUNK_ROWS)
        for cb in range(d // LANES):
            cols = slice(cb * LANES, (cb + 1) * LANES)
            acc = jnp.broadcast_to(bdw_ref[:, cols], (CONV_CHUNK_ROWS, LANES))
            for k in range(CONV_WIDTH):
                a, m = divmod(lead + k, SUBLANES)
                acc = acc + wdw_ref[k:k + 1, cols] * ush_ref[m, pl.ds(r0 + a * SUBLANES, CONV_CHUNK_ROWS), cols]
            y_ref[pl.ds(r0, CONV_CHUNK_ROWS), cols] = acc
        return carry

    lax.fori_loop(0, ts // CONV_CHUNK_ROWS, chunk, 0)
    o_ref[0] = _conv_tail(y_ref[...], x, lng_ref, lnb_ref, w2_ref, b2_ref, g_ref)
    last = ush_ref[0, ts:ts + CARRY_ROWS, :]
    ush_ref[0, 0:CARRY_ROWS, :] = last
    ul_ref[0] = last


def _conv_p(x, gains, w1, b1, wdw, bdw, lng, lnb, w2, b2, ts):
    b, s, d = x.shape
    consts = (gains, w1, b1, wdw, bdw, lng, lnb, w2, b2)
    return pl.pallas_call(
        functools.partial(_conv_p_body, ts=ts),
        grid=(b, s // ts),
        in_specs=[pl.BlockSpec((1, ts, d), lambda i, j: (i, j, 0))] + [_resident(a.shape) for a in consts],
        out_specs=[pl.BlockSpec((1, ts, d), lambda i, j: (i, j, 0)),
                   pl.BlockSpec((1, CARRY_ROWS, d), lambda i, j: (i, 0, 0))],
        out_shape=[jax.ShapeDtypeStruct((b, s, d), F32), jax.ShapeDtypeStruct((b, CARRY_ROWS, d), F32)],
        scratch_shapes=[pltpu.VMEM((SUBLANES, CARRY_ROWS + ts, d), F32), pltpu.VMEM((ts, d), F32)],
        compiler_params=_params("parallel", "arbitrary"),
        name="conv_prompt",
    )(x, *consts)


def _conv_s_body(x_ref, st_ref, g_ref, w1_ref, b1_ref, wdw_ref, bdw_ref, lng_ref, lnb_ref, w2_ref, b2_ref,
                 o_ref, ns_ref, *, nb, nt):
    x = x_ref[...]
    u = _glu(x, g_ref, w1_ref, b1_ref)
    n_prev = CONV_WIDTH - 1
    ext = [st_ref[i * nb:(i + 1) * nb, :] for i in range(n_prev)]
    ext += [u[t * nb:(t + 1) * nb, :] for t in range(nt)]
    ys = []
    for t in range(nt):
        y = None
        for k in range(CONV_WIDTH):
            term = wdw_ref[k:k + 1, :] * ext[t + k]
            y = term if y is None else y + term
        ys.append(y + bdw_ref[...])
    o_ref[...] = _conv_tail(jnp.concatenate(ys, axis=0), x, lng_ref, lnb_ref, w2_ref, b2_ref, g_ref)
    for i in range(n_prev):
        ns_ref[i * nb:(i + 1) * nb, :] = ext[nt + i]


def _conv_s(x, state, gains, w1, b1, wdw, bdw, lng, lnb, w2, b2, nb, nt):
    return pl.pallas_call(
        functools.partial(_conv_s_body, nb=nb, nt=nt),
        out_shape=[jax.ShapeDtypeStruct(x.shape, F32), jax.ShapeDtypeStruct(state.shape, F32)],
        compiler_params=pltpu.CompilerParams(vmem_limit_bytes=VMEM_LIMIT),
        name="conv_sample",
    )(x, state, gains, w1, b1, wdw, bdw, lng, lnb, w2, b2)


def _t5_bucket(dist):
    max_exact = N_BUCKETS // 2
    df = jnp.maximum(dist, 1).astype(F32)
    large = max_exact + (jnp.log(df / max_exact) / math.log(MAX_DISTANCE / max_exact)
                         * (N_BUCKETS - max_exact)).astype(jnp.int32)
    large = jnp.minimum(large, N_BUCKETS - 1)
    return jnp.where(dist < max_exact, dist, large)


def _bias_by_step(rel_bias, g, dil, n_steps, n_heads):
    tab = rel_bias[:, g * n_heads:(g + 1) * n_heads]
    bucket = _t5_bucket(dil * jnp.arange(n_steps + 1, dtype=jnp.int32))
    onehot = (bucket[:, None] == jnp.arange(N_BUCKETS, dtype=jnp.int32)[None, :]).astype(F32)
    return jnp.dot(onehot, tab, precision=lax.Precision.HIGHEST)


def _band_bias(by_step, n_steps, nk):
    off = nk - QB
    period = 2 * nk
    heads = by_step.shape[1]
    base = jnp.concatenate([by_step[::-1].T, jnp.full((heads, period - n_steps - 1), NEG_INF, F32)], axis=1)
    w = jnp.roll(base, off - n_steps, axis=1)
    r = period - 1
    flat = jnp.tile(w, (1, -(-(QB * r) // period)))[:, :QB * r]
    band = flat.reshape(heads, QB, r)[:, :, :nk]
    if off > 0:
        hidden = (np.arange(nk) < off)[None, None, :]
        band = jnp.stack([band, jnp.where(hidden, NEG_INF, band)])
    else:
        band = band[None]
    per = LANES // HEAD_DIM
    return band.reshape(band.shape[0], heads // per, per * QB, nk)


def _window_bias(by_step, dil, n_steps, nt):
    base = jnp.repeat(by_step[n_steps:0:-1].T, dil, axis=1)
    bias = jnp.stack([jnp.roll(base, t, axis=1) for t in range(nt)], axis=1)
    n = dil * n_steps
    delta = n + np.arange(nt)[:, None] - np.arange(n)[None, :]
    attended = (delta % dil == 0) & (delta <= dil * n_steps)
    return jnp.where(attended[None], bias, NEG_INF)


def _new_key_bias(by_step, dil, n_steps, nt):
    zero = jnp.zeros_like(by_step[0])
    rows = []
    for t in range(nt):
        cols = []
        for tp in range(nt):
            s, rem = divmod(t - tp, dil)
            cols.append(by_step[s] if (t >= tp and rem == 0 and s <= n_steps) else zero)
        rows.append(jnp.stack(cols, axis=1))
    return jnp.tile(jnp.stack(rows, axis=1), (1, 1, LANES // nt))


def _time_major(a):
    b, t, d = a.shape
    return a.transpose(1, 0, 2).reshape(t * b, d)


def _batch_major(a, b):
    return a.reshape(-1, b, a.shape[1]).transpose(1, 0, 2)


def _kv_window_layout(c):
    return c.transpose(0, 2, 3, 4, 1)


def _kv_window_unlayout(c):
    return c.transpose(0, 4, 1, 2, 3)


def kernel(x_prompt, x_sample, state_pool, cache_kv_g0, cache_kv_g1, cache_kv_g2, state_conv, norm_gains, rel_bias, pool_w, pool_scale, attn_w_qkv, attn_w_o, conv_w_pw1, conv_b_pw1, conv_w_dw, conv_b_dw, conv_ln_g, conv_ln_b, conv_w_pw2, conv_b_pw2, ffn_w_up, ffn_w_down):
    batch, seq, d = x_prompt.shape
    dec_b, dec_t, _ = x_sample.shape
    depth = norm_gains.shape[0]
    n_heads = attn_w_o.shape[1] // HEAD_DIM
    e_dim = n_heads * HEAD_DIM
    n_groups = len(ATTN_GROUPS)
    dils = tuple(dil for _, dil in ATTN_GROUPS)
    caches_in = (cache_kv_g0, cache_kv_g1, cache_kv_g2)
    n_tok = batch * seq
    n_dec = dec_b * dec_t

    xp = x_prompt
    xs = _time_major(x_sample)

    expand = jnp.asarray(np.kron(np.eye(LANES, n_heads), np.ones((1, HEAD_DIM))), BF16)

    pool_p, pool_s, conv_p, conv_s = [], [], [], []
    kv_p = [[] for _ in ATTN_GROUPS]
    kv_s = [[] for _ in ATTN_GROUPS]
    for i in range(depth):
        kind, j = i % 3, i // 3
        gains = norm_gains[i]
        if kind == 0:
            pw, ps = pool_w[j].astype(BF16), pool_scale[j][None, :]
            xp, h_last = _pool_p(xp, gains, pw, ps, ts=512)
            pool_p.append(h_last[:, POOL_CARRY - POOL_STATE:])
            xs, new_state = _pool_s(xs, _time_major(state_pool[j]), gains, pw, ps,
                                    nb=dec_b, nt=dec_t, pos0=PAST_LEN)
            pool_s.append(_batch_major(new_state, dec_b))
        elif kind == 1:
            w4 = attn_w_qkv[j].reshape(d, n_groups, 3, e_dim)
            w_qkv = attn_w_qkv[j].astype(BF16)
            w_q = w4[:, :, 0].reshape(d, n_groups * e_dim).astype(BF16)
            wt_kv = w4[:, :, 1:3].reshape(d, n_groups * 2 * e_dim).T.astype(BF16)
            w_o = attn_w_o[j].astype(BF16)
            by_step = [_bias_by_step(rel_bias, g, dil, win // dil, n_heads)
                       for g, (win, dil) in enumerate(ATTN_GROUPS)]
            qkv = _qkv_p(xp, gains, w_qkv, dils, e_dim, tm=512)
            os_, ms, ls = [], [], []
            for g, (win, dil) in enumerate(ATTN_GROUPS):
                n_steps = win // dil
                nk = 2 * QB if seq // dil > QB else QB
                bias = _band_bias(by_step[g], n_steps, nk)
                o, m, l = _attn_p(qkv[3 * g], qkv[3 * g + 1], qkv[3 * g + 2], bias, g, n_heads)
                os_.append(o)
                ms.append(m)
                ls.append(l)
                keep = min(win, seq)
                kvt = _kvt(xp, gains, wt_kv, g, 2 * e_dim, keep, tm=min(keep, 512))
                kv_p[g].append(_kv_window_unlayout(kvt.reshape(batch, 2, n_heads, HEAD_DIM, keep)))
            xp = _merge(xp, os_, ms, ls, expand, w_o, gains, tm=512)
            xs_b = _batch_major(xs, dec_b)
            xs_flat = xs_b.reshape(n_dec, d)
            n_chunks = ffn_w_up.shape[2] // FFN_CHUNK
            hc = n_heads // n_chunks
            q_s = _q_s(xs_flat, gains, w_q).reshape(dec_b, dec_t, n_groups, n_chunks, hc * HEAD_DIM)
            q_s = q_s.transpose(0, 2, 3, 1, 4)
            kvt_new = _kvt(xs_flat[None], gains, wt_kv, 0, n_groups * 2 * e_dim, n_dec, tm=n_dec)[0]
            groups = tuple((dil, win // dil) for win, dil in ATTN_GROUPS)
            caches = [_kv_window_layout(c[j]) for c in caches_in]
            bias_cs = [_window_bias(by_step[g], dil, n_steps, dec_t) for g, (dil, n_steps) in enumerate(groups)]
            bias_ns = [_new_key_bias(by_step[g], dil, n_steps, dec_t) for g, (dil, n_steps) in enumerate(groups)]
            w_up = ffn_w_up[i].reshape(d, n_chunks, FFN_CHUNK).transpose(1, 0, 2).astype(BF16)
            w_down = ffn_w_down[i].astype(BF16)
            xp, os_, lses, new_caches = _ffn_attn_s(xp.reshape(n_tok, d), gains, w_up, w_down, q_s, kvt_new,
                                                    caches, bias_cs, bias_ns, groups, dec_t, tm=n_tok // dec_b)
            xp = xp.reshape(batch, seq, d)
            ms, ls = [], []
            for g in range(n_groups):
                kv_s[g].append(_kv_window_unlayout(new_caches[g]))
                lse = lses[g][..., :hc].transpose(0, 2, 1, 3).reshape(1, 1, n_dec, n_heads)
                ms.append(jnp.pad(lse, ((0, 0), (0, 0), (0, 0), (0, LANES - n_heads))))
                ls.append(jnp.ones((1, 1, n_dec, LANES), F32))
            os_ = [o.transpose(0, 3, 1, 2).reshape(1, 1, n_dec, e_dim) for o in os_]
            xs_flat = _merge(xs_flat[None], os_, ms, ls, expand, w_o, gains, tm=n_dec)[0]
            xs = _time_major(xs_flat.reshape(dec_b, dec_t, d))
            xs = _ffn(xs, gains, ffn_w_up[i].astype(BF16), w_down, tm=n_dec)
            continue
        else:
            cw = (conv_w_pw1[j].astype(BF16), conv_b_pw1[j][None, :], conv_w_dw[j], conv_b_dw[j][None, :],
                  conv_ln_g[j][None, :], conv_ln_b[j][None, :], conv_w_pw2[j].astype(BF16), conv_b_pw2[j][None, :])
            xp, u_last = _conv_p(xp, gains, *cw, ts=512)
            conv_p.append(u_last[:, CARRY_ROWS - (CONV_WIDTH - 1):])
            xs, new_state = _conv_s(xs, _time_major(state_conv[j]), gains, *cw, nb=dec_b, nt=dec_t)
            conv_s.append(_batch_major(new_state, dec_b))
        w_up, w_down = ffn_w_up[i].astype(BF16), ffn_w_down[i].astype(BF16)
        xp = _ffn(xp.reshape(n_tok, d), gains, w_up, w_down, tm=512).reshape(batch, seq, d)
        xs = _ffn(xs, gains, w_up, w_down, tm=n_dec)
    return (xp, _batch_major(xs, dec_b),
            jnp.stack(pool_p), jnp.stack(pool_s),
            jnp.stack(kv_p[0]), jnp.stack(kv_s[0]),
            jnp.stack(kv_p[1]), jnp.stack(kv_s[1]),
            jnp.stack(kv_p[2]), jnp.stack(kv_s[2]),
            jnp.stack(conv_p), jnp.stack(conv_s))
```

```python
import functools
import math

import jax
import jax.numpy as jnp
import numpy as np
from jax import lax
from jax.experimental import pallas as pl
from jax.experimental.pallas import tpu as pltpu

F32 = jnp.float32
BF16 = jnp.bfloat16

EPS = 1e-6
NEG_INF = -1e30
POOL_WINDOWS = (2, 4, 8, 16)
POOL_STATE = max(POOL_WINDOWS) - 1
PAST_LEN = 8192
ATTN_GROUPS = ((128, 1), (512, 4), (2048, 16))
HEAD_DIM = 64
QB = 128
ATTN_AHEAD = 3
N_BUCKETS = 32
MAX_DISTANCE = 2048
CONV_WIDTH = 31
CARRY_ROWS = 32
POOL_CARRY = 16
LANES = 128
SUBLANES = 8
CONV_CHUNK_ROWS = 64
CONV_COPY_ROWS = 128

VMEM_LIMIT = 56 * 1024 * 1024
QKV_GROUP_SPLIT = ((0, 2), (2, 3))
FFN_CHUNK = 512

NT_DIMS = (((1,), (1,)), ((), ()))


def _params(*sem):
    return pltpu.CompilerParams(dimension_semantics=sem, vmem_limit_bytes=VMEM_LIMIT)


def _resident(shape):
    nd = len(shape)
    return pl.BlockSpec(shape, lambda *_: (0,) * nd, pipeline_mode=pl.Buffered(1))


def _rms(x, g):
    return x * lax.rsqrt(jnp.mean(x * x, axis=-1, keepdims=True) + EPS) * g


def _dot(a, b):
    return jnp.dot(a, b, preferred_element_type=F32)


def _dot_nt(a, b):
    return lax.dot_general(a, b, NT_DIMS, preferred_element_type=F32)


def _ffn_body(x_ref, g_ref, wup_ref, wdn_ref, o_ref, *, f_chunk):
    x = x_ref[...]
    h = _rms(x, g_ref[2:3]).astype(BF16)
    acc = jnp.zeros_like(x)
    for c in range(wup_ref.shape[2] // f_chunk):
        a = jnp.maximum(_dot(h, wup_ref[0, :, c * f_chunk:(c + 1) * f_chunk]), 0.0)
        acc = acc + _dot((a * a).astype(BF16), wdn_ref[0, c * f_chunk:(c + 1) * f_chunk, :])
    o_ref[...] = x + _rms(acc, g_ref[3:4])


def _layer_weights(w, layer):
    nd = w.ndim - 1
    return pl.BlockSpec((1,) + w.shape[1:], lambda *_: (layer,) + (0,) * nd, pipeline_mode=pl.Buffered(1))


def _ffn(x, gains, w_up, w_down, layer, tm):
    n, d = x.shape
    return pl.pallas_call(
        functools.partial(_ffn_body, f_chunk=1024),
        grid=(n // tm,),
        in_specs=[pl.BlockSpec((tm, d), lambda i: (i, 0)), _resident(gains.shape),
                  _layer_weights(w_up, layer), _layer_weights(w_down, layer)],
        out_specs=pl.BlockSpec((tm, d), lambda i: (i, 0)),
        out_shape=jax.ShapeDtypeStruct((n, d), F32),
        compiler_params=_params("parallel"),
        name="ffn",
    )(x, gains, w_up, w_down)


def _pool_p_body(x_ref, g_ref, pw_ref, ps_ref, o_ref, hl_ref, carry_ref, *, ts):
    j = pl.program_id(1)

    @pl.when(j == 0)
    def _():
        carry_ref[...] = jnp.zeros_like(carry_ref)

    x = x_ref[0]
    h = _rms(x, g_ref[0:1])
    ext = jnp.concatenate([carry_ref[...], h], axis=0)
    sums = []
    s = ext
    for k in (1, 2, 4, 8):
        s = s + pltpu.roll(s, k, axis=0)
        sums.append(s)
    pos = j * ts + lax.broadcasted_iota(jnp.int32, (ts, 1), 0)
    gw = pw_ref.shape[1]
    ys = []
    for gi, (w, s) in enumerate(zip(POOL_WINDOWS, sums)):
        cols = slice(gi * gw, (gi + 1) * gw)
        inv_cnt = 1.0 / jnp.minimum(pos + 1, w).astype(F32)
        pooled = s[POOL_CARRY:, cols] * inv_cnt - h[:, cols]
        ys.append(_dot(pooled.astype(BF16), pw_ref[gi]))
    y = jnp.concatenate(ys, axis=1) * ps_ref[...]
    o_ref[0] = x + _rms(y, g_ref[1:2])
    carry_ref[...] = h[ts - POOL_CARRY:, :]
    hl_ref[0] = h[ts - POOL_CARRY:, :]


def _pool_p(x, gains, pw, ps, ts):
    b, s, d = x.shape
    return pl.pallas_call(
        functools.partial(_pool_p_body, ts=ts),
        grid=(b, s // ts),
        in_specs=[pl.BlockSpec((1, ts, d), lambda i, j: (i, j, 0)), _resident(gains.shape),
                  _resident(pw.shape), _resident(ps.shape)],
        out_specs=[pl.BlockSpec((1, ts, d), lambda i, j: (i, j, 0)),
                   pl.BlockSpec((1, POOL_CARRY, d), lambda i, j: (i, 0, 0))],
        out_shape=[jax.ShapeDtypeStruct((b, s, d), F32),
                   jax.ShapeDtypeStruct((b, POOL_CARRY, d), F32)],
        scratch_shapes=[pltpu.VMEM((POOL_CARRY, d), F32)],
        compiler_params=_params("parallel", "arbitrary"),
        name="pool_prompt",
    )(x, gains, pw, ps)


def _pool_s_body(x_ref, st_ref, g_ref, pw_ref, ps_ref, o_ref, ns_ref, *, nb, nt, pos0):
    x = x_ref[...]
    h = _rms(x, g_ref[0:1])
    ext = [st_ref[i * nb:(i + 1) * nb, :] for i in range(POOL_STATE)]
    ext += [h[t * nb:(t + 1) * nb, :] for t in range(nt)]
    gw = pw_ref.shape[1]
    for t in range(nt):
        ys = []
        for gi, w in enumerate(POOL_WINDOWS):
            cols = slice(gi * gw, (gi + 1) * gw)
            end = POOL_STATE + t
            acc = ext[end][:, cols]
            for i in range(1, min(w, end + 1)):
                acc = acc + ext[end - i][:, cols]
            pooled = acc / float(min(pos0 + t + 1, w)) - ext[end][:, cols]
            ys.append(_dot(pooled.astype(BF16), pw_ref[gi]))
        y = jnp.concatenate(ys, axis=1) * ps_ref[...]
        o_ref[t * nb:(t + 1) * nb, :] = x[t * nb:(t + 1) * nb, :] + _rms(y, g_ref[1:2])
    for i in range(POOL_STATE):
        ns_ref[i * nb:(i + 1) * nb, :] = ext[nt + i]


def _pool_s(x, state, gains, pw, ps, nb, nt, pos0):
    return pl.pallas_call(
        functools.partial(_pool_s_body, nb=nb, nt=nt, pos0=pos0),
        out_shape=[jax.ShapeDtypeStruct(x.shape, F32), jax.ShapeDtypeStruct(state.shape, F32)],
        compiler_params=pltpu.CompilerParams(vmem_limit_bytes=VMEM_LIMIT),
        name="pool_sample",
    )(x, state, gains, pw, ps)


def _kept_tiles(seq, keep, tm):
    rows = min(keep, tm)
    tiles = keep // rows
    return seq // tm - tiles, tiles, rows


def _qkv_p_body(x_ref, g_ref, w_ref, *refs, dils, keeps, seq, tm):
    ng = len(dils)
    out_refs, kvt_refs, ybuf_ref = refs[:3 * ng], refs[3 * ng:4 * ng], refs[-1]
    e = out_refs[0].shape[-1]
    j = pl.program_id(1)
    h = _rms(x_ref[0], g_ref[0:1]).astype(BF16)
    for g, dil in enumerate(dils):
        first, _, rows = _kept_tiles(seq, keeps[g], tm)
        for c in range(3):
            idx = g * 3 + c
            y = _dot(h, w_ref[:, idx * e:(idx + 1) * e])
            if c == 0:
                y = y * (HEAD_DIM ** -0.5)
            else:
                def keep_window(y=y, g=g, c=c, rows=rows):
                    kvt_refs[g][0, (c - 1) * e:c * e, :] = y[tm - rows:, :].T

                if first == 0:
                    keep_window()
                else:
                    pl.when(j >= first)(keep_window)
            if dil == 1:
                out_refs[idx][0, 0] = y.astype(BF16)
            else:
                slot = idx % 2
                for cc in range(e // LANES):
                    ybuf_ref[slot, cc] = y[:, cc * LANES:(cc + 1) * LANES]
                for r in range(dil):
                    out_refs[idx][0, r] = jnp.concatenate(
                        [ybuf_ref[slot, cc, pl.ds(r, tm // dil, stride=dil), :] for cc in range(e // LANES)],
                        axis=1).astype(BF16)


def _qkv_p(x, gains, w, first_group, dils, keeps, e, tm):
    b, s, d = x.shape
    width = len(dils) * 3 * e
    assert (first_group * 3 * e) % width == 0
    w_spec = pl.BlockSpec((d, width), lambda i, j: (0, first_group * 3 * e // width), pipeline_mode=pl.Buffered(1))
    out_specs, out_shape = [], []
    for dil in dils:
        for _ in range(3):
            out_specs.append(pl.BlockSpec((1, dil, tm // dil, e), lambda i, j: (i, 0, j, 0)))
            out_shape.append(jax.ShapeDtypeStruct((b, dil, s // dil, e), BF16))
    for keep in keeps:
        first, _, rows = _kept_tiles(s, keep, tm)
        out_specs.append(pl.BlockSpec((1, 2 * e, rows), lambda i, j, first=first: (i, 0, jnp.maximum(j - first, 0))))
        out_shape.append(jax.ShapeDtypeStruct((b, 2 * e, keep), F32))
    return pl.pallas_call(
        functools.partial(_qkv_p_body, dils=dils, keeps=tuple(keeps), seq=s, tm=tm),
        grid=(b, s // tm),
        in_specs=[pl.BlockSpec((1, tm, d), lambda i, j: (i, j, 0)), _resident(gains.shape), w_spec],
        out_specs=out_specs,
        out_shape=out_shape,
        scratch_shapes=[pltpu.VMEM((2, e // LANES, tm, LANES), F32)],
        compiler_params=_params("parallel", "arbitrary"),
        name="qkv_prompt",
    )(x, gains, w)


def _qkv_s_body(x_ref, g_ref, w_ref, q_ref, kvt_ref, *, n_groups):
    e = w_ref.shape[1] // (3 * n_groups)
    h = _rms(x_ref[...], g_ref[0:1]).astype(BF16)
    for g in range(n_groups):
        q_ref[:, g * e:(g + 1) * e] = _dot(h, w_ref[:, 3 * g * e:(3 * g + 1) * e]) * (HEAD_DIM ** -0.5)
        for c in (1, 2):
            kv = _dot(h, w_ref[:, (3 * g + c) * e:(3 * g + c + 1) * e])
            kvt_ref[(2 * g + c - 1) * e:(2 * g + c) * e, :] = kv.T


def _qkv_s(x, gains, w, n_groups):
    n = x.shape[0]
    e = w.shape[1] // (3 * n_groups)
    return pl.pallas_call(
        functools.partial(_qkv_s_body, n_groups=n_groups),
        out_shape=[jax.ShapeDtypeStruct((n, n_groups * e), F32), jax.ShapeDtypeStruct((n_groups * 2 * e, n), F32)],
        compiler_params=pltpu.CompilerParams(vmem_limit_bytes=VMEM_LIMIT),
        name="qkv_sample",
    )(x, gains, w)


def _attn_p_body(*refs, has_prev, n_heads):
    if has_prev:
        q_ref, kp_ref, kc_ref, vp_ref, vc_ref, bias_ref, ones_ref, o_ref, m_ref, l_ref = refs
        first = (pl.program_id(2) == 0).astype(jnp.int32)
        kv_refs = ((kp_ref, vp_ref), (kc_ref, vc_ref))
    else:
        q_ref, kc_ref, vc_ref, bias_ref, ones_ref, o_ref, m_ref, l_ref = refs
        first = 0
        kv_refs = ((kc_ref, vc_ref),)
    per = LANES // HEAD_DIM
    n_grp = n_heads // per
    lane = lax.broadcasted_iota(jnp.int32, (QB, LANES), 1)
    lane_head = lax.broadcasted_iota(jnp.int32, (1, LANES), 1) // HEAD_DIM

    def scores(grp):
        cols = slice(grp * LANES, (grp + 1) * LANES)
        q = q_ref[0, 0, :, cols]
        qq = jnp.concatenate([q * jnp.where(lane_head == hh, 1.0, 0.0).astype(BF16) for hh in range(per)], axis=0)
        k = jnp.concatenate([k_ref[0, 0, :, cols] for k_ref, _ in kv_refs], axis=0)
        return _dot_nt(qq, k) + bias_ref[first, grp]

    m_all = jnp.zeros((QB, LANES), F32)
    l_all = jnp.ones((QB, LANES), F32)
    queue = [scores(g) for g in range(min(ATTN_AHEAD, n_grp))]
    for grp in range(n_grp):
        s = queue.pop(0)
        if grp + ATTN_AHEAD < n_grp:
            queue.append(scores(grp + ATTN_AHEAD))
        cols = slice(grp * LANES, (grp + 1) * LANES)
        m = jnp.max(s, axis=-1, keepdims=True)
        e = jnp.exp(s - m).astype(BF16)
        v1 = jnp.concatenate([jnp.concatenate([v_ref[0, 0, :, cols], ones_ref[...]], axis=1) for _, v_ref in kv_refs],
                             axis=0)
        ol = _dot(e, v1)
        o, l = ol[:, :LANES], ol[:, LANES:]
        out = o[0:QB]
        for hh in range(per):
            rows = slice(hh * QB, (hh + 1) * QB)
            if hh > 0:
                out = jnp.where(lane_head == hh, o[rows], out)
            m_all = jnp.where(lane == grp * per + hh, m[rows], m_all)
            l_all = jnp.where(lane == grp * per + hh, l[rows], l_all)
        o_ref[0, 0, :, cols] = out
    m_ref[0, 0] = m_all
    l_ref[0, 0] = l_all


def _attn_p(q, k, v, bias, g, n_heads):
    batch, dil, sub, e = q.shape
    nb = sub // QB
    has_prev = nb > 1
    assert bias.shape[0] == (2 if has_prev else 1)
    ones = jnp.ones((QB, LANES), BF16)
    blk = (1, 1, QB, e)
    cur = lambda b, r, n: (b, r, n, 0)
    prev = lambda b, r, n: (b, r, jnp.maximum(n - 1, 0), 0)
    if has_prev:
        in_specs = [pl.BlockSpec(blk, cur), pl.BlockSpec(blk, prev), pl.BlockSpec(blk, cur),
                    pl.BlockSpec(blk, prev), pl.BlockSpec(blk, cur)]
        args = (q, k, k, v, v)
    else:
        in_specs = [pl.BlockSpec(blk, cur)] * 3
        args = (q, k, v)
    return pl.pallas_call(
        functools.partial(_attn_p_body, has_prev=has_prev, n_heads=n_heads),
        grid=(batch, dil, nb),
        in_specs=in_specs + [_resident(bias.shape), _resident(ones.shape)],
        out_specs=[pl.BlockSpec(blk, cur)] + [pl.BlockSpec((1, 1, QB, LANES), cur)] * 2,
        out_shape=[jax.ShapeDtypeStruct((batch, dil, sub, e), F32)]
                  + [jax.ShapeDtypeStruct((batch, dil, sub, LANES), F32)] * 2,
        compiler_params=_params("parallel", "parallel", "arbitrary"),
        name=f"attn_prompt_g{g}",
    )(*args, bias, ones)


def _expand_heads(w, e_ref):
    hi = w.astype(BF16)
    lo = (w - hi.astype(F32)).astype(BF16)
    return _dot(hi, e_ref[...]) + _dot(lo, e_ref[...])


def _merge_groups(os_, ms, ls, e_ref):
    top = functools.reduce(jnp.maximum, ms)
    es = [jnp.exp(m - top) for m in ms]
    den = functools.reduce(lambda a, b: a + b, [e * l for e, l in zip(es, ls)])
    acc = None
    for o, e in zip(os_, es):
        term = _expand_heads(e / den, e_ref) * o
        acc = term if acc is None else acc + term
    return acc


def _merge_body(x_ref, *refs, dils, tm):
    ng = len(dils)
    o_refs, m_refs, l_refs = refs[:ng], refs[ng:2 * ng], refs[2 * ng:3 * ng]
    e_ref, wo_ref, g_ref, out_ref, obuf_ref, sbuf_ref = refs[3 * ng:]
    os_, ms, ls = [], [], []
    for gi, dil in enumerate(dils):
        if dil == 1:
            os_.append(o_refs[gi][0, 0])
            ms.append(m_refs[gi][0, 0])
            ls.append(l_refs[gi][0, 0])
        else:
            n_cc = obuf_ref.shape[1]
            for r in range(dil):
                rows = pl.ds(r, tm // dil, stride=dil)
                for cc in range(n_cc):
                    obuf_ref[gi, cc, rows, :] = o_refs[gi][0, r, :, cc * LANES:(cc + 1) * LANES]
                sbuf_ref[0, gi, rows, :] = m_refs[gi][0, r]
                sbuf_ref[1, gi, rows, :] = l_refs[gi][0, r]
            os_.append(jnp.concatenate([obuf_ref[gi, cc] for cc in range(n_cc)], axis=1))
            ms.append(sbuf_ref[0, gi])
            ls.append(sbuf_ref[1, gi])
    o = _merge_groups(os_, ms, ls, e_ref)
    y = _dot(o.astype(BF16), wo_ref[...])
    out_ref[0] = x_ref[0] + _rms(y, g_ref[1:2])


def _merge(x, os_, ms, ls, expand, w_o, gains, tm):
    b, s, d = x.shape
    dils = tuple(o.shape[1] for o in os_)
    e = os_[0].shape[-1]
    grp = lambda i, j: (i, 0, j, 0)
    return pl.pallas_call(
        functools.partial(_merge_body, dils=dils, tm=tm),
        grid=(b, s // tm),
        in_specs=[pl.BlockSpec((1, tm, d), lambda i, j: (i, j, 0))]
                 + [pl.BlockSpec((1, dil, tm // dil, e), grp) for dil in dils]
                 + [pl.BlockSpec((1, dil, tm // dil, LANES), grp) for dil in dils] * 2
                 + [_resident(expand.shape), _resident(w_o.shape), _resident(gains.shape)],
        out_specs=pl.BlockSpec((1, tm, d), lambda i, j: (i, j, 0)),
        out_shape=jax.ShapeDtypeStruct((b, s, d), F32),
        scratch_shapes=[pltpu.VMEM((len(dils), e // LANES, tm, LANES), F32),
                        pltpu.VMEM((2, len(dils), tm, LANES), F32)],
        compiler_params=_params("parallel", "parallel"),
        name="attn_merge",
    )(x, *os_, *ms, *ls, expand, w_o, gains)


class _WindowAttnChunk:
    def __init__(self, b, q, knew, vnew, c_ref, bias_c, bias_n, cout_ref, *, dil, n_steps, nt, hc):
        self.b, self.q, self.knew, self.vnew, self.c_ref, self.cout_ref = b, q.astype(BF16), knew, vnew, c_ref, cout_ref
        self.bias_c, self.bias_n, self.dil, self.n_steps, self.nt = bias_c, bias_n, dil, n_steps, nt
        self.heads = [slice(hh * HEAD_DIM, (hh + 1) * HEAD_DIM) for hh in range(hc)]
        self.lane = lax.broadcasted_iota(jnp.int32, (nt, LANES), 1)

    def logits(self):
        nt, dil = self.nt, self.dil
        dn = lax.broadcasted_iota(jnp.int32, (nt, LANES), 0) - lax.rem(self.lane, nt)
        valid_n = (self.lane // nt == self.b) & (dn >= 0) & (lax.rem(dn, dil) == 0) & (dn <= dil * self.n_steps)
        self.s_c = [_dot(self.q[:, sl], self.c_ref[0, 0, hh].astype(BF16)) + self.bias_c[hh]
                    for hh, sl in enumerate(self.heads)]
        self.s_n = [jnp.where(valid_n, _dot(self.q[:, sl], self.knew[sl, :].astype(BF16)) + self.bias_n[hh], NEG_INF)
                    for hh, sl in enumerate(self.heads)]

    def softmax(self):
        ms = [jnp.maximum(jnp.max(a, axis=-1, keepdims=True), jnp.max(c, axis=-1, keepdims=True))
              for a, c in zip(self.s_c, self.s_n)]
        e_c = [jnp.exp(a - m) for a, m in zip(self.s_c, ms)]
        e_n = [jnp.exp(c - m) for c, m in zip(self.s_n, ms)]
        ls = [jnp.sum(a, axis=-1, keepdims=True) + jnp.sum(c, axis=-1, keepdims=True) for a, c in zip(e_c, e_n)]
        self.p_c = [(e / l).astype(BF16) for e, l in zip(e_c, ls)]
        self.p_n = [(e / l).astype(BF16) for e, l in zip(e_n, ls)]
        self.lse = jnp.zeros((self.nt, LANES), F32)
        for hh, (m, l) in enumerate(zip(ms, ls)):
            self.lse = jnp.where(self.lane == hh, m + jnp.log(l), self.lse)

    def output(self):
        os_ = [_dot_nt(self.c_ref[0, 1, hh].astype(BF16), self.p_c[hh])
               + _dot_nt(self.vnew[sl, :].astype(BF16), self.p_n[hh]) for hh, sl in enumerate(self.heads)]
        return jnp.concatenate(os_, axis=0), self.lse

    def shift(self):
        n, nt = self.c_ref.shape[-1], self.nt
        tail = lax.broadcasted_iota(jnp.int32, (HEAD_DIM, LANES), 1) >= LANES - nt
        lanes_to_tail = LANES - nt - self.b * nt
        for hh, sl in enumerate(self.heads):
            for kv, new in enumerate((self.knew, self.vnew)):
                rolled = pltpu.roll(self.c_ref[0, kv, hh], n - nt, axis=1)
                placed = pltpu.roll(new[sl, :], lanes_to_tail, axis=1)
                if n > LANES:
                    self.cout_ref[0, kv, hh, :, 0:n - LANES] = rolled[:, 0:n - LANES]
                self.cout_ref[0, kv, hh, :, n - LANES:n] = jnp.where(tail, placed, rolled[:, n - LANES:n])


def _ffn_attn_s_body(x_ref, g_ref, wup_ref, wdn_ref, q_ref, kvt_ref, *refs, groups, nt, hc, f_chunk, n_chunks):
    ng = len(groups)
    c_refs, bc_refs, bn_refs = refs[:ng], refs[ng:2 * ng], refs[2 * ng:3 * ng]
    y_ref = refs[3 * ng]
    o_refs, lse_refs, cout_refs = (refs[3 * ng + 1 + k * ng:3 * ng + 1 + (k + 1) * ng] for k in range(3))
    h_ref, acc_ref = refs[-2:]
    i, j = pl.program_id(0), pl.program_id(1)

    @pl.when(j == 0)
    def _():
        h_ref[...] = _rms(x_ref[...], g_ref[2:3]).astype(BF16)
        acc_ref[...] = jnp.zeros_like(acc_ref)

    cw = hc * HEAD_DIM
    e = cw * n_chunks
    attn = []
    for g, (dil, n_steps) in enumerate(groups):
        knew = kvt_ref[pl.ds(pl.multiple_of((2 * g) * e + j * cw, cw), cw), :]
        vnew = kvt_ref[pl.ds(pl.multiple_of((2 * g + 1) * e + j * cw, cw), cw), :]
        bias_c = [bc_refs[g][j * hc + hh] for hh in range(hc)]
        bias_n = [bn_refs[g][j * hc + hh] for hh in range(hc)]
        attn.append(_WindowAttnChunk(i, q_ref[i, g, j], knew, vnew, c_refs[g], bias_c, bias_n, cout_refs[g],
                                     dil=dil, n_steps=n_steps, nt=nt, hc=hc))
    hidden = pl.ds(pl.multiple_of(j * f_chunk, f_chunk), f_chunk)
    up = jnp.maximum(_dot(h_ref[...], wup_ref[0, :, hidden]), 0.0)
    for a in attn:
        a.logits()
    acc_ref[...] += _dot((up * up).astype(BF16), wdn_ref[0, hidden, :])
    for a in attn:
        a.softmax()
    for g, a in enumerate(attn):
        o_refs[g][0, j], lse_refs[g][0, j] = a.output()
    for a in attn:
        a.shift()

    @pl.when(j == n_chunks - 1)
    def _():
        y_ref[...] = x_ref[...] + _rms(acc_ref[...], g_ref[3:4])


def _ffn_attn_s(x, gains, w_up, w_down, layer, q, kvt_new, caches, bias_cs, bias_ns, groups, nt, tm):
    n, d = x.shape
    n_chunks, f_chunk = q.shape[2], w_up.shape[2] // q.shape[2]
    b, _, n_heads, _, _ = caches[0].shape
    hc = n_heads // n_chunks
    cw = hc * HEAD_DIM
    assert n // tm == b and kvt_new.shape[1] == LANES == b * nt and hc * n_chunks == n_heads
    assert all(c.shape[-1] == dil * n_steps for c, (dil, n_steps) in zip(caches, groups))
    row = lambda i, j: (i, 0)
    win = lambda i, j: (i, 0, j, 0, 0)
    per_row = lambda i, j: (i, 0, 0, 0)
    win_specs = [pl.BlockSpec((1, 2, hc, HEAD_DIM, c.shape[-1]), win) for c in caches]
    ng = len(caches)
    outs = pl.pallas_call(
        functools.partial(_ffn_attn_s_body, groups=groups, nt=nt, hc=hc, f_chunk=f_chunk, n_chunks=n_chunks),
        grid=(b, n_chunks),
        in_specs=[pl.BlockSpec((tm, d), row), _resident(gains.shape), _layer_weights(w_up, layer),
                  _layer_weights(w_down, layer), _resident(q.shape), _resident(kvt_new.shape)]
                 + win_specs + [_resident(a.shape) for a in bias_cs] + [_resident(a.shape) for a in bias_ns],
        out_specs=[pl.BlockSpec((tm, d), row)]
                  + [pl.BlockSpec((1, n_chunks, cw, nt), per_row)] * ng
                  + [pl.BlockSpec((1, n_chunks, nt, LANES), per_row)] * ng + win_specs,
        out_shape=[jax.ShapeDtypeStruct((n, d), F32)]
                  + [jax.ShapeDtypeStruct((b, n_chunks, cw, nt), F32)] * ng
                  + [jax.ShapeDtypeStruct((b, n_chunks, nt, LANES), F32)] * ng
                  + [jax.ShapeDtypeStruct(c.shape, F32) for c in caches],
        scratch_shapes=[pltpu.VMEM((tm, d), BF16), pltpu.VMEM((tm, d), F32)],
        compiler_params=_params("parallel", "arbitrary"),
        name="ffn_with_sample_attn",
    )(x, gains, w_up, w_down, q, kvt_new, *caches, *bias_cs, *bias_ns)
    return outs[0], outs[1:1 + ng], outs[1 + ng:1 + 2 * ng], outs[1 + 2 * ng:]


def _conv_tail(y, x, lng_ref, lnb_ref, w2_ref, b2_ref, g_ref):
    mu = jnp.mean(y, axis=-1, keepdims=True)
    yc = y - mu
    var = jnp.mean(yc * yc, axis=-1, keepdims=True)
    yn = yc * lax.rsqrt(var + EPS) * lng_ref[...] + lnb_ref[...]
    z = _dot((yn * jax.nn.sigmoid(yn)).astype(BF16), w2_ref[...]) + b2_ref[...]
    return x + _rms(z, g_ref[1:2])


def _glu(x, g_ref, w1_ref, b1_ref):
    d = x.shape[1]
    a = _dot(_rms(x, g_ref[0:1]).astype(BF16), w1_ref[...]) + b1_ref[...]
    return a[:, :d] * jax.nn.sigmoid(a[:, d:])


def _conv_p_body(x_ref, g_ref, w1_ref, b1_ref, wdw_ref, bdw_ref, lng_ref, lnb_ref, w2_ref, b2_ref,
                 o_ref, ul_ref, ush_ref, y_ref, *, ts):
    j = pl.program_id(1)
    d = y_ref.shape[1]
    rows_ext = CARRY_ROWS + ts

    @pl.when(j == 0)
    def _():
        ush_ref[0, 0:CARRY_ROWS, :] = jnp.zeros((CARRY_ROWS, d), F32)

    x = x_ref[0]
    ush_ref[0, CARRY_ROWS:, :] = _glu(x, g_ref, w1_ref, b1_ref)
    n_sh = rows_ext - SUBLANES
    for m in range(1, SUBLANES):
        for r0 in range(0, n_sh, CONV_COPY_ROWS):
            r1 = min(r0 + CONV_COPY_ROWS, n_sh)
            ush_ref[m, r0:r1, :] = ush_ref[0, r0 + m:r1 + m, :]
    lead = CARRY_ROWS - (CONV_WIDTH - 1)

    def chunk(c, carry):
        r0 = pl.multiple_of(c * CONV_CHUNK_ROWS, CONV_CHUNK_ROWS)
        for cb in range(d // LANES):
            cols = slice(cb * LANES, (cb + 1) * LANES)
            acc = jnp.broadcast_to(bdw_ref[:, cols], (CONV_CHUNK_ROWS, LANES))
            for k in range(CONV_WIDTH):
                a, m = divmod(lead + k, SUBLANES)
                acc = acc + wdw_ref[k:k + 1, cols] * ush_ref[m, pl.ds(r0 + a * SUBLANES, CONV_CHUNK_ROWS), cols]
            y_ref[pl.ds(r0, CONV_CHUNK_ROWS), cols] = acc
        return carry

    lax.fori_loop(0, ts // CONV_CHUNK_ROWS, chunk, 0)
    o_ref[0] = _conv_tail(y_ref[...], x, lng_ref, lnb_ref, w2_ref, b2_ref, g_ref)
    last = ush_ref[0, ts:ts + CARRY_ROWS, :]
    ush_ref[0, 0:CARRY_ROWS, :] = last
    ul_ref[0] = last


def _conv_p(x, gains, w1, b1, wdw, bdw, lng, lnb, w2, b2, ts):
    b, s, d = x.shape
    consts = (gains, w1, b1, wdw, bdw, lng, lnb, w2, b2)
    return pl.pallas_call(
        functools.partial(_conv_p_body, ts=ts),
        grid=(b, s // ts),
        in_specs=[pl.BlockSpec((1, ts, d), lambda i, j: (i, j, 0))] + [_resident(a.shape) for a in consts],
        out_specs=[pl.BlockSpec((1, ts, d), lambda i, j: (i, j, 0)),
                   pl.BlockSpec((1, CARRY_ROWS, d), lambda i, j: (i, 0, 0))],
        out_shape=[jax.ShapeDtypeStruct((b, s, d), F32), jax.ShapeDtypeStruct((b, CARRY_ROWS, d), F32)],
        scratch_shapes=[pltpu.VMEM((SUBLANES, CARRY_ROWS + ts, d), F32), pltpu.VMEM((ts, d), F32)],
        compiler_params=_params("parallel", "arbitrary"),
        name="conv_prompt",
    )(x, *consts)


def _conv_s_body(x_ref, st_ref, g_ref, w1_ref, b1_ref, wdw_ref, bdw_ref, lng_ref, lnb_ref, w2_ref, b2_ref,
                 o_ref, ns_ref, *, nb, nt):
    x = x_ref[...]
    u = _glu(x, g_ref, w1_ref, b1_ref)
    n_prev = CONV_WIDTH - 1
    ext = [st_ref[i * nb:(i + 1) * nb, :] for i in range(n_prev)]
    ext += [u[t * nb:(t + 1) * nb, :] for t in range(nt)]
    ys = []
    for t in range(nt):
        y = None
        for k in range(CONV_WIDTH):
            term = wdw_ref[k:k + 1, :] * ext[t + k]
            y = term if y is None else y + term
        ys.append(y + bdw_ref[...])
    o_ref[...] = _conv_tail(jnp.concatenate(ys, axis=0), x, lng_ref, lnb_ref, w2_ref, b2_ref, g_ref)
    for i in range(n_prev):
        ns_ref[i * nb:(i + 1) * nb, :] = ext[nt + i]


def _conv_s(x, state, gains, w1, b1, wdw, bdw, lng, lnb, w2, b2, nb, nt):
    return pl.pallas_call(
        functools.partial(_conv_s_body, nb=nb, nt=nt),
        out_shape=[jax.ShapeDtypeStruct(x.shape, F32), jax.ShapeDtypeStruct(state.shape, F32)],
        compiler_params=pltpu.CompilerParams(vmem_limit_bytes=VMEM_LIMIT),
        name="conv_sample",
    )(x, state, gains, w1, b1, wdw, bdw, lng, lnb, w2, b2)


def _t5_bucket(dist):
    max_exact = N_BUCKETS // 2
    df = jnp.maximum(dist, 1).astype(F32)
    large = max_exact + (jnp.log(df / max_exact) / math.log(MAX_DISTANCE / max_exact)
                         * (N_BUCKETS - max_exact)).astype(jnp.int32)
    large = jnp.minimum(large, N_BUCKETS - 1)
    return jnp.where(dist < max_exact, dist, large)


def _bias_by_step(rel_bias, g, dil, n_steps, n_heads):
    tab = rel_bias[:, g * n_heads:(g + 1) * n_heads]
    bucket = _t5_bucket(dil * jnp.arange(n_steps + 1, dtype=jnp.int32))
    onehot = (bucket[:, None] == jnp.arange(N_BUCKETS, dtype=jnp.int32)[None, :]).astype(F32)
    return jnp.dot(onehot, tab, precision=lax.Precision.HIGHEST)


def _band_bias(by_step, n_steps, nk):
    off = nk - QB
    period = 2 * nk
    heads = by_step.shape[1]
    base = jnp.concatenate([by_step[::-1].T, jnp.full((heads, period - n_steps - 1), NEG_INF, F32)], axis=1)
    w = jnp.roll(base, off - n_steps, axis=1)
    r = period - 1
    flat = jnp.tile(w, (1, -(-(QB * r) // period)))[:, :QB * r]
    band = flat.reshape(heads, QB, r)[:, :, :nk]
    if off > 0:
        hidden = (np.arange(nk) < off)[None, None, :]
        band = jnp.stack([band, jnp.where(hidden, NEG_INF, band)])
    else:
        band = band[None]
    per = LANES // HEAD_DIM
    return band.reshape(band.shape[0], heads // per, per * QB, nk)


def _window_bias(by_step, dil, n_steps, nt):
    base = jnp.repeat(by_step[n_steps:0:-1].T, dil, axis=1)
    bias = jnp.stack([jnp.roll(base, t, axis=1) for t in range(nt)], axis=1)
    n = dil * n_steps
    delta = n + np.arange(nt)[:, None] - np.arange(n)[None, :]
    attended = (delta % dil == 0) & (delta <= dil * n_steps)
    return jnp.where(attended[None], bias, NEG_INF)


def _new_key_bias(by_step, dil, n_steps, nt):
    zero = jnp.zeros_like(by_step[0])
    rows = []
    for t in range(nt):
        cols = []
        for tp in range(nt):
            s, rem = divmod(t - tp, dil)
            cols.append(by_step[s] if (t >= tp and rem == 0 and s <= n_steps) else zero)
        rows.append(jnp.stack(cols, axis=1))
    return jnp.tile(jnp.stack(rows, axis=1), (1, 1, LANES // nt))


def _time_major(a):
    b, t, d = a.shape
    return a.transpose(1, 0, 2).reshape(t * b, d)


def _batch_major(a, b):
    return a.reshape(-1, b, a.shape[1]).transpose(1, 0, 2)


def _kv_window_layout(c):
    return c.transpose(0, 2, 3, 4, 1)


def _kv_window_unlayout(c):
    return c.transpose(0, 4, 1, 2, 3)


def kernel(x_prompt, x_sample, state_pool, cache_kv_g0, cache_kv_g1, cache_kv_g2, state_conv, norm_gains, rel_bias, pool_w, pool_scale, attn_w_qkv, attn_w_o, conv_w_pw1, conv_b_pw1, conv_w_dw, conv_b_dw, conv_ln_g, conv_ln_b, conv_w_pw2, conv_b_pw2, ffn_w_up, ffn_w_down):
    batch, seq, d = x_prompt.shape
    dec_b, dec_t, _ = x_sample.shape
    depth = norm_gains.shape[0]
    n_heads = attn_w_o.shape[1] // HEAD_DIM
    e_dim = n_heads * HEAD_DIM
    n_groups = len(ATTN_GROUPS)
    dils = tuple(dil for _, dil in ATTN_GROUPS)
    caches_in = (cache_kv_g0, cache_kv_g1, cache_kv_g2)
    n_tok = batch * seq
    n_dec = dec_b * dec_t

    xp = x_prompt
    xs = _time_major(x_sample)

    expand = jnp.asarray(np.kron(np.eye(LANES, n_heads), np.ones((1, HEAD_DIM))), BF16)

    w_up, w_down = ffn_w_up.astype(BF16), ffn_w_down.astype(BF16)

    pool_p, pool_s, conv_p, conv_s = [], [], [], []
    kv_p = [[] for _ in ATTN_GROUPS]
    kv_s = [[] for _ in ATTN_GROUPS]
    for i in range(depth):
        kind, j = i % 3, i // 3
        gains = norm_gains[i]
        if kind == 0:
            pw, ps = pool_w[j].astype(BF16), pool_scale[j][None, :]
            xp, h_last = _pool_p(xp, gains, pw, ps, ts=512)
            pool_p.append(h_last[:, POOL_CARRY - POOL_STATE:])
            xs, new_state = _pool_s(xs, _time_major(state_pool[j]), gains, pw, ps,
                                    nb=dec_b, nt=dec_t, pos0=PAST_LEN)
            pool_s.append(_batch_major(new_state, dec_b))
        elif kind == 1:
            w_qkv = attn_w_qkv[j].astype(BF16)
            w_o = attn_w_o[j].astype(BF16)
            by_step = [_bias_by_step(rel_bias, g, dil, win // dil, n_heads)
                       for g, (win, dil) in enumerate(ATTN_GROUPS)]
            keeps = [min(win, seq) for win, _ in ATTN_GROUPS]
            qkv, kvts = [], []
            for lo, hi in QKV_GROUP_SPLIT:
                outs = _qkv_p(xp, gains, w_qkv, lo, dils[lo:hi], keeps[lo:hi], e_dim, tm=512)
                qkv += outs[:3 * (hi - lo)]
                kvts += outs[3 * (hi - lo):]
            os_, ms, ls = [], [], []
            for g, (win, dil) in enumerate(ATTN_GROUPS):
                n_steps = win // dil
                nk = 2 * QB if seq // dil > QB else QB
                bias = _band_bias(by_step[g], n_steps, nk)
                o, m, l = _attn_p(qkv[3 * g], qkv[3 * g + 1], qkv[3 * g + 2], bias, g, n_heads)
                os_.append(o)
                ms.append(m)
                ls.append(l)
                kv_p[g].append(_kv_window_unlayout(kvts[g].reshape(batch, 2, n_heads, HEAD_DIM, keeps[g])))
            xp = _merge(xp, os_, ms, ls, expand, w_o, gains, tm=512)
            xs_b = _batch_major(xs, dec_b)
            xs_flat = xs_b.reshape(n_dec, d)
            n_chunks = ffn_w_up.shape[2] // FFN_CHUNK
            hc = n_heads // n_chunks
            q_s, kvt_new = _qkv_s(xs_flat, gains, w_qkv, n_groups)
            q_s = q_s.reshape(dec_b, dec_t, n_groups, n_chunks, hc * HEAD_DIM)
            q_s = q_s.transpose(0, 2, 3, 1, 4)
            groups = tuple((dil, win // dil) for win, dil in ATTN_GROUPS)
            caches = [_kv_window_layout(c[j]) for c in caches_in]
            bias_cs = [_window_bias(by_step[g], dil, n_steps, dec_t) for g, (dil, n_steps) in enumerate(groups)]
            bias_ns = [_new_key_bias(by_step[g], dil, n_steps, dec_t) for g, (dil, n_steps) in enumerate(groups)]
            xp, os_, lses, new_caches = _ffn_attn_s(xp.reshape(n_tok, d), gains, w_up, w_down, i, q_s, kvt_new,
                                                    caches, bias_cs, bias_ns, groups, dec_t, tm=n_tok // dec_b)
            xp = xp.reshape(batch, seq, d)
            ms, ls = [], []
            for g in range(n_groups):
                kv_s[g].append(_kv_window_unlayout(new_caches[g]))
                lse = lses[g][..., :hc].transpose(0, 2, 1, 3).reshape(1, 1, n_dec, n_heads)
                ms.append(jnp.pad(lse, ((0, 0), (0, 0), (0, 0), (0, LANES - n_heads))))
                ls.append(jnp.ones((1, 1, n_dec, LANES), F32))
            os_ = [o.transpose(0, 3, 1, 2).reshape(1, 1, n_dec, e_dim) for o in os_]
            xs_flat = _merge(xs_flat[None], os_, ms, ls, expand, w_o, gains, tm=n_dec)[0]
            xs = _time_major(xs_flat.reshape(dec_b, dec_t, d))
            xs = _ffn(xs, gains, w_up, w_down, i, tm=n_dec)
            continue
        else:
            cw = (conv_w_pw1[j].astype(BF16), conv_b_pw1[j][None, :], conv_w_dw[j], conv_b_dw[j][None, :],
                  conv_ln_g[j][None, :], conv_ln_b[j][None, :], conv_w_pw2[j].astype(BF16), conv_b_pw2[j][None, :])
            xp, u_last = _conv_p(xp, gains, *cw, ts=512)
            conv_p.append(u_last[:, CARRY_ROWS - (CONV_WIDTH - 1):])
            xs, new_state = _conv_s(xs, _time_major(state_conv[j]), gains, *cw, nb=dec_b, nt=dec_t)
            conv_s.append(_batch_major(new_state, dec_b))
        xp = _ffn(xp.reshape(n_tok, d), gains, w_up, w_down, i, tm=512).reshape(batch, seq, d)
        xs = _ffn(xs, gains, w_up, w_down, i, tm=n_dec)
    return (xp, _batch_major(xs, dec_b),
            jnp.stack(pool_p), jnp.stack(pool_s),
            jnp.stack(kv_p[0]), jnp.stack(kv_s[0]),
            jnp.stack(kv_p[1]), jnp.stack(kv_s[1]),
            jnp.stack(kv_p[2]), jnp.stack(kv_s[2]),
            jnp.stack(conv_p), jnp.stack(conv_s))
```

```python
import functools
import math

import jax
import jax.numpy as jnp
import numpy as np
from jax import lax
from jax.experimental import pallas as pl
from jax.experimental.pallas import tpu as pltpu

F32 = jnp.float32
BF16 = jnp.bfloat16

EPS = 1e-6
NEG_INF = -1e30
POOL_WINDOWS = (2, 4, 8, 16)
POOL_STATE = max(POOL_WINDOWS) - 1
PAST_LEN = 8192
ATTN_GROUPS = ((128, 1), (512, 4), (2048, 16))
HEAD_DIM = 64
QB = 128
ATTN_AHEAD = 3
N_BUCKETS = 32
MAX_DISTANCE = 2048
CONV_WIDTH = 31
CARRY_ROWS = 32
POOL_CARRY = 16
LANES = 128
SUBLANES = 8
CONV_CHUNK_ROWS = 128
CONV_COPY_ROWS = 128

VMEM_LIMIT = 56 * 1024 * 1024
QKV_GROUP_SPLIT = ((0, 2), (2, 3))
MLP_CHUNK = 1024
FFN_CHUNK = 512

NT_DIMS = (((1,), (1,)), ((), ()))


def _params(*sem):
    return pltpu.CompilerParams(dimension_semantics=sem, vmem_limit_bytes=VMEM_LIMIT)


def _resident(shape):
    nd = len(shape)
    return pl.BlockSpec(shape, lambda *_: (0,) * nd, pipeline_mode=pl.Buffered(1))


def _rms(x, g):
    return x * lax.rsqrt(jnp.mean(x * x, axis=-1, keepdims=True) + EPS) * g


def _dot(a, b):
    return jnp.dot(a, b, preferred_element_type=F32)


def _dot_nt(a, b):
    return lax.dot_general(a, b, NT_DIMS, preferred_element_type=F32)


def _mlp(x, g_ref, wup_ref, wdn_ref):
    h = _rms(x, g_ref[2:3]).astype(BF16)
    acc = jnp.zeros_like(x)
    for c in range(wup_ref.shape[2] // MLP_CHUNK):
        cols = slice(c * MLP_CHUNK, (c + 1) * MLP_CHUNK)
        a = jnp.maximum(_dot(h, wup_ref[0, :, cols]), 0.0)
        acc = acc + _dot((a * a).astype(BF16), wdn_ref[0, cols, :])
    return x + _rms(acc, g_ref[3:4])


def _ffn_body(x_ref, g_ref, wup_ref, wdn_ref, o_ref):
    o_ref[...] = _mlp(x_ref[...], g_ref, wup_ref, wdn_ref)


def _layer_weights(w, layer):
    nd = w.ndim - 1
    return pl.BlockSpec((1,) + w.shape[1:], lambda *_: (layer,) + (0,) * nd, pipeline_mode=pl.Buffered(1))


def _ffn(x, gains, w_up, w_down, layer, tm):
    n, d = x.shape
    return pl.pallas_call(
        _ffn_body,
        grid=(n // tm,),
        in_specs=[pl.BlockSpec((tm, d), lambda i: (i, 0)), _resident(gains.shape),
                  _layer_weights(w_up, layer), _layer_weights(w_down, layer)],
        out_specs=pl.BlockSpec((tm, d), lambda i: (i, 0)),
        out_shape=jax.ShapeDtypeStruct((n, d), F32),
        compiler_params=_params("parallel"),
        name="ffn",
    )(x, gains, w_up, w_down)


def _pool_p_body(x_ref, g_ref, pw_ref, ps_ref, wup_ref, wdn_ref, o_ref, hl_ref, carry_ref, *, ts):
    j = pl.program_id(1)

    @pl.when(j == 0)
    def _():
        carry_ref[...] = jnp.zeros_like(carry_ref)

    x = x_ref[0]
    h = _rms(x, g_ref[0:1])
    ext = jnp.concatenate([carry_ref[...], h], axis=0)
    sums = []
    s = ext
    for k in (1, 2, 4, 8):
        s = s + pltpu.roll(s, k, axis=0)
        sums.append(s)
    pos = j * ts + lax.broadcasted_iota(jnp.int32, (ts, 1), 0)
    gw = pw_ref.shape[1]
    ys = []
    for gi, (w, s) in enumerate(zip(POOL_WINDOWS, sums)):
        cols = slice(gi * gw, (gi + 1) * gw)
        inv_cnt = 1.0 / jnp.minimum(pos + 1, w).astype(F32)
        pooled = s[POOL_CARRY:, cols] * inv_cnt - h[:, cols]
        ys.append(_dot(pooled.astype(BF16), pw_ref[gi]))
    y = jnp.concatenate(ys, axis=1) * ps_ref[...]
    carry_ref[...] = h[ts - POOL_CARRY:, :]
    hl_ref[0] = h[ts - POOL_CARRY:, :]
    o_ref[0] = _mlp(x + _rms(y, g_ref[1:2]), g_ref, wup_ref, wdn_ref)


def _pool_p(x, gains, pw, ps, w_up, w_down, layer, ts):
    b, s, d = x.shape
    return pl.pallas_call(
        functools.partial(_pool_p_body, ts=ts),
        grid=(b, s // ts),
        in_specs=[pl.BlockSpec((1, ts, d), lambda i, j: (i, j, 0)), _resident(gains.shape),
                  _resident(pw.shape), _resident(ps.shape),
                  _layer_weights(w_up, layer), _layer_weights(w_down, layer)],
        out_specs=[pl.BlockSpec((1, ts, d), lambda i, j: (i, j, 0)),
                   pl.BlockSpec((1, POOL_CARRY, d), lambda i, j: (i, 0, 0))],
        out_shape=[jax.ShapeDtypeStruct((b, s, d), F32),
                   jax.ShapeDtypeStruct((b, POOL_CARRY, d), F32)],
        scratch_shapes=[pltpu.VMEM((POOL_CARRY, d), F32)],
        compiler_params=_params("parallel", "arbitrary"),
        name="pool_mlp_prompt",
    )(x, gains, pw, ps, w_up, w_down)


def _pool_s_body(x_ref, st_ref, g_ref, pw_ref, ps_ref, o_ref, ns_ref, *, nb, nt, pos0):
    x = x_ref[...]
    h = _rms(x, g_ref[0:1])
    ext = [st_ref[i * nb:(i + 1) * nb, :] for i in range(POOL_STATE)]
    ext += [h[t * nb:(t + 1) * nb, :] for t in range(nt)]
    gw = pw_ref.shape[1]
    for t in range(nt):
        ys = []
        for gi, w in enumerate(POOL_WINDOWS):
            cols = slice(gi * gw, (gi + 1) * gw)
            end = POOL_STATE + t
            acc = ext[end][:, cols]
            for i in range(1, min(w, end + 1)):
                acc = acc + ext[end - i][:, cols]
            pooled = acc / float(min(pos0 + t + 1, w)) - ext[end][:, cols]
            ys.append(_dot(pooled.astype(BF16), pw_ref[gi]))
        y = jnp.concatenate(ys, axis=1) * ps_ref[...]
        o_ref[t * nb:(t + 1) * nb, :] = x[t * nb:(t + 1) * nb, :] + _rms(y, g_ref[1:2])
    for i in range(POOL_STATE):
        ns_ref[i * nb:(i + 1) * nb, :] = ext[nt + i]


def _pool_s(x, state, gains, pw, ps, nb, nt, pos0):
    return pl.pallas_call(
        functools.partial(_pool_s_body, nb=nb, nt=nt, pos0=pos0),
        out_shape=[jax.ShapeDtypeStruct(x.shape, F32), jax.ShapeDtypeStruct(state.shape, F32)],
        compiler_params=pltpu.CompilerParams(vmem_limit_bytes=VMEM_LIMIT),
        name="pool_sample",
    )(x, state, gains, pw, ps)


def _kept_tiles(seq, keep, tm):
    rows = min(keep, tm)
    tiles = keep // rows
    return seq // tm - tiles, tiles, rows


def _qkv_p_body(x_ref, g_ref, w_ref, *refs, dils, keeps, seq, tm):
    ng = len(dils)
    out_refs, kvt_refs, ybuf_ref = refs[:3 * ng], refs[3 * ng:4 * ng], refs[-1]
    e = out_refs[0].shape[-1]
    j = pl.program_id(1)
    h = _rms(x_ref[0], g_ref[0:1]).astype(BF16)
    for g, dil in enumerate(dils):
        first, _, rows = _kept_tiles(seq, keeps[g], tm)
        for c in range(3):
            idx = g * 3 + c
            y = _dot(h, w_ref[:, idx * e:(idx + 1) * e])
            if c == 0:
                y = y * (HEAD_DIM ** -0.5)
            else:
                def keep_window(y=y, g=g, c=c, rows=rows):
                    kvt_refs[g][0, (c - 1) * e:c * e, :] = y[tm - rows:, :].T

                if first == 0:
                    keep_window()
                else:
                    pl.when(j >= first)(keep_window)
            if dil == 1:
                out_refs[idx][0, 0] = y.astype(BF16)
            else:
                slot = idx % 2
                for cc in range(e // LANES):
                    ybuf_ref[slot, cc] = y[:, cc * LANES:(cc + 1) * LANES]
                for r in range(dil):
                    out_refs[idx][0, r] = jnp.concatenate(
                        [ybuf_ref[slot, cc, pl.ds(r, tm // dil, stride=dil), :] for cc in range(e // LANES)],
                        axis=1).astype(BF16)


def _qkv_p(x, gains, w, first_group, dils, keeps, e, tm):
    b, s, d = x.shape
    width = len(dils) * 3 * e
    assert (first_group * 3 * e) % width == 0
    w_spec = pl.BlockSpec((d, width), lambda i, j: (0, first_group * 3 * e // width), pipeline_mode=pl.Buffered(1))
    out_specs, out_shape = [], []
    for dil in dils:
        for _ in range(3):
            out_specs.append(pl.BlockSpec((1, dil, tm // dil, e), lambda i, j: (i, 0, j, 0)))
            out_shape.append(jax.ShapeDtypeStruct((b, dil, s // dil, e), BF16))
    for keep in keeps:
        first, _, rows = _kept_tiles(s, keep, tm)
        out_specs.append(pl.BlockSpec((1, 2 * e, rows), lambda i, j, first=first: (i, 0, jnp.maximum(j - first, 0))))
        out_shape.append(jax.ShapeDtypeStruct((b, 2 * e, keep), F32))
    return pl.pallas_call(
        functools.partial(_qkv_p_body, dils=dils, keeps=tuple(keeps), seq=s, tm=tm),
        grid=(b, s // tm),
        in_specs=[pl.BlockSpec((1, tm, d), lambda i, j: (i, j, 0)), _resident(gains.shape), w_spec],
        out_specs=out_specs,
        out_shape=out_shape,
        scratch_shapes=[pltpu.VMEM((2, e // LANES, tm, LANES), F32)],
        compiler_params=_params("parallel", "arbitrary"),
        name="qkv_prompt",
    )(x, gains, w)


def _qkv_s_body(x_ref, g_ref, w_ref, q_ref, kvt_ref, *, n_groups):
    e = w_ref.shape[1] // (3 * n_groups)
    h = _rms(x_ref[...], g_ref[0:1]).astype(BF16)
    for g in range(n_groups):
        q_ref[:, g * e:(g + 1) * e] = _dot(h, w_ref[:, 3 * g * e:(3 * g + 1) * e]) * (HEAD_DIM ** -0.5)
        for c in (1, 2):
            kv = _dot(h, w_ref[:, (3 * g + c) * e:(3 * g + c + 1) * e])
            kvt_ref[(2 * g + c - 1) * e:(2 * g + c) * e, :] = kv.T


def _qkv_s(x, gains, w, n_groups):
    n = x.shape[0]
    e = w.shape[1] // (3 * n_groups)
    return pl.pallas_call(
        functools.partial(_qkv_s_body, n_groups=n_groups),
        out_shape=[jax.ShapeDtypeStruct((n, n_groups * e), F32), jax.ShapeDtypeStruct((n_groups * 2 * e, n), F32)],
        compiler_params=pltpu.CompilerParams(vmem_limit_bytes=VMEM_LIMIT),
        name="qkv_sample",
    )(x, gains, w)


def _attn_p_body(*refs, has_prev, n_heads):
    if has_prev:
        q_ref, kp_ref, kc_ref, vp_ref, vc_ref, bias_ref, ones_ref, o_ref, m_ref, l_ref = refs
        first = (pl.program_id(2) == 0).astype(jnp.int32)
        kv_refs = ((kp_ref, vp_ref), (kc_ref, vc_ref))
    else:
        q_ref, kc_ref, vc_ref, bias_ref, ones_ref, o_ref, m_ref, l_ref = refs
        first = 0
        kv_refs = ((kc_ref, vc_ref),)
    per = LANES // HEAD_DIM
    n_grp = n_heads // per
    lane = lax.broadcasted_iota(jnp.int32, (QB, LANES), 1)
    lane_head = lax.broadcasted_iota(jnp.int32, (1, LANES), 1) // HEAD_DIM

    def scores(grp):
        cols = slice(grp * LANES, (grp + 1) * LANES)
        q = q_ref[0, 0, :, cols]
        qq = jnp.concatenate([q * jnp.where(lane_head == hh, 1.0, 0.0).astype(BF16) for hh in range(per)], axis=0)
        k = jnp.concatenate([k_ref[0, 0, :, cols] for k_ref, _ in kv_refs], axis=0)
        return _dot_nt(qq, k) + bias_ref[first, grp]

    m_all = jnp.zeros((QB, LANES), F32)
    l_all = jnp.ones((QB, LANES), F32)
    queue = [scores(g) for g in range(min(ATTN_AHEAD, n_grp))]
    for grp in range(n_grp):
        s = queue.pop(0)
        if grp + ATTN_AHEAD < n_grp:
            queue.append(scores(grp + ATTN_AHEAD))
        cols = slice(grp * LANES, (grp + 1) * LANES)
        m = jnp.max(s, axis=-1, keepdims=True)
        e = jnp.exp(s - m).astype(BF16)
        v1 = jnp.concatenate([jnp.concatenate([v_ref[0, 0, :, cols], ones_ref[...]], axis=1) for _, v_ref in kv_refs],
                             axis=0)
        ol = _dot(e, v1)
        o, l = ol[:, :LANES], ol[:, LANES:]
        out = o[0:QB]
        for hh in range(per):
            rows = slice(hh * QB, (hh + 1) * QB)
            if hh > 0:
                out = jnp.where(lane_head == hh, o[rows], out)
            m_all = jnp.where(lane == grp * per + hh, m[rows], m_all)
            l_all = jnp.where(lane == grp * per + hh, l[rows], l_all)
        o_ref[0, 0, :, cols] = out
    m_ref[0, 0] = m_all
    l_ref[0, 0] = l_all


def _attn_p(q, k, v, bias, g, n_heads):
    batch, dil, sub, e = q.shape
    nb = sub // QB
    has_prev = nb > 1
    assert bias.shape[0] == (2 if has_prev else 1)
    ones = jnp.ones((QB, LANES), BF16)
    blk = (1, 1, QB, e)
    cur = lambda b, r, n: (b, r, n, 0)
    prev = lambda b, r, n: (b, r, jnp.maximum(n - 1, 0), 0)
    if has_prev:
        in_specs = [pl.BlockSpec(blk, cur), pl.BlockSpec(blk, prev), pl.BlockSpec(blk, cur),
                    pl.BlockSpec(blk, prev), pl.BlockSpec(blk, cur)]
        args = (q, k, k, v, v)
    else:
        in_specs = [pl.BlockSpec(blk, cur)] * 3
        args = (q, k, v)
    return pl.pallas_call(
        functools.partial(_attn_p_body, has_prev=has_prev, n_heads=n_heads),
        grid=(batch, dil, nb),
        in_specs=in_specs + [_resident(bias.shape), _resident(ones.shape)],
        out_specs=[pl.BlockSpec(blk, cur)] + [pl.BlockSpec((1, 1, QB, LANES), cur)] * 2,
        out_shape=[jax.ShapeDtypeStruct((batch, dil, sub, e), F32)]
                  + [jax.ShapeDtypeStruct((batch, dil, sub, LANES), F32)] * 2,
        compiler_params=_params("parallel", "parallel", "arbitrary"),
        name=f"attn_prompt_g{g}",
    )(*args, bias, ones)


def _expand_heads(w, e_ref):
    hi = w.astype(BF16)
    lo = (w - hi.astype(F32)).astype(BF16)
    return _dot(hi, e_ref[...]) + _dot(lo, e_ref[...])


def _merge_groups(os_, ms, ls, e_ref):
    top = functools.reduce(jnp.maximum, ms)
    es = [jnp.exp(m - top) for m in ms]
    den = functools.reduce(lambda a, b: a + b, [e * l for e, l in zip(es, ls)])
    acc = None
    for o, e in zip(os_, es):
        term = _expand_heads(e / den, e_ref) * o
        acc = term if acc is None else acc + term
    return acc


def _merge_body(x_ref, *refs, dils, tm):
    ng = len(dils)
    o_refs, m_refs, l_refs = refs[:ng], refs[ng:2 * ng], refs[2 * ng:3 * ng]
    e_ref, wo_ref, g_ref, out_ref, obuf_ref, sbuf_ref = refs[3 * ng:]
    os_, ms, ls = [], [], []
    for gi, dil in enumerate(dils):
        if dil == 1:
            os_.append(o_refs[gi][0, 0])
            ms.append(m_refs[gi][0, 0])
            ls.append(l_refs[gi][0, 0])
        else:
            n_cc = obuf_ref.shape[1]
            for r in range(dil):
                rows = pl.ds(r, tm // dil, stride=dil)
                for cc in range(n_cc):
                    obuf_ref[gi, cc, rows, :] = o_refs[gi][0, r, :, cc * LANES:(cc + 1) * LANES]
                sbuf_ref[0, gi, rows, :] = m_refs[gi][0, r]
                sbuf_ref[1, gi, rows, :] = l_refs[gi][0, r]
            os_.append(jnp.concatenate([obuf_ref[gi, cc] for cc in range(n_cc)], axis=1))
            ms.append(sbuf_ref[0, gi])
            ls.append(sbuf_ref[1, gi])
    o = _merge_groups(os_, ms, ls, e_ref)
    y = _dot(o.astype(BF16), wo_ref[...])
    out_ref[0] = x_ref[0] + _rms(y, g_ref[1:2])


def _merge(x, os_, ms, ls, expand, w_o, gains, tm):
    b, s, d = x.shape
    dils = tuple(o.shape[1] for o in os_)
    e = os_[0].shape[-1]
    grp = lambda i, j: (i, 0, j, 0)
    return pl.pallas_call(
        functools.partial(_merge_body, dils=dils, tm=tm),
        grid=(b, s // tm),
        in_specs=[pl.BlockSpec((1, tm, d), lambda i, j: (i, j, 0))]
                 + [pl.BlockSpec((1, dil, tm // dil, e), grp) for dil in dils]
                 + [pl.BlockSpec((1, dil, tm // dil, LANES), grp) for dil in dils] * 2
                 + [_resident(expand.shape), _resident(w_o.shape), _resident(gains.shape)],
        out_specs=pl.BlockSpec((1, tm, d), lambda i, j: (i, j, 0)),
        out_shape=jax.ShapeDtypeStruct((b, s, d), F32),
        scratch_shapes=[pltpu.VMEM((len(dils), e // LANES, tm, LANES), F32),
                        pltpu.VMEM((2, len(dils), tm, LANES), F32)],
        compiler_params=_params("parallel", "parallel"),
        name="attn_merge",
    )(x, *os_, *ms, *ls, expand, w_o, gains)


class _WindowAttnChunk:
    def __init__(self, b, q, knew, vnew, c_ref, bias_c, bias_n, cout_ref, *, dil, n_steps, nt, hc):
        self.b, self.q, self.knew, self.vnew, self.c_ref, self.cout_ref = b, q.astype(BF16), knew, vnew, c_ref, cout_ref
        self.bias_c, self.bias_n, self.dil, self.n_steps, self.nt = bias_c, bias_n, dil, n_steps, nt
        self.heads = [slice(hh * HEAD_DIM, (hh + 1) * HEAD_DIM) for hh in range(hc)]
        self.lane = lax.broadcasted_iota(jnp.int32, (nt, LANES), 1)

    def logits(self):
        nt, dil = self.nt, self.dil
        dn = lax.broadcasted_iota(jnp.int32, (nt, LANES), 0) - lax.rem(self.lane, nt)
        valid_n = (self.lane // nt == self.b) & (dn >= 0) & (lax.rem(dn, dil) == 0) & (dn <= dil * self.n_steps)
        self.s_c = [_dot(self.q[:, sl], self.c_ref[0, 0, hh].astype(BF16)) + self.bias_c[hh]
                    for hh, sl in enumerate(self.heads)]
        self.s_n = [jnp.where(valid_n, _dot(self.q[:, sl], self.knew[sl, :].astype(BF16)) + self.bias_n[hh], NEG_INF)
                    for hh, sl in enumerate(self.heads)]

    def softmax(self):
        ms = [jnp.maximum(jnp.max(a, axis=-1, keepdims=True), jnp.max(c, axis=-1, keepdims=True))
              for a, c in zip(self.s_c, self.s_n)]
        e_c = [jnp.exp(a - m) for a, m in zip(self.s_c, ms)]
        e_n = [jnp.exp(c - m) for c, m in zip(self.s_n, ms)]
        ls = [jnp.sum(a, axis=-1, keepdims=True) + jnp.sum(c, axis=-1, keepdims=True) for a, c in zip(e_c, e_n)]
        self.p_c = [(e / l).astype(BF16) for e, l in zip(e_c, ls)]
        self.p_n = [(e / l).astype(BF16) for e, l in zip(e_n, ls)]
        self.lse = jnp.zeros((self.nt, LANES), F32)
        for hh, (m, l) in enumerate(zip(ms, ls)):
            self.lse = jnp.where(self.lane == hh, m + jnp.log(l), self.lse)

    def output(self):
        os_ = [_dot_nt(self.c_ref[0, 1, hh].astype(BF16), self.p_c[hh])
               + _dot_nt(self.vnew[sl, :].astype(BF16), self.p_n[hh]) for hh, sl in enumerate(self.heads)]
        return jnp.concatenate(os_, axis=0), self.lse

    def shift(self):
        n, nt = self.c_ref.shape[-1], self.nt
        tail = lax.broadcasted_iota(jnp.int32, (HEAD_DIM, LANES), 1) >= LANES - nt
        lanes_to_tail = LANES - nt - self.b * nt
        for hh, sl in enumerate(self.heads):
            for kv, new in enumerate((self.knew, self.vnew)):
                rolled = pltpu.roll(self.c_ref[0, kv, hh], n - nt, axis=1)
                placed = pltpu.roll(new[sl, :], lanes_to_tail, axis=1)
                if n > LANES:
                    self.cout_ref[0, kv, hh, :, 0:n - LANES] = rolled[:, 0:n - LANES]
                self.cout_ref[0, kv, hh, :, n - LANES:n] = jnp.where(tail, placed, rolled[:, n - LANES:n])


def _ffn_attn_s_body(x_ref, g_ref, wup_ref, wdn_ref, q_ref, kvt_ref, *refs, groups, nt, hc, f_chunk, n_chunks):
    ng = len(groups)
    c_refs, bc_refs, bn_refs = refs[:ng], refs[ng:2 * ng], refs[2 * ng:3 * ng]
    y_ref = refs[3 * ng]
    o_refs, lse_refs, cout_refs = (refs[3 * ng + 1 + k * ng:3 * ng + 1 + (k + 1) * ng] for k in range(3))
    h_ref, acc_ref = refs[-2:]
    i, j = pl.program_id(0), pl.program_id(1)

    @pl.when(j == 0)
    def _():
        h_ref[...] = _rms(x_ref[...], g_ref[2:3]).astype(BF16)
        acc_ref[...] = jnp.zeros_like(acc_ref)

    cw = hc * HEAD_DIM
    e = cw * n_chunks
    attn = []
    for g, (dil, n_steps) in enumerate(groups):
        knew = kvt_ref[pl.ds(pl.multiple_of((2 * g) * e + j * cw, cw), cw), :]
        vnew = kvt_ref[pl.ds(pl.multiple_of((2 * g + 1) * e + j * cw, cw), cw), :]
        bias_c = [bc_refs[g][j * hc + hh] for hh in range(hc)]
        bias_n = [bn_refs[g][j * hc + hh] for hh in range(hc)]
        attn.append(_WindowAttnChunk(i, q_ref[i, g, j], knew, vnew, c_refs[g], bias_c, bias_n, cout_refs[g],
                                     dil=dil, n_steps=n_steps, nt=nt, hc=hc))
    hidden = pl.ds(pl.multiple_of(j * f_chunk, f_chunk), f_chunk)
    up = jnp.maximum(_dot(h_ref[...], wup_ref[0, :, hidden]), 0.0)
    for a in attn:
        a.logits()
    acc_ref[...] += _dot((up * up).astype(BF16), wdn_ref[0, hidden, :])
    for a in attn:
        a.softmax()
    for g, a in enumerate(attn):
        o_refs[g][0, j], lse_refs[g][0, j] = a.output()
    for a in attn:
        a.shift()

    @pl.when(j == n_chunks - 1)
    def _():
        y_ref[...] = x_ref[...] + _rms(acc_ref[...], g_ref[3:4])


def _ffn_attn_s(x, gains, w_up, w_down, layer, q, kvt_new, caches, bias_cs, bias_ns, groups, nt, tm):
    n, d = x.shape
    n_chunks, f_chunk = q.shape[2], w_up.shape[2] // q.shape[2]
    b, _, n_heads, _, _ = caches[0].shape
    hc = n_heads // n_chunks
    cw = hc * HEAD_DIM
    assert n // tm == b and kvt_new.shape[1] == LANES == b * nt and hc * n_chunks == n_heads
    assert all(c.shape[-1] == dil * n_steps for c, (dil, n_steps) in zip(caches, groups))
    row = lambda i, j: (i, 0)
    win = lambda i, j: (i, 0, j, 0, 0)
    per_row = lambda i, j: (i, 0, 0, 0)
    win_specs = [pl.BlockSpec((1, 2, hc, HEAD_DIM, c.shape[-1]), win) for c in caches]
    ng = len(caches)
    outs = pl.pallas_call(
        functools.partial(_ffn_attn_s_body, groups=groups, nt=nt, hc=hc, f_chunk=f_chunk, n_chunks=n_chunks),
        grid=(b, n_chunks),
        in_specs=[pl.BlockSpec((tm, d), row), _resident(gains.shape), _layer_weights(w_up, layer),
                  _layer_weights(w_down, layer), _resident(q.shape), _resident(kvt_new.shape)]
                 + win_specs + [_resident(a.shape) for a in bias_cs] + [_resident(a.shape) for a in bias_ns],
        out_specs=[pl.BlockSpec((tm, d), row)]
                  + [pl.BlockSpec((1, n_chunks, cw, nt), per_row)] * ng
                  + [pl.BlockSpec((1, n_chunks, nt, LANES), per_row)] * ng + win_specs,
        out_shape=[jax.ShapeDtypeStruct((n, d), F32)]
                  + [jax.ShapeDtypeStruct((b, n_chunks, cw, nt), F32)] * ng
                  + [jax.ShapeDtypeStruct((b, n_chunks, nt, LANES), F32)] * ng
                  + [jax.ShapeDtypeStruct(c.shape, F32) for c in caches],
        scratch_shapes=[pltpu.VMEM((tm, d), BF16), pltpu.VMEM((tm, d), F32)],
        compiler_params=_params("parallel", "arbitrary"),
        name="ffn_with_sample_attn",
    )(x, gains, w_up, w_down, q, kvt_new, *caches, *bias_cs, *bias_ns)
    return outs[0], outs[1:1 + ng], outs[1 + ng:1 + 2 * ng], outs[1 + 2 * ng:]


def _conv_tail(y, x, lng_ref, lnb_ref, w2_ref, b2_ref, g_ref):
    mu = jnp.mean(y, axis=-1, keepdims=True)
    yc = y - mu
    var = jnp.mean(yc * yc, axis=-1, keepdims=True)
    yn = yc * lax.rsqrt(var + EPS) * lng_ref[...] + lnb_ref[...]
    z = _dot((yn * jax.nn.sigmoid(yn)).astype(BF16), w2_ref[...]) + b2_ref[...]
    return x + _rms(z, g_ref[1:2])


def _glu(x, g_ref, w1_ref, b1_ref):
    d = x.shape[1]
    a = _dot(_rms(x, g_ref[0:1]).astype(BF16), w1_ref[...]) + b1_ref[...]
    return a[:, :d] * jax.nn.sigmoid(a[:, d:])


def _conv_p_body(x_ref, g_ref, w1_ref, b1_ref, wdw_ref, bdw_ref, lng_ref, lnb_ref, w2_ref, b2_ref,
                 o_ref, ul_ref, ush_ref, y_ref, *, ts):
    j = pl.program_id(1)
    n_cb = y_ref.shape[0]
    rows_ext = CARRY_ROWS + ts

    @pl.when(j == 0)
    def _():
        ush_ref[0, :, 0:CARRY_ROWS, :] = jnp.zeros((n_cb, CARRY_ROWS, LANES), F32)

    x = x_ref[0]
    u = _glu(x, g_ref, w1_ref, b1_ref)
    for cb in range(n_cb):
        ush_ref[0, cb, CARRY_ROWS:, :] = u[:, cb * LANES:(cb + 1) * LANES]
    n_sh = rows_ext - SUBLANES
    lead = CARRY_ROWS - (CONV_WIDTH - 1)

    def column_block(cb, carry):
        for m in range(1, SUBLANES):
            for r0 in range(0, n_sh, CONV_COPY_ROWS):
                r1 = min(r0 + CONV_COPY_ROWS, n_sh)
                ush_ref[m, cb, r0:r1, :] = ush_ref[0, cb, r0 + m:r1 + m, :]

        def rows(c, carry):
            r0 = pl.multiple_of(c * CONV_CHUNK_ROWS, CONV_CHUNK_ROWS)
            acc = jnp.broadcast_to(bdw_ref[cb], (CONV_CHUNK_ROWS, LANES))
            for k in range(CONV_WIDTH):
                a, m = divmod(lead + k, SUBLANES)
                acc = acc + wdw_ref[cb, k:k + 1, :] * ush_ref[m, cb, pl.ds(r0 + a * SUBLANES, CONV_CHUNK_ROWS), :]
            y_ref[cb, pl.ds(r0, CONV_CHUNK_ROWS), :] = acc
            return carry

        return lax.fori_loop(0, ts // CONV_CHUNK_ROWS, rows, carry)

    lax.fori_loop(0, n_cb, column_block, 0)
    y = jnp.concatenate([y_ref[cb] for cb in range(n_cb)], axis=1)
    o_ref[0] = _conv_tail(y, x, lng_ref, lnb_ref, w2_ref, b2_ref, g_ref)
    last = [ush_ref[0, cb, ts:ts + CARRY_ROWS, :] for cb in range(n_cb)]
    for cb in range(n_cb):
        ush_ref[0, cb, 0:CARRY_ROWS, :] = last[cb]
    ul_ref[0] = jnp.concatenate(last, axis=1)


def _conv_p(x, gains, w1, b1, wdw, bdw, lng, lnb, w2, b2, ts):
    b, s, d = x.shape
    n_cb = d // LANES
    wdw_cb = wdw.reshape(wdw.shape[0], n_cb, LANES).transpose(1, 0, 2)
    bdw_cb = bdw.reshape(n_cb, 1, LANES)
    consts = (gains, w1, b1, wdw_cb, bdw_cb, lng, lnb, w2, b2)
    return pl.pallas_call(
        functools.partial(_conv_p_body, ts=ts),
        grid=(b, s // ts),
        in_specs=[pl.BlockSpec((1, ts, d), lambda i, j: (i, j, 0))] + [_resident(a.shape) for a in consts],
        out_specs=[pl.BlockSpec((1, ts, d), lambda i, j: (i, j, 0)),
                   pl.BlockSpec((1, CARRY_ROWS, d), lambda i, j: (i, 0, 0))],
        out_shape=[jax.ShapeDtypeStruct((b, s, d), F32), jax.ShapeDtypeStruct((b, CARRY_ROWS, d), F32)],
        scratch_shapes=[pltpu.VMEM((SUBLANES, n_cb, CARRY_ROWS + ts, LANES), F32), pltpu.VMEM((n_cb, ts, LANES), F32)],
        compiler_params=_params("parallel", "arbitrary"),
        name="conv_prompt",
    )(x, *consts)


def _conv_s_body(x_ref, st_ref, g_ref, w1_ref, b1_ref, wdw_ref, bdw_ref, lng_ref, lnb_ref, w2_ref, b2_ref,
                 o_ref, ns_ref, *, nb, nt):
    x = x_ref[...]
    u = _glu(x, g_ref, w1_ref, b1_ref)
    n_prev = CONV_WIDTH - 1
    ext = [st_ref[i * nb:(i + 1) * nb, :] for i in range(n_prev)]
    ext += [u[t * nb:(t + 1) * nb, :] for t in range(nt)]
    ys = []
    for t in range(nt):
        y = None
        for k in range(CONV_WIDTH):
            term = wdw_ref[k:k + 1, :] * ext[t + k]
            y = term if y is None else y + term
        ys.append(y + bdw_ref[...])
    o_ref[...] = _conv_tail(jnp.concatenate(ys, axis=0), x, lng_ref, lnb_ref, w2_ref, b2_ref, g_ref)
    for i in range(n_prev):
        ns_ref[i * nb:(i + 1) * nb, :] = ext[nt + i]


def _conv_s(x, state, gains, w1, b1, wdw, bdw, lng, lnb, w2, b2, nb, nt):
    return pl.pallas_call(
        functools.partial(_conv_s_body, nb=nb, nt=nt),
        out_shape=[jax.ShapeDtypeStruct(x.shape, F32), jax.ShapeDtypeStruct(state.shape, F32)],
        compiler_params=pltpu.CompilerParams(vmem_limit_bytes=VMEM_LIMIT),
        name="conv_sample",
    )(x, state, gains, w1, b1, wdw, bdw, lng, lnb, w2, b2)


def _t5_bucket(dist):
    max_exact = N_BUCKETS // 2
    df = jnp.maximum(dist, 1).astype(F32)
    large = max_exact + (jnp.log(df / max_exact) / math.log(MAX_DISTANCE / max_exact)
                         * (N_BUCKETS - max_exact)).astype(jnp.int32)
    large = jnp.minimum(large, N_BUCKETS - 1)
    return jnp.where(dist < max_exact, dist, large)


def _bias_by_step(rel_bias, g, dil, n_steps, n_heads):
    tab = rel_bias[:, g * n_heads:(g + 1) * n_heads]
    bucket = _t5_bucket(dil * jnp.arange(n_steps + 1, dtype=jnp.int32))
    onehot = (bucket[:, None] == jnp.arange(N_BUCKETS, dtype=jnp.int32)[None, :]).astype(F32)
    return jnp.dot(onehot, tab, precision=lax.Precision.HIGHEST)


def _band_bias(by_step, n_steps, nk):
    off = nk - QB
    period = 2 * nk
    heads = by_step.shape[1]
    base = jnp.concatenate([by_step[::-1].T, jnp.full((heads, period - n_steps - 1), NEG_INF, F32)], axis=1)
    w = jnp.roll(base, off - n_steps, axis=1)
    r = period - 1
    flat = jnp.tile(w, (1, -(-(QB * r) // period)))[:, :QB * r]
    band = flat.reshape(heads, QB, r)[:, :, :nk]
    if off > 0:
        hidden = (np.arange(nk) < off)[None, None, :]
        band = jnp.stack([band, jnp.where(hidden, NEG_INF, band)])
    else:
        band = band[None]
    per = LANES // HEAD_DIM
    return band.reshape(band.shape[0], heads // per, per * QB, nk)


def _window_bias(by_step, dil, n_steps, nt):
    base = jnp.repeat(by_step[n_steps:0:-1].T, dil, axis=1)
    bias = jnp.stack([jnp.roll(base, t, axis=1) for t in range(nt)], axis=1)
    n = dil * n_steps
    delta = n + np.arange(nt)[:, None] - np.arange(n)[None, :]
    attended = (delta % dil == 0) & (delta <= dil * n_steps)
    return jnp.where(attended[None], bias, NEG_INF)


def _new_key_bias(by_step, dil, n_steps, nt):
    zero = jnp.zeros_like(by_step[0])
    rows = []
    for t in range(nt):
        cols = []
        for tp in range(nt):
            s, rem = divmod(t - tp, dil)
            cols.append(by_step[s] if (t >= tp and rem == 0 and s <= n_steps) else zero)
        rows.append(jnp.stack(cols, axis=1))
    return jnp.tile(jnp.stack(rows, axis=1), (1, 1, LANES // nt))


def _time_major(a):
    b, t, d = a.shape
    return a.transpose(1, 0, 2).reshape(t * b, d)


def _batch_major(a, b):
    return a.reshape(-1, b, a.shape[1]).transpose(1, 0, 2)


def _kv_window_layout(c):
    return c.transpose(0, 2, 3, 4, 1)


def _kv_window_unlayout(c):
    return c.transpose(0, 4, 1, 2, 3)


def kernel(x_prompt, x_sample, state_pool, cache_kv_g0, cache_kv_g1, cache_kv_g2, state_conv, norm_gains, rel_bias, pool_w, pool_scale, attn_w_qkv, attn_w_o, conv_w_pw1, conv_b_pw1, conv_w_dw, conv_b_dw, conv_ln_g, conv_ln_b, conv_w_pw2, conv_b_pw2, ffn_w_up, ffn_w_down):
    batch, seq, d = x_prompt.shape
    dec_b, dec_t, _ = x_sample.shape
    depth = norm_gains.shape[0]
    n_heads = attn_w_o.shape[1] // HEAD_DIM
    e_dim = n_heads * HEAD_DIM
    n_groups = len(ATTN_GROUPS)
    dils = tuple(dil for _, dil in ATTN_GROUPS)
    caches_in = (cache_kv_g0, cache_kv_g1, cache_kv_g2)
    n_tok = batch * seq
    n_dec = dec_b * dec_t

    xp = x_prompt
    xs = _time_major(x_sample)

    expand = jnp.asarray(np.kron(np.eye(LANES, n_heads), np.ones((1, HEAD_DIM))), BF16)

    w_up, w_down = ffn_w_up.astype(BF16), ffn_w_down.astype(BF16)

    pool_p, pool_s, conv_p, conv_s = [], [], [], []
    kv_p = [[] for _ in ATTN_GROUPS]
    kv_s = [[] for _ in ATTN_GROUPS]
    for i in range(depth):
        kind, j = i % 3, i // 3
        gains = norm_gains[i]
        if kind == 0:
            pw, ps = pool_w[j].astype(BF16), pool_scale[j][None, :]
            xp, h_last = _pool_p(xp, gains, pw, ps, w_up, w_down, i, ts=512)
            pool_p.append(h_last[:, POOL_CARRY - POOL_STATE:])
            xs, new_state = _pool_s(xs, _time_major(state_pool[j]), gains, pw, ps,
                                    nb=dec_b, nt=dec_t, pos0=PAST_LEN)
            pool_s.append(_batch_major(new_state, dec_b))
            xs = _ffn(xs, gains, w_up, w_down, i, tm=n_dec)
            continue
        elif kind == 1:
            w_qkv = attn_w_qkv[j].astype(BF16)
            w_o = attn_w_o[j].astype(BF16)
            by_step = [_bias_by_step(rel_bias, g, dil, win // dil, n_heads)
                       for g, (win, dil) in enumerate(ATTN_GROUPS)]
            keeps = [min(win, seq) for win, _ in ATTN_GROUPS]
            qkv, kvts = [], []
            for lo, hi in QKV_GROUP_SPLIT:
                outs = _qkv_p(xp, gains, w_qkv, lo, dils[lo:hi], keeps[lo:hi], e_dim, tm=512)
                qkv += outs[:3 * (hi - lo)]
                kvts += outs[3 * (hi - lo):]
            os_, ms, ls = [], [], []
            for g, (win, dil) in enumerate(ATTN_GROUPS):
                n_steps = win // dil
                nk = 2 * QB if seq // dil > QB else QB
                bias = _band_bias(by_step[g], n_steps, nk)
                o, m, l = _attn_p(qkv[3 * g], qkv[3 * g + 1], qkv[3 * g + 2], bias, g, n_heads)
                os_.append(o)
                ms.append(m)
                ls.append(l)
                kv_p[g].append(_kv_window_unlayout(kvts[g].reshape(batch, 2, n_heads, HEAD_DIM, keeps[g])))
            xp = _merge(xp, os_, ms, ls, expand, w_o, gains, tm=512)
            xs_b = _batch_major(xs, dec_b)
            xs_flat = xs_b.reshape(n_dec, d)
            n_chunks = ffn_w_up.shape[2] // FFN_CHUNK
            hc = n_heads // n_chunks
            q_s, kvt_new = _qkv_s(xs_flat, gains, w_qkv, n_groups)
            q_s = q_s.reshape(dec_b, dec_t, n_groups, n_chunks, hc * HEAD_DIM)
            q_s = q_s.transpose(0, 2, 3, 1, 4)
            groups = tuple((dil, win // dil) for win, dil in ATTN_GROUPS)
            caches = [_kv_window_layout(c[j]) for c in caches_in]
            bias_cs = [_window_bias(by_step[g], dil, n_steps, dec_t) for g, (dil, n_steps) in enumerate(groups)]
            bias_ns = [_new_key_bias(by_step[g], dil, n_steps, dec_t) for g, (dil, n_steps) in enumerate(groups)]
            xp, os_, lses, new_caches = _ffn_attn_s(xp.reshape(n_tok, d), gains, w_up, w_down, i, q_s, kvt_new,
                                                    caches, bias_cs, bias_ns, groups, dec_t, tm=n_tok // dec_b)
            xp = xp.reshape(batch, seq, d)
            ms, ls = [], []
            for g in range(n_groups):
                kv_s[g].append(_kv_window_unlayout(new_caches[g]))
                lse = lses[g][..., :hc].transpose(0, 2, 1, 3).reshape(1, 1, n_dec, n_heads)
                ms.append(jnp.pad(lse, ((0, 0), (0, 0), (0, 0), (0, LANES - n_heads))))
                ls.append(jnp.ones((1, 1, n_dec, LANES), F32))
            os_ = [o.transpose(0, 3, 1, 2).reshape(1, 1, n_dec, e_dim) for o in os_]
            xs_flat = _merge(xs_flat[None], os_, ms, ls, expand, w_o, gains, tm=n_dec)[0]
            xs = _time_major(xs_flat.reshape(dec_b, dec_t, d))
            xs = _ffn(xs, gains, w_up, w_down, i, tm=n_dec)
            continue
        else:
            cw = (conv_w_pw1[j].astype(BF16), conv_b_pw1[j][None, :], conv_w_dw[j], conv_b_dw[j][None, :],
                  conv_ln_g[j][None, :], conv_ln_b[j][None, :], conv_w_pw2[j].astype(BF16), conv_b_pw2[j][None, :])
            xp, u_last = _conv_p(xp, gains, *cw, ts=512)
            conv_p.append(u_last[:, CARRY_ROWS - (CONV_WIDTH - 1):])
            xs, new_state = _conv_s(xs, _time_major(state_conv[j]), gains, *cw, nb=dec_b, nt=dec_t)
            conv_s.append(_batch_major(new_state, dec_b))
        xp = _ffn(xp.reshape(n_tok, d), gains, w_up, w_down, i, tm=512).reshape(batch, seq, d)
        xs = _ffn(xs, gains, w_up, w_down, i, tm=n_dec)
    return (xp, _batch_major(xs, dec_b),
            jnp.stack(pool_p), jnp.stack(pool_s),
            jnp.stack(kv_p[0]), jnp.stack(kv_s[0]),
            jnp.stack(kv_p[1]), jnp.stack(kv_s[1]),
            jnp.stack(kv_p[2]), jnp.stack(kv_s[2]),
            jnp.stack(conv_p), jnp.stack(conv_s))
```

```python
import functools
import math

import jax
import jax.numpy as jnp
import numpy as np
from jax import lax
from jax.experimental import pallas as pl
from jax.experimental.pallas import tpu as pltpu

F32 = jnp.float32
BF16 = jnp.bfloat16

EPS = 1e-6
NEG_INF = -1e30
POOL_WINDOWS = (2, 4, 8, 16)
POOL_STATE = max(POOL_WINDOWS) - 1
PAST_LEN = 8192
ATTN_GROUPS = ((128, 1), (512, 4), (2048, 16))
HEAD_DIM = 64
QB = 128
ATTN_BLOCKS_PER_STEP = 4
ATTN_AHEAD = 3
N_BUCKETS = 32
MAX_DISTANCE = 2048
CONV_WIDTH = 31
CARRY_ROWS = 32
POOL_CARRY = 16
LANES = 128
SUBLANES = 8
CONV_CHUNK_ROWS = 128
CONV_COPY_ROWS = 128

VMEM_LIMIT = 60 * 1024 * 1024
QKV_GROUP_SPLIT = ((0, 2), (2, 3))
MLP_CHUNK = 1024
FFN_CHUNK = 1024

NT_DIMS = (((1,), (1,)), ((), ()))


def _params(*sem):
    return pltpu.CompilerParams(dimension_semantics=sem, vmem_limit_bytes=VMEM_LIMIT)


def _resident(shape):
    nd = len(shape)
    return pl.BlockSpec(shape, lambda *_: (0,) * nd, pipeline_mode=pl.Buffered(1))


def _rms(x, g):
    return x * lax.rsqrt(jnp.mean(x * x, axis=-1, keepdims=True) + EPS) * g


def _dot(a, b):
    return jnp.dot(a, b, preferred_element_type=F32)


def _dot_nt(a, b):
    return lax.dot_general(a, b, NT_DIMS, preferred_element_type=F32)


def _mlp(x, g_ref, wup_ref, wdn_ref):
    h = _rms(x, g_ref[2:3]).astype(BF16)
    acc = jnp.zeros_like(x)
    for c in range(wup_ref.shape[2] // MLP_CHUNK):
        cols = slice(c * MLP_CHUNK, (c + 1) * MLP_CHUNK)
        a = jnp.maximum(_dot(h, wup_ref[0, :, cols]), 0.0)
        acc = acc + _dot((a * a).astype(BF16), wdn_ref[0, cols, :])
    return x + _rms(acc, g_ref[3:4])


def _ffn_body(x_ref, g_ref, wup_ref, wdn_ref, o_ref):
    o_ref[...] = _mlp(x_ref[...], g_ref, wup_ref, wdn_ref)


def _layer_weights(w, layer):
    nd = w.ndim - 1
    return pl.BlockSpec((1,) + w.shape[1:], lambda *_: (layer,) + (0,) * nd, pipeline_mode=pl.Buffered(1))


def _ffn(x, gains, w_up, w_down, layer, tm):
    n, d = x.shape
    return pl.pallas_call(
        _ffn_body,
        grid=(n // tm,),
        in_specs=[pl.BlockSpec((tm, d), lambda i: (i, 0)), _resident(gains.shape),
                  _layer_weights(w_up, layer), _layer_weights(w_down, layer)],
        out_specs=pl.BlockSpec((tm, d), lambda i: (i, 0)),
        out_shape=jax.ShapeDtypeStruct((n, d), F32),
        compiler_params=_params("parallel"),
        name="ffn",
    )(x, gains, w_up, w_down)


def _pool_p_body(x_ref, g_ref, pw_ref, ps_ref, wup_ref, wdn_ref, o_ref, hl_ref, carry_ref, *, ts):
    j = pl.program_id(1)

    @pl.when(j == 0)
    def _():
        carry_ref[...] = jnp.zeros_like(carry_ref)

    x = x_ref[0]
    h = _rms(x, g_ref[0:1])
    ext = jnp.concatenate([carry_ref[...], h], axis=0)
    sums = []
    s = ext
    for k in (1, 2, 4, 8):
        s = s + pltpu.roll(s, k, axis=0)
        sums.append(s)
    pos = j * ts + lax.broadcasted_iota(jnp.int32, (ts, 1), 0)
    gw = pw_ref.shape[1]
    ys = []
    for gi, (w, s) in enumerate(zip(POOL_WINDOWS, sums)):
        cols = slice(gi * gw, (gi + 1) * gw)
        inv_cnt = 1.0 / jnp.minimum(pos + 1, w).astype(F32)
        pooled = s[POOL_CARRY:, cols] * inv_cnt - h[:, cols]
        ys.append(_dot(pooled.astype(BF16), pw_ref[gi]))
    y = jnp.concatenate(ys, axis=1) * ps_ref[...]
    carry_ref[...] = h[ts - POOL_CARRY:, :]
    hl_ref[0] = h[ts - POOL_CARRY:, :]
    o_ref[0] = _mlp(x + _rms(y, g_ref[1:2]), g_ref, wup_ref, wdn_ref)


def _pool_p(x, gains, pw, ps, w_up, w_down, layer, ts):
    b, s, d = x.shape
    return pl.pallas_call(
        functools.partial(_pool_p_body, ts=ts),
        grid=(b, s // ts),
        in_specs=[pl.BlockSpec((1, ts, d), lambda i, j: (i, j, 0)), _resident(gains.shape),
                  _resident(pw.shape), _resident(ps.shape),
                  _layer_weights(w_up, layer), _layer_weights(w_down, layer)],
        out_specs=[pl.BlockSpec((1, ts, d), lambda i, j: (i, j, 0)),
                   pl.BlockSpec((1, POOL_CARRY, d), lambda i, j: (i, 0, 0))],
        out_shape=[jax.ShapeDtypeStruct((b, s, d), F32),
                   jax.ShapeDtypeStruct((b, POOL_CARRY, d), F32)],
        scratch_shapes=[pltpu.VMEM((POOL_CARRY, d), F32)],
        compiler_params=_params("parallel", "arbitrary"),
        name="pool_mlp_prompt",
    )(x, gains, pw, ps, w_up, w_down)


def _pool_s_body(x_ref, st_ref, g_ref, pw_ref, ps_ref, o_ref, ns_ref, *, nb, nt, pos0):
    x = x_ref[...]
    h = _rms(x, g_ref[0:1])
    ext = [st_ref[i * nb:(i + 1) * nb, :] for i in range(POOL_STATE)]
    ext += [h[t * nb:(t + 1) * nb, :] for t in range(nt)]
    gw = pw_ref.shape[1]
    for t in range(nt):
        ys = []
        for gi, w in enumerate(POOL_WINDOWS):
            cols = slice(gi * gw, (gi + 1) * gw)
            end = POOL_STATE + t
            acc = ext[end][:, cols]
            for i in range(1, min(w, end + 1)):
                acc = acc + ext[end - i][:, cols]
            pooled = acc / float(min(pos0 + t + 1, w)) - ext[end][:, cols]
            ys.append(_dot(pooled.astype(BF16), pw_ref[gi]))
        y = jnp.concatenate(ys, axis=1) * ps_ref[...]
        o_ref[t * nb:(t + 1) * nb, :] = x[t * nb:(t + 1) * nb, :] + _rms(y, g_ref[1:2])
    for i in range(POOL_STATE):
        ns_ref[i * nb:(i + 1) * nb, :] = ext[nt + i]


def _pool_s(x, state, gains, pw, ps, nb, nt, pos0):
    return pl.pallas_call(
        functools.partial(_pool_s_body, nb=nb, nt=nt, pos0=pos0),
        out_shape=[jax.ShapeDtypeStruct(x.shape, F32), jax.ShapeDtypeStruct(state.shape, F32)],
        compiler_params=pltpu.CompilerParams(vmem_limit_bytes=VMEM_LIMIT),
        name="pool_sample",
    )(x, state, gains, pw, ps)


def _kept_tiles(seq, keep, tm):
    rows = min(keep, tm)
    tiles = keep // rows
    return seq // tm - tiles, tiles, rows


def _qkv_p_body(x_ref, g_ref, w_ref, *refs, dils, keeps, seq, tm):
    ng = len(dils)
    out_refs, kvt_refs, ybuf_ref = refs[:3 * ng], refs[3 * ng:4 * ng], refs[-1]
    e = out_refs[0].shape[-1]
    j = pl.program_id(1)
    h = _rms(x_ref[0], g_ref[0:1]).astype(BF16)
    for g, dil in enumerate(dils):
        first, _, rows = _kept_tiles(seq, keeps[g], tm)
        for c in range(3):
            idx = g * 3 + c
            y = _dot(h, w_ref[:, idx * e:(idx + 1) * e])
            if c == 0:
                y = y * (HEAD_DIM ** -0.5)
            else:
                def keep_window(y=y, g=g, c=c, rows=rows):
                    kvt_refs[g][0, (c - 1) * e:c * e, :] = y[tm - rows:, :].T

                if first == 0:
                    keep_window()
                else:
                    pl.when(j >= first)(keep_window)
            if dil == 1:
                out_refs[idx][0, 0] = y.astype(BF16)
            else:
                slot = idx % 2
                for cc in range(e // LANES):
                    ybuf_ref[slot, cc] = y[:, cc * LANES:(cc + 1) * LANES]
                for r in range(dil):
                    out_refs[idx][0, r] = jnp.concatenate(
                        [ybuf_ref[slot, cc, pl.ds(r, tm // dil, stride=dil), :] for cc in range(e // LANES)],
                        axis=1).astype(BF16)


def _qkv_p(x, gains, w, first_group, dils, keeps, e, tm):
    b, s, d = x.shape
    width = len(dils) * 3 * e
    assert (first_group * 3 * e) % width == 0
    w_spec = pl.BlockSpec((d, width), lambda i, j: (0, first_group * 3 * e // width), pipeline_mode=pl.Buffered(1))
    out_specs, out_shape = [], []
    for dil in dils:
        for _ in range(3):
            out_specs.append(pl.BlockSpec((1, dil, tm // dil, e), lambda i, j: (i, 0, j, 0)))
            out_shape.append(jax.ShapeDtypeStruct((b, dil, s // dil, e), BF16))
    for keep in keeps:
        first, _, rows = _kept_tiles(s, keep, tm)
        out_specs.append(pl.BlockSpec((1, 2 * e, rows), lambda i, j, first=first: (i, 0, jnp.maximum(j - first, 0))))
        out_shape.append(jax.ShapeDtypeStruct((b, 2 * e, keep), F32))
    return pl.pallas_call(
        functools.partial(_qkv_p_body, dils=dils, keeps=tuple(keeps), seq=s, tm=tm),
        grid=(b, s // tm),
        in_specs=[pl.BlockSpec((1, tm, d), lambda i, j: (i, j, 0)), _resident(gains.shape), w_spec],
        out_specs=out_specs,
        out_shape=out_shape,
        scratch_shapes=[pltpu.VMEM((2, e // LANES, tm, LANES), F32)],
        compiler_params=_params("parallel", "arbitrary"),
        name="qkv_prompt",
    )(x, gains, w)


def _qkv_s_body(x_ref, g_ref, w_ref, q_ref, kvt_ref, *, n_groups):
    e = w_ref.shape[1] // (3 * n_groups)
    h = _rms(x_ref[...], g_ref[0:1]).astype(BF16)
    for g in range(n_groups):
        q_ref[:, g * e:(g + 1) * e] = _dot(h, w_ref[:, 3 * g * e:(3 * g + 1) * e]) * (HEAD_DIM ** -0.5)
        for c in (1, 2):
            kv = _dot(h, w_ref[:, (3 * g + c) * e:(3 * g + c + 1) * e])
            kvt_ref[(2 * g + c - 1) * e:(2 * g + c) * e, :] = kv.T


def _qkv_s(x, gains, w, n_groups):
    n = x.shape[0]
    e = w.shape[1] // (3 * n_groups)
    return pl.pallas_call(
        functools.partial(_qkv_s_body, n_groups=n_groups),
        out_shape=[jax.ShapeDtypeStruct((n, n_groups * e), F32), jax.ShapeDtypeStruct((n_groups * 2 * e, n), F32)],
        compiler_params=pltpu.CompilerParams(vmem_limit_bytes=VMEM_LIMIT),
        name="qkv_sample",
    )(x, gains, w)


def _attn_p_body(*refs, has_prev, n_heads):
    if has_prev:
        q_ref, kp_ref, kc_ref, vp_ref, vc_ref, bias_ref, ones_ref, o_ref, m_ref, l_ref = refs
        first_step = (pl.program_id(2) == 0).astype(jnp.int32)
    else:
        q_ref, kc_ref, vc_ref, bias_ref, ones_ref, o_ref, m_ref, l_ref = refs
    n_sub = q_ref.shape[2] // QB
    per = LANES // HEAD_DIM
    n_grp = n_heads // per
    lane = lax.broadcasted_iota(jnp.int32, (QB, LANES), 1)
    lane_head = lax.broadcasted_iota(jnp.int32, (1, LANES), 1) // HEAD_DIM

    def tiles(ref_prev, ref_cur, sub, cols):
        own = ref_cur[0, 0, sub * QB:(sub + 1) * QB, cols]
        if not has_prev:
            return [own]
        before = ref_prev[0, 0, :, cols] if sub == 0 else ref_cur[0, 0, (sub - 1) * QB:sub * QB, cols]
        return [before, own]

    def scores(item):
        sub, grp = item
        cols = slice(grp * LANES, (grp + 1) * LANES)
        q = q_ref[0, 0, sub * QB:(sub + 1) * QB, cols]
        qq = jnp.concatenate([q * jnp.where(lane_head == hh, 1.0, 0.0).astype(BF16) for hh in range(per)], axis=0)
        k = jnp.concatenate(tiles(kp_ref if has_prev else None, kc_ref, sub, cols), axis=0)
        first = first_step if (has_prev and sub == 0) else 0
        return _dot_nt(qq, k) + bias_ref[first, grp]

    items = [(sub, grp) for sub in range(n_sub) for grp in range(n_grp)]
    stats = [[jnp.zeros((QB, LANES), F32), jnp.ones((QB, LANES), F32)]
             for _ in range(n_sub)]
    queue = [scores(it) for it in items[:ATTN_AHEAD]]
    for idx, (sub, grp) in enumerate(items):
        s = queue.pop(0)
        if idx + ATTN_AHEAD < len(items):
            queue.append(scores(items[idx + ATTN_AHEAD]))
        cols = slice(grp * LANES, (grp + 1) * LANES)
        m = jnp.max(s, axis=-1, keepdims=True)
        e = jnp.exp(s - m).astype(BF16)
        v1 = jnp.concatenate([jnp.concatenate([v, ones_ref[...]], axis=1)
                              for v in tiles(vp_ref if has_prev else None, vc_ref, sub, cols)], axis=0)
        ol = _dot(e, v1)
        o, l = ol[:, :LANES], ol[:, LANES:]
        out = o[0:QB]
        for hh in range(per):
            rows = slice(hh * QB, (hh + 1) * QB)
            if hh > 0:
                out = jnp.where(lane_head == hh, o[rows], out)
            stats[sub][0] = jnp.where(lane == grp * per + hh, m[rows], stats[sub][0])
            stats[sub][1] = jnp.where(lane == grp * per + hh, l[rows], stats[sub][1])
        o_ref[0, 0, sub * QB:(sub + 1) * QB, cols] = out
    for sub in range(n_sub):
        m_ref[0, 0, sub * QB:(sub + 1) * QB, :] = stats[sub][0]
        l_ref[0, 0, sub * QB:(sub + 1) * QB, :] = stats[sub][1]


def _attn_p(q, k, v, bias, g, n_heads):
    out_dims = q.shape[:3]
    has_prev = q.shape[2] > QB
    assert bias.shape[0] == (2 if has_prev else 1)
    if not has_prev:
        q, k, v = (a.reshape(a.shape[0], 1, a.shape[1] * a.shape[2], a.shape[3]) for a in (q, k, v))
    batch, dil, sub, e = q.shape
    nb = sub // QB
    ones = jnp.ones((QB, LANES), BF16)
    n_sub = ATTN_BLOCKS_PER_STEP if nb % ATTN_BLOCKS_PER_STEP == 0 else 1
    blk = (1, 1, n_sub * QB, e)
    cur = lambda b, r, n: (b, r, n, 0)
    prev = lambda b, r, n: (b, r, jnp.maximum(n * n_sub - 1, 0), 0)
    if has_prev:
        one = (1, 1, QB, e)
        in_specs = [pl.BlockSpec(blk, cur), pl.BlockSpec(one, prev), pl.BlockSpec(blk, cur),
                    pl.BlockSpec(one, prev), pl.BlockSpec(blk, cur)]
        args = (q, k, k, v, v)
    else:
        in_specs = [pl.BlockSpec(blk, cur)] * 3
        args = (q, k, v)
    outs = pl.pallas_call(
        functools.partial(_attn_p_body, has_prev=has_prev, n_heads=n_heads),
        grid=(batch, dil, nb // n_sub),
        in_specs=in_specs + [_resident(bias.shape), _resident(ones.shape)],
        out_specs=[pl.BlockSpec(blk, cur)] + [pl.BlockSpec((1, 1, n_sub * QB, LANES), cur)] * 2,
        out_shape=[jax.ShapeDtypeStruct((batch, dil, sub, e), F32)]
                  + [jax.ShapeDtypeStruct((batch, dil, sub, LANES), F32)] * 2,
        compiler_params=_params("parallel", "parallel", "arbitrary"),
        name=f"attn_prompt_g{g}",
    )(*args, bias, ones)
    return [a.reshape(out_dims + a.shape[3:]) for a in outs]


def _expand_heads(w, e_ref):
    hi = w.astype(BF16)
    lo = (w - hi.astype(F32)).astype(BF16)
    return _dot(hi, e_ref[...]) + _dot(lo, e_ref[...])


def _merge_groups(os_, ms, ls, e_ref):
    top = functools.reduce(jnp.maximum, ms)
    es = [jnp.exp(m - top) for m in ms]
    den = functools.reduce(lambda a, b: a + b, [e * l for e, l in zip(es, ls)])
    acc = None
    for o, e in zip(os_, es):
        term = _expand_heads(e / den, e_ref) * o
        acc = term if acc is None else acc + term
    return acc


def _merge_body(x_ref, *refs, dils, tm):
    ng = len(dils)
    o_refs, m_refs, l_refs = refs[:ng], refs[ng:2 * ng], refs[2 * ng:3 * ng]
    e_ref, wo_ref, g_ref, out_ref, obuf_ref, sbuf_ref = refs[3 * ng:]
    os_, ms, ls = [], [], []
    for gi, dil in enumerate(dils):
        if dil == 1:
            os_.append(o_refs[gi][0, 0])
            ms.append(m_refs[gi][0, 0])
            ls.append(l_refs[gi][0, 0])
        else:
            n_cc = obuf_ref.shape[1]
            for r in range(dil):
                rows = pl.ds(r, tm // dil, stride=dil)
                for cc in range(n_cc):
                    obuf_ref[gi, cc, rows, :] = o_refs[gi][0, r, :, cc * LANES:(cc + 1) * LANES]
                sbuf_ref[0, gi, rows, :] = m_refs[gi][0, r]
                sbuf_ref[1, gi, rows, :] = l_refs[gi][0, r]
            os_.append(jnp.concatenate([obuf_ref[gi, cc] for cc in range(n_cc)], axis=1))
            ms.append(sbuf_ref[0, gi])
            ls.append(sbuf_ref[1, gi])
    o = _merge_groups(os_, ms, ls, e_ref)
    y = _dot(o.astype(BF16), wo_ref[...])
    out_ref[0] = x_ref[0] + _rms(y, g_ref[1:2])


def _merge(x, os_, ms, ls, expand, w_o, gains, tm):
    b, s, d = x.shape
    dils = tuple(o.shape[1] for o in os_)
    e = os_[0].shape[-1]
    grp = lambda i, j: (i, 0, j, 0)
    return pl.pallas_call(
        functools.partial(_merge_body, dils=dils, tm=tm),
        grid=(b, s // tm),
        in_specs=[pl.BlockSpec((1, tm, d), lambda i, j: (i, j, 0))]
                 + [pl.BlockSpec((1, dil, tm // dil, e), grp) for dil in dils]
                 + [pl.BlockSpec((1, dil, tm // dil, LANES), grp) for dil in dils] * 2
                 + [_resident(expand.shape), _resident(w_o.shape), _resident(gains.shape)],
        out_specs=pl.BlockSpec((1, tm, d), lambda i, j: (i, j, 0)),
        out_shape=jax.ShapeDtypeStruct((b, s, d), F32),
        scratch_shapes=[pltpu.VMEM((len(dils), e // LANES, tm, LANES), F32),
                        pltpu.VMEM((2, len(dils), tm, LANES), F32)],
        compiler_params=_params("parallel", "parallel"),
        name="attn_merge",
    )(x, *os_, *ms, *ls, expand, w_o, gains)


class _WindowAttnChunk:
    def __init__(self, b, q, knew, vnew, c_ref, bias_c, bias_n, cout_ref, *, dil, n_steps, nt, hc):
        self.b, self.q, self.knew, self.vnew, self.c_ref, self.cout_ref = b, q.astype(BF16), knew, vnew, c_ref, cout_ref
        self.bias_c, self.bias_n, self.dil, self.n_steps, self.nt = bias_c, bias_n, dil, n_steps, nt
        self.heads = [slice(hh * HEAD_DIM, (hh + 1) * HEAD_DIM) for hh in range(hc)]
        self.lane = lax.broadcasted_iota(jnp.int32, (nt, LANES), 1)

    def logits(self):
        nt, dil = self.nt, self.dil
        dn = lax.broadcasted_iota(jnp.int32, (nt, LANES), 0) - lax.rem(self.lane, nt)
        valid_n = (self.lane // nt == self.b) & (dn >= 0) & (lax.rem(dn, dil) == 0) & (dn <= dil * self.n_steps)
        self.s_c = [_dot(self.q[:, sl], self.c_ref[0, 0, hh].astype(BF16)) + self.bias_c[hh]
                    for hh, sl in enumerate(self.heads)]
        self.s_n = [jnp.where(valid_n, _dot(self.q[:, sl], self.knew[sl, :].astype(BF16)) + self.bias_n[hh], NEG_INF)
                    for hh, sl in enumerate(self.heads)]

    def softmax(self):
        ms = [jnp.maximum(jnp.max(a, axis=-1, keepdims=True), jnp.max(c, axis=-1, keepdims=True))
              for a, c in zip(self.s_c, self.s_n)]
        e_c = [jnp.exp(a - m) for a, m in zip(self.s_c, ms)]
        e_n = [jnp.exp(c - m) for c, m in zip(self.s_n, ms)]
        ls = [jnp.sum(a, axis=-1, keepdims=True) + jnp.sum(c, axis=-1, keepdims=True) for a, c in zip(e_c, e_n)]
        self.p_c = [(e / l).astype(BF16) for e, l in zip(e_c, ls)]
        self.p_n = [(e / l).astype(BF16) for e, l in zip(e_n, ls)]
        self.lse = jnp.zeros((self.nt, LANES), F32)
        for hh, (m, l) in enumerate(zip(ms, ls)):
            self.lse = jnp.where(self.lane == hh, m + jnp.log(l), self.lse)

    def output(self):
        os_ = [_dot_nt(self.c_ref[0, 1, hh].astype(BF16), self.p_c[hh])
               + _dot_nt(self.vnew[sl, :].astype(BF16), self.p_n[hh]) for hh, sl in enumerate(self.heads)]
        return jnp.concatenate(os_, axis=0), self.lse

    def shift(self):
        n, nt = self.c_ref.shape[-1], self.nt
        tail = lax.broadcasted_iota(jnp.int32, (HEAD_DIM, LANES), 1) >= LANES - nt
        lanes_to_tail = LANES - nt - self.b * nt
        for hh, sl in enumerate(self.heads):
            for kv, new in enumerate((self.knew, self.vnew)):
                rolled = pltpu.roll(self.c_ref[0, kv, hh], n - nt, axis=1)
                placed = pltpu.roll(new[sl, :], lanes_to_tail, axis=1)
                if n > LANES:
                    self.cout_ref[0, kv, hh, :, 0:n - LANES] = rolled[:, 0:n - LANES]
                self.cout_ref[0, kv, hh, :, n - LANES:n] = jnp.where(tail, placed, rolled[:, n - LANES:n])


def _ffn_attn_s_body(x_ref, g_ref, wup_ref, wdn_ref, q_ref, kvt_ref, *refs, groups, nt, hc, f_chunk, n_chunks):
    ng = len(groups)
    c_refs, bc_refs, bn_refs = refs[:ng], refs[ng:2 * ng], refs[2 * ng:3 * ng]
    y_ref = refs[3 * ng]
    o_refs, lse_refs, cout_refs = (refs[3 * ng + 1 + k * ng:3 * ng + 1 + (k + 1) * ng] for k in range(3))
    h_ref, acc_ref = refs[-2:]
    i, j = pl.program_id(0), pl.program_id(1)

    @pl.when(j == 0)
    def _():
        h_ref[...] = _rms(x_ref[...], g_ref[2:3]).astype(BF16)
        acc_ref[...] = jnp.zeros_like(acc_ref)

    cw = hc * HEAD_DIM
    e = cw * n_chunks
    attn = []
    for g, (dil, n_steps) in enumerate(groups):
        knew = kvt_ref[pl.ds(pl.multiple_of((2 * g) * e + j * cw, cw), cw), :]
        vnew = kvt_ref[pl.ds(pl.multiple_of((2 * g + 1) * e + j * cw, cw), cw), :]
        bias_c = [bc_refs[g][j * hc + hh] for hh in range(hc)]
        bias_n = [bn_refs[g][j * hc + hh] for hh in range(hc)]
        attn.append(_WindowAttnChunk(i, q_ref[i, g, j], knew, vnew, c_refs[g], bias_c, bias_n, cout_refs[g],
                                     dil=dil, n_steps=n_steps, nt=nt, hc=hc))
    hidden = pl.ds(pl.multiple_of(j * f_chunk, f_chunk), f_chunk)
    up = jnp.maximum(_dot(h_ref[...], wup_ref[0, :, hidden]), 0.0)
    for a in attn:
        a.logits()
    acc_ref[...] += _dot((up * up).astype(BF16), wdn_ref[0, hidden, :])
    for a in attn:
        a.softmax()
    for g, a in enumerate(attn):
        o_refs[g][0, j], lse_refs[g][0, j] = a.output()
    for a in attn:
        a.shift()

    @pl.when(j == n_chunks - 1)
    def _():
        y_ref[...] = x_ref[...] + _rms(acc_ref[...], g_ref[3:4])


def _ffn_attn_s(x, gains, w_up, w_down, layer, q, kvt_new, caches, bias_cs, bias_ns, groups, nt, tm):
    n, d = x.shape
    n_chunks, f_chunk = q.shape[2], w_up.shape[2] // q.shape[2]
    b, _, n_heads, _, _ = caches[0].shape
    hc = n_heads // n_chunks
    cw = hc * HEAD_DIM
    assert n // tm == b and kvt_new.shape[1] == LANES == b * nt and hc * n_chunks == n_heads
    assert all(c.shape[-1] == dil * n_steps for c, (dil, n_steps) in zip(caches, groups))
    row = lambda i, j: (i, 0)
    win = lambda i, j: (i, 0, j, 0, 0)
    per_row = lambda i, j: (i, 0, 0, 0)
    win_specs = [pl.BlockSpec((1, 2, hc, HEAD_DIM, c.shape[-1]), win) for c in caches]
    ng = len(caches)
    outs = pl.pallas_call(
        functools.partial(_ffn_attn_s_body, groups=groups, nt=nt, hc=hc, f_chunk=f_chunk, n_chunks=n_chunks),
        grid=(b, n_chunks),
        in_specs=[pl.BlockSpec((tm, d), row), _resident(gains.shape), _layer_weights(w_up, layer),
                  _layer_weights(w_down, layer), _resident(q.shape), _resident(kvt_new.shape)]
                 + win_specs + [_resident(a.shape) for a in bias_cs] + [_resident(a.shape) for a in bias_ns],
        out_specs=[pl.BlockSpec((tm, d), row)]
                  + [pl.BlockSpec((1, n_chunks, cw, nt), per_row)] * ng
                  + [pl.BlockSpec((1, n_chunks, nt, LANES), per_row)] * ng + win_specs,
        out_shape=[jax.ShapeDtypeStruct((n, d), F32)]
                  + [jax.ShapeDtypeStruct((b, n_chunks, cw, nt), F32)] * ng
                  + [jax.ShapeDtypeStruct((b, n_chunks, nt, LANES), F32)] * ng
                  + [jax.ShapeDtypeStruct(c.shape, F32) for c in caches],
        scratch_shapes=[pltpu.VMEM((tm, d), BF16), pltpu.VMEM((tm, d), F32)],
        compiler_params=_params("parallel", "arbitrary"),
        name="ffn_with_sample_attn",
    )(x, gains, w_up, w_down, q, kvt_new, *caches, *bias_cs, *bias_ns)
    return outs[0], outs[1:1 + ng], outs[1 + ng:1 + 2 * ng], outs[1 + 2 * ng:]


def _conv_tail(y, x, lng_ref, lnb_ref, w2_ref, b2_ref, g_ref):
    mu = jnp.mean(y, axis=-1, keepdims=True)
    yc = y - mu
    var = jnp.mean(yc * yc, axis=-1, keepdims=True)
    yn = yc * lax.rsqrt(var + EPS) * lng_ref[...] + lnb_ref[...]
    z = _dot((yn * jax.nn.sigmoid(yn)).astype(BF16), w2_ref[...]) + b2_ref[...]
    return x + _rms(z, g_ref[1:2])


def _glu(x, g_ref, w1_ref, b1_ref):
    d = x.shape[1]
    a = _dot(_rms(x, g_ref[0:1]).astype(BF16), w1_ref[...]) + b1_ref[...]
    return a[:, :d] * jax.nn.sigmoid(a[:, d:])


def _conv_p_body(x_ref, g_ref, w1_ref, b1_ref, wdw_ref, bdw_ref, lng_ref, lnb_ref, w2_ref, b2_ref,
                 o_ref, ul_ref, ush_ref, y_ref, *, ts):
    j = pl.program_id(1)
    n_cb = y_ref.shape[0]
    rows_ext = CARRY_ROWS + ts

    @pl.when(j == 0)
    def _():
        ush_ref[0, :, 0:CARRY_ROWS, :] = jnp.zeros((n_cb, CARRY_ROWS, LANES), F32)

    x = x_ref[0]
    u = _glu(x, g_ref, w1_ref, b1_ref)
    for cb in range(n_cb):
        ush_ref[0, cb, CARRY_ROWS:, :] = u[:, cb * LANES:(cb + 1) * LANES]
    n_sh = rows_ext - SUBLANES
    lead = CARRY_ROWS - (CONV_WIDTH - 1)

    def column_block(cb, carry):
        for m in range(1, SUBLANES):
            for r0 in range(0, n_sh, CONV_COPY_ROWS):
                r1 = min(r0 + CONV_COPY_ROWS, n_sh)
                ush_ref[m, cb, r0:r1, :] = ush_ref[0, cb, r0 + m:r1 + m, :]

        def rows(c, carry):
            r0 = pl.multiple_of(c * CONV_CHUNK_ROWS, CONV_CHUNK_ROWS)
            acc = jnp.broadcast_to(bdw_ref[cb], (CONV_CHUNK_ROWS, LANES))
            for k in range(CONV_WIDTH):
                a, m = divmod(lead + k, SUBLANES)
                acc = acc + wdw_ref[cb, k:k + 1, :] * ush_ref[m, cb, pl.ds(r0 + a * SUBLANES, CONV_CHUNK_ROWS), :]
            y_ref[cb, pl.ds(r0, CONV_CHUNK_ROWS), :] = acc
            return carry

        return lax.fori_loop(0, ts // CONV_CHUNK_ROWS, rows, carry)

    lax.fori_loop(0, n_cb, column_block, 0)
    y = jnp.concatenate([y_ref[cb] for cb in range(n_cb)], axis=1)
    o_ref[0] = _conv_tail(y, x, lng_ref, lnb_ref, w2_ref, b2_ref, g_ref)
    last = [ush_ref[0, cb, ts:ts + CARRY_ROWS, :] for cb in range(n_cb)]
    for cb in range(n_cb):
        ush_ref[0, cb, 0:CARRY_ROWS, :] = last[cb]
    ul_ref[0] = jnp.concatenate(last, axis=1)


def _conv_p(x, gains, w1, b1, wdw, bdw, lng, lnb, w2, b2, ts):
    b, s, d = x.shape
    n_cb = d // LANES
    wdw_cb = wdw.reshape(wdw.shape[0], n_cb, LANES).transpose(1, 0, 2)
    bdw_cb = bdw.reshape(n_cb, 1, LANES)
    consts = (gains, w1, b1, wdw_cb, bdw_cb, lng, lnb, w2, b2)
    return pl.pallas_call(
        functools.partial(_conv_p_body, ts=ts),
        grid=(b, s // ts),
        in_specs=[pl.BlockSpec((1, ts, d), lambda i, j: (i, j, 0))] + [_resident(a.shape) for a in consts],
        out_specs=[pl.BlockSpec((1, ts, d), lambda i, j: (i, j, 0)),
                   pl.BlockSpec((1, CARRY_ROWS, d), lambda i, j: (i, 0, 0))],
        out_shape=[jax.ShapeDtypeStruct((b, s, d), F32), jax.ShapeDtypeStruct((b, CARRY_ROWS, d), F32)],
        scratch_shapes=[pltpu.VMEM((SUBLANES, n_cb, CARRY_ROWS + ts, LANES), F32), pltpu.VMEM((n_cb, ts, LANES), F32)],
        compiler_params=_params("parallel", "arbitrary"),
        name="conv_prompt",
    )(x, *consts)


def _conv_s_body(x_ref, st_ref, g_ref, w1_ref, b1_ref, wdw_ref, bdw_ref, lng_ref, lnb_ref, w2_ref, b2_ref,
                 o_ref, ns_ref, *, nb, nt):
    x = x_ref[...]
    u = _glu(x, g_ref, w1_ref, b1_ref)
    n_prev = CONV_WIDTH - 1
    ext = [st_ref[i * nb:(i + 1) * nb, :] for i in range(n_prev)]
    ext += [u[t * nb:(t + 1) * nb, :] for t in range(nt)]
    ys = []
    for t in range(nt):
        y = None
        for k in range(CONV_WIDTH):
            term = wdw_ref[k:k + 1, :] * ext[t + k]
            y = term if y is None else y + term
        ys.append(y + bdw_ref[...])
    o_ref[...] = _conv_tail(jnp.concatenate(ys, axis=0), x, lng_ref, lnb_ref, w2_ref, b2_ref, g_ref)
    for i in range(n_prev):
        ns_ref[i * nb:(i + 1) * nb, :] = ext[nt + i]


def _conv_s(x, state, gains, w1, b1, wdw, bdw, lng, lnb, w2, b2, nb, nt):
    return pl.pallas_call(
        functools.partial(_conv_s_body, nb=nb, nt=nt),
        out_shape=[jax.ShapeDtypeStruct(x.shape, F32), jax.ShapeDtypeStruct(state.shape, F32)],
        compiler_params=pltpu.CompilerParams(vmem_limit_bytes=VMEM_LIMIT),
        name="conv_sample",
    )(x, state, gains, w1, b1, wdw, bdw, lng, lnb, w2, b2)


def _t5_bucket(dist):
    max_exact = N_BUCKETS // 2
    df = jnp.maximum(dist, 1).astype(F32)
    large = max_exact + (jnp.log(df / max_exact) / math.log(MAX_DISTANCE / max_exact)
                         * (N_BUCKETS - max_exact)).astype(jnp.int32)
    large = jnp.minimum(large, N_BUCKETS - 1)
    return jnp.where(dist < max_exact, dist, large)


def _bias_by_step(rel_bias, g, dil, n_steps, n_heads):
    tab = rel_bias[:, g * n_heads:(g + 1) * n_heads]
    bucket = _t5_bucket(dil * jnp.arange(n_steps + 1, dtype=jnp.int32))
    onehot = (bucket[:, None] == jnp.arange(N_BUCKETS, dtype=jnp.int32)[None, :]).astype(F32)
    return jnp.dot(onehot, tab, precision=lax.Precision.HIGHEST)


def _band_bias(by_step, n_steps, nk):
    off = nk - QB
    period = 2 * nk
    heads = by_step.shape[1]
    base = jnp.concatenate([by_step[::-1].T, jnp.full((heads, period - n_steps - 1), NEG_INF, F32)], axis=1)
    w = jnp.roll(base, off - n_steps, axis=1)
    r = period - 1
    flat = jnp.tile(w, (1, -(-(QB * r) // period)))[:, :QB * r]
    band = flat.reshape(heads, QB, r)[:, :, :nk]
    if off > 0:
        hidden = (np.arange(nk) < off)[None, None, :]
        band = jnp.stack([band, jnp.where(hidden, NEG_INF, band)])
    else:
        band = band[None]
    per = LANES // HEAD_DIM
    return band.reshape(band.shape[0], heads // per, per * QB, nk)


def _window_bias(by_step, dil, n_steps, nt):
    base = jnp.repeat(by_step[n_steps:0:-1].T, dil, axis=1)
    bias = jnp.stack([jnp.roll(base, t, axis=1) for t in range(nt)], axis=1)
    n = dil * n_steps
    delta = n + np.arange(nt)[:, None] - np.arange(n)[None, :]
    attended = (delta % dil == 0) & (delta <= dil * n_steps)
    return jnp.where(attended[None], bias, NEG_INF)


def _new_key_bias(by_step, dil, n_steps, nt):
    zero = jnp.zeros_like(by_step[0])
    rows = []
    for t in range(nt):
        cols = []
        for tp in range(nt):
            s, rem = divmod(t - tp, dil)
            cols.append(by_step[s] if (t >= tp and rem == 0 and s <= n_steps) else zero)
        rows.append(jnp.stack(cols, axis=1))
    return jnp.tile(jnp.stack(rows, axis=1), (1, 1, LANES // nt))


def _time_major(a):
    b, t, d = a.shape
    return a.transpose(1, 0, 2).reshape(t * b, d)


def _batch_major(a, b):
    return a.reshape(-1, b, a.shape[1]).transpose(1, 0, 2)


def _kv_window_layout(c):
    return c.transpose(0, 2, 3, 4, 1)


def _kv_window_unlayout(c):
    return c.transpose(0, 4, 1, 2, 3)


def kernel(x_prompt, x_sample, state_pool, cache_kv_g0, cache_kv_g1, cache_kv_g2, state_conv, norm_gains, rel_bias, pool_w, pool_scale, attn_w_qkv, attn_w_o, conv_w_pw1, conv_b_pw1, conv_w_dw, conv_b_dw, conv_ln_g, conv_ln_b, conv_w_pw2, conv_b_pw2, ffn_w_up, ffn_w_down):
    batch, seq, d = x_prompt.shape
    dec_b, dec_t, _ = x_sample.shape
    depth = norm_gains.shape[0]
    n_heads = attn_w_o.shape[1] // HEAD_DIM
    e_dim = n_heads * HEAD_DIM
    n_groups = len(ATTN_GROUPS)
    dils = tuple(dil for _, dil in ATTN_GROUPS)
    caches_in = (cache_kv_g0, cache_kv_g1, cache_kv_g2)
    n_tok = batch * seq
    n_dec = dec_b * dec_t

    xp = x_prompt
    xs = _time_major(x_sample)

    expand = jnp.asarray(np.kron(np.eye(LANES, n_heads), np.ones((1, HEAD_DIM))), BF16)

    w_up, w_down = ffn_w_up.astype(BF16), ffn_w_down.astype(BF16)

    pool_p, pool_s, conv_p, conv_s = [], [], [], []
    kv_p = [[] for _ in ATTN_GROUPS]
    kv_s = [[] for _ in ATTN_GROUPS]
    for i in range(depth):
        kind, j = i % 3, i // 3
        gains = norm_gains[i]
        if kind == 0:
            pw, ps = pool_w[j].astype(BF16), pool_scale[j][None, :]
            xp, h_last = _pool_p(xp, gains, pw, ps, w_up, w_down, i, ts=512)
            pool_p.append(h_last[:, POOL_CARRY - POOL_STATE:])
            xs, new_state = _pool_s(xs, _time_major(state_pool[j]), gains, pw, ps,
                                    nb=dec_b, nt=dec_t, pos0=PAST_LEN)
            pool_s.append(_batch_major(new_state, dec_b))
            xs = _ffn(xs, gains, w_up, w_down, i, tm=n_dec)
            continue
        elif kind == 1:
            w_qkv = attn_w_qkv[j].astype(BF16)
            w_o = attn_w_o[j].astype(BF16)
            by_step = [_bias_by_step(rel_bias, g, dil, win // dil, n_heads)
                       for g, (win, dil) in enumerate(ATTN_GROUPS)]
            keeps = [min(win, seq) for win, _ in ATTN_GROUPS]
            qkv, kvts = [], []
            for lo, hi in QKV_GROUP_SPLIT:
                outs = _qkv_p(xp, gains, w_qkv, lo, dils[lo:hi], keeps[lo:hi], e_dim, tm=512)
                qkv += outs[:3 * (hi - lo)]
                kvts += outs[3 * (hi - lo):]
            os_, ms, ls = [], [], []
            for g, (win, dil) in enumerate(ATTN_GROUPS):
                n_steps = win // dil
                nk = 2 * QB if seq // dil > QB else QB
                bias = _band_bias(by_step[g], n_steps, nk)
                o, m, l = _attn_p(qkv[3 * g], qkv[3 * g + 1], qkv[3 * g + 2], bias, g, n_heads)
                os_.append(o)
                ms.append(m)
                ls.append(l)
                kv_p[g].append(_kv_window_unlayout(kvts[g].reshape(batch, 2, n_heads, HEAD_DIM, keeps[g])))
            xp = _merge(xp, os_, ms, ls, expand, w_o, gains, tm=512)
            xs_b = _batch_major(xs, dec_b)
            xs_flat = xs_b.reshape(n_dec, d)
            n_chunks = ffn_w_up.shape[2] // FFN_CHUNK
            hc = n_heads // n_chunks
            q_s, kvt_new = _qkv_s(xs_flat, gains, w_qkv, n_groups)
            q_s = q_s.reshape(dec_b, dec_t, n_groups, n_chunks, hc * HEAD_DIM)
            q_s = q_s.transpose(0, 2, 3, 1, 4)
            groups = tuple((dil, win // dil) for win, dil in ATTN_GROUPS)
            caches = [_kv_window_layout(c[j]) for c in caches_in]
            bias_cs = [_window_bias(by_step[g], dil, n_steps, dec_t) for g, (dil, n_steps) in enumerate(groups)]
            bias_ns = [_new_key_bias(by_step[g], dil, n_steps, dec_t) for g, (dil, n_steps) in enumerate(groups)]
            xp, os_, lses, new_caches = _ffn_attn_s(xp.reshape(n_tok, d), gains, w_up, w_down, i, q_s, kvt_new,
                                                    caches, bias_cs, bias_ns, groups, dec_t, tm=n_tok // dec_b)
            xp = xp.reshape(batch, seq, d)
            ms, ls = [], []
            for g in range(n_groups):
                kv_s[g].append(_kv_window_unlayout(new_caches[g]))
                lse = lses[g][..., :hc].transpose(0, 2, 1, 3).reshape(1, 1, n_dec, n_heads)
                ms.append(jnp.pad(lse, ((0, 0), (0, 0), (0, 0), (0, LANES - n_heads))))
                ls.append(jnp.ones((1, 1, n_dec, LANES), F32))
            os_ = [o.transpose(0, 3, 1, 2).reshape(1, 1, n_dec, e_dim) for o in os_]
            xs_flat = _merge(xs_flat[None], os_, ms, ls, expand, w_o, gains, tm=n_dec)[0]
            xs = _time_major(xs_flat.reshape(dec_b, dec_t, d))
            xs = _ffn(xs, gains, w_up, w_down, i, tm=n_dec)
            continue
        else:
            cw = (conv_w_pw1[j].astype(BF16), conv_b_pw1[j][None, :], conv_w_dw[j], conv_b_dw[j][None, :],
                  conv_ln_g[j][None, :], conv_ln_b[j][None, :], conv_w_pw2[j].astype(BF16), conv_b_pw2[j][None, :])
            xp, u_last = _conv_p(xp, gains, *cw, ts=512)
            conv_p.append(u_last[:, CARRY_ROWS - (CONV_WIDTH - 1):])
            xs, new_state = _conv_s(xs, _time_major(state_conv[j]), gains, *cw, nb=dec_b, nt=dec_t)
            conv_s.append(_batch_major(new_state, dec_b))
        xp = _ffn(xp.reshape(n_tok, d), gains, w_up, w_down, i, tm=512).reshape(batch, seq, d)
        xs = _ffn(xs, gains, w_up, w_down, i, tm=n_dec)
    return (xp, _batch_major(xs, dec_b),
            jnp.stack(pool_p), jnp.stack(pool_s),
            jnp.stack(kv_p[0]), jnp.stack(kv_s[0]),
            jnp.stack(kv_p[1]), jnp.stack(kv_s[1]),
            jnp.stack(kv_p[2]), jnp.stack(kv_s[2]),
            jnp.stack(conv_p), jnp.stack(conv_s))
```

```python
import functools
import math

import jax
import jax.numpy as jnp
import numpy as np
from jax import lax
from jax.experimental import pallas as pl
from jax.experimental.pallas import tpu as pltpu

F32 = jnp.float32
BF16 = jnp.bfloat16

EPS = 1e-6
NEG_INF = -1e30
POOL_WINDOWS = (2, 4, 8, 16)
POOL_STATE = max(POOL_WINDOWS) - 1
PAST_LEN = 8192
ATTN_GROUPS = ((128, 1), (512, 4), (2048, 16))
HEAD_DIM = 64
QB = 128
ATTN_BLOCKS_PER_STEP = 4
ATTN_AHEAD = 4
N_BUCKETS = 32
MAX_DISTANCE = 2048
CONV_WIDTH = 31
CARRY_ROWS = 32
POOL_CARRY = 16
LANES = 128
SUBLANES = 8
CONV_CHUNK_ROWS = 128
CONV_COPY_ROWS = 128

VMEM_LIMIT = 60 * 1024 * 1024
QKV_GROUP_SPLIT = ((0, 2), (2, 3))
MLP_CHUNK = 1024
FFN_CHUNK = 1024

NT_DIMS = (((1,), (1,)), ((), ()))


def _params(*sem):
    return pltpu.CompilerParams(dimension_semantics=sem, vmem_limit_bytes=VMEM_LIMIT)


def _resident(shape):
    nd = len(shape)
    return pl.BlockSpec(shape, lambda *_: (0,) * nd, pipeline_mode=pl.Buffered(1))


def _rms(x, g):
    return x * lax.rsqrt(jnp.mean(x * x, axis=-1, keepdims=True) + EPS) * g


def _dot(a, b):
    return jnp.dot(a, b, preferred_element_type=F32)


def _dot_nt(a, b):
    return lax.dot_general(a, b, NT_DIMS, preferred_element_type=F32)


def _mlp(x, g_ref, wup_ref, wdn_ref):
    h = _rms(x, g_ref[2:3]).astype(BF16)
    acc = jnp.zeros_like(x)
    for c in range(wup_ref.shape[2] // MLP_CHUNK):
        cols = slice(c * MLP_CHUNK, (c + 1) * MLP_CHUNK)
        a = jnp.maximum(_dot(h, wup_ref[0, :, cols]), 0.0)
        acc = acc + _dot((a * a).astype(BF16), wdn_ref[0, cols, :])
    return x + _rms(acc, g_ref[3:4])


def _ffn_body(x_ref, g_ref, wup_ref, wdn_ref, o_ref):
    o_ref[...] = _mlp(x_ref[...], g_ref, wup_ref, wdn_ref)


def _layer_weights(w, layer):
    nd = w.ndim - 1
    return pl.BlockSpec((1,) + w.shape[1:], lambda *_: (layer,) + (0,) * nd, pipeline_mode=pl.Buffered(1))


def _ffn(x, gains, w_up, w_down, layer, tm):
    n, d = x.shape
    return pl.pallas_call(
        _ffn_body,
        grid=(n // tm,),
        in_specs=[pl.BlockSpec((tm, d), lambda i: (i, 0)), _resident(gains.shape),
                  _layer_weights(w_up, layer), _layer_weights(w_down, layer)],
        out_specs=pl.BlockSpec((tm, d), lambda i: (i, 0)),
        out_shape=jax.ShapeDtypeStruct((n, d), F32),
        compiler_params=_params("parallel"),
        name="ffn",
    )(x, gains, w_up, w_down)


def _pool_p_body(x_ref, g_ref, pw_ref, ps_ref, wup_ref, wdn_ref, o_ref, hl_ref, carry_ref, *, ts):
    j = pl.program_id(1)

    @pl.when(j == 0)
    def _():
        carry_ref[...] = jnp.zeros_like(carry_ref)

    x = x_ref[0]
    h = _rms(x, g_ref[0:1])
    ext = jnp.concatenate([carry_ref[...], h], axis=0)
    gw = pw_ref.shape[1]
    assert POOL_WINDOWS == tuple(2 << gi for gi in range(len(POOL_WINDOWS)))
    sums = []
    s = ext
    for gi in range(len(POOL_WINDOWS)):
        s = s + pltpu.roll(s, 1 << gi, axis=0)
        sums.append(s[:, :gw])
        if gi + 1 < len(POOL_WINDOWS):
            s = s[:, gw:]
    pos = j * ts + lax.broadcasted_iota(jnp.int32, (ts, 1), 0)
    ys = []
    for gi, (w, s) in enumerate(zip(POOL_WINDOWS, sums)):
        cols = slice(gi * gw, (gi + 1) * gw)
        inv_cnt = 1.0 / jnp.minimum(pos + 1, w).astype(F32)
        pooled = s[POOL_CARRY:, :] * inv_cnt - h[:, cols]
        ys.append(_dot(pooled.astype(BF16), pw_ref[gi]))
    y = jnp.concatenate(ys, axis=1) * ps_ref[...]
    carry_ref[...] = h[ts - POOL_CARRY:, :]
    hl_ref[0] = h[ts - POOL_CARRY:, :]
    o_ref[0] = _mlp(x + _rms(y, g_ref[1:2]), g_ref, wup_ref, wdn_ref)


def _pool_p(x, gains, pw, ps, w_up, w_down, layer, ts):
    b, s, d = x.shape
    return pl.pallas_call(
        functools.partial(_pool_p_body, ts=ts),
        grid=(b, s // ts),
        in_specs=[pl.BlockSpec((1, ts, d), lambda i, j: (i, j, 0)), _resident(gains.shape),
                  _resident(pw.shape), _resident(ps.shape),
                  _layer_weights(w_up, layer), _layer_weights(w_down, layer)],
        out_specs=[pl.BlockSpec((1, ts, d), lambda i, j: (i, j, 0)),
                   pl.BlockSpec((1, POOL_CARRY, d), lambda i, j: (i, 0, 0))],
        out_shape=[jax.ShapeDtypeStruct((b, s, d), F32),
                   jax.ShapeDtypeStruct((b, POOL_CARRY, d), F32)],
        scratch_shapes=[pltpu.VMEM((POOL_CARRY, d), F32)],
        compiler_params=_params("parallel", "arbitrary"),
        name="pool_mlp_prompt",
    )(x, gains, pw, ps, w_up, w_down)


def _pool_s_body(x_ref, st_ref, g_ref, pw_ref, ps_ref, o_ref, ns_ref, *, nb, nt, pos0):
    x = x_ref[...]
    h = _rms(x, g_ref[0:1])
    ext = [st_ref[i * nb:(i + 1) * nb, :] for i in range(POOL_STATE)]
    ext += [h[t * nb:(t + 1) * nb, :] for t in range(nt)]
    gw = pw_ref.shape[1]
    for t in range(nt):
        ys = []
        for gi, w in enumerate(POOL_WINDOWS):
            cols = slice(gi * gw, (gi + 1) * gw)
            end = POOL_STATE + t
            acc = ext[end][:, cols]
            for i in range(1, min(w, end + 1)):
                acc = acc + ext[end - i][:, cols]
            pooled = acc / float(min(pos0 + t + 1, w)) - ext[end][:, cols]
            ys.append(_dot(pooled.astype(BF16), pw_ref[gi]))
        y = jnp.concatenate(ys, axis=1) * ps_ref[...]
        o_ref[t * nb:(t + 1) * nb, :] = x[t * nb:(t + 1) * nb, :] + _rms(y, g_ref[1:2])
    for i in range(POOL_STATE):
        ns_ref[i * nb:(i + 1) * nb, :] = ext[nt + i]


def _pool_s(x, state, gains, pw, ps, nb, nt, pos0):
    return pl.pallas_call(
        functools.partial(_pool_s_body, nb=nb, nt=nt, pos0=pos0),
        out_shape=[jax.ShapeDtypeStruct(x.shape, F32), jax.ShapeDtypeStruct(state.shape, F32)],
        compiler_params=pltpu.CompilerParams(vmem_limit_bytes=VMEM_LIMIT),
        name="pool_sample",
    )(x, state, gains, pw, ps)


def _kept_tiles(seq, keep, tm):
    rows = min(keep, tm)
    tiles = keep // rows
    return seq // tm - tiles, tiles, rows


def _qkv_p_body(x_ref, g_ref, w_ref, *refs, dils, keeps, seq, tm):
    ng = len(dils)
    out_refs, kvt_refs, ybuf_ref = refs[:3 * ng], refs[3 * ng:4 * ng], refs[-1]
    e = out_refs[0].shape[-1]
    j = pl.program_id(1)
    h = _rms(x_ref[0], g_ref[0:1]).astype(BF16)
    for g, dil in enumerate(dils):
        first, _, rows = _kept_tiles(seq, keeps[g], tm)
        for c in range(3):
            idx = g * 3 + c
            y = _dot(h, w_ref[:, idx * e:(idx + 1) * e])
            if c == 0:
                y = y * (HEAD_DIM ** -0.5)
            else:
                def keep_window(y=y, g=g, c=c, rows=rows):
                    kvt_refs[g][0, (c - 1) * e:c * e, :] = y[tm - rows:, :].T

                if first == 0:
                    keep_window()
                else:
                    pl.when(j >= first)(keep_window)
            if dil == 1:
                out_refs[idx][0, 0] = y.astype(BF16)
            else:
                slot = idx % 2
                for cc in range(e // LANES):
                    ybuf_ref[slot, cc] = y[:, cc * LANES:(cc + 1) * LANES]
                for r in range(dil):
                    out_refs[idx][0, r] = jnp.concatenate(
                        [ybuf_ref[slot, cc, pl.ds(r, tm // dil, stride=dil), :] for cc in range(e // LANES)],
                        axis=1).astype(BF16)


def _qkv_p(x, gains, w, first_group, dils, keeps, e, tm):
    b, s, d = x.shape
    width = len(dils) * 3 * e
    assert (first_group * 3 * e) % width == 0
    w_spec = pl.BlockSpec((d, width), lambda i, j: (0, first_group * 3 * e // width), pipeline_mode=pl.Buffered(1))
    out_specs, out_shape = [], []
    for dil in dils:
        for _ in range(3):
            out_specs.append(pl.BlockSpec((1, dil, tm // dil, e), lambda i, j: (i, 0, j, 0)))
            out_shape.append(jax.ShapeDtypeStruct((b, dil, s // dil, e), BF16))
    for keep in keeps:
        first, _, rows = _kept_tiles(s, keep, tm)
        out_specs.append(pl.BlockSpec((1, 2 * e, rows), lambda i, j, first=first: (i, 0, jnp.maximum(j - first, 0))))
        out_shape.append(jax.ShapeDtypeStruct((b, 2 * e, keep), F32))
    return pl.pallas_call(
        functools.partial(_qkv_p_body, dils=dils, keeps=tuple(keeps), seq=s, tm=tm),
        grid=(b, s // tm),
        in_specs=[pl.BlockSpec((1, tm, d), lambda i, j: (i, j, 0)), _resident(gains.shape), w_spec],
        out_specs=out_specs,
        out_shape=out_shape,
        scratch_shapes=[pltpu.VMEM((2, e // LANES, tm, LANES), F32)],
        compiler_params=_params("parallel", "arbitrary"),
        name="qkv_prompt",
    )(x, gains, w)


def _qkv_s_body(x_ref, g_ref, w_ref, q_ref, kvt_ref, *, n_groups):
    e = w_ref.shape[1] // (3 * n_groups)
    h = _rms(x_ref[...], g_ref[0:1]).astype(BF16)
    for g in range(n_groups):
        q_ref[:, g * e:(g + 1) * e] = _dot(h, w_ref[:, 3 * g * e:(3 * g + 1) * e]) * (HEAD_DIM ** -0.5)
        for c in (1, 2):
            kv = _dot(h, w_ref[:, (3 * g + c) * e:(3 * g + c + 1) * e])
            kvt_ref[(2 * g + c - 1) * e:(2 * g + c) * e, :] = kv.T


def _qkv_s(x, gains, w, n_groups):
    n = x.shape[0]
    e = w.shape[1] // (3 * n_groups)
    return pl.pallas_call(
        functools.partial(_qkv_s_body, n_groups=n_groups),
        out_shape=[jax.ShapeDtypeStruct((n, n_groups * e), F32), jax.ShapeDtypeStruct((n_groups * 2 * e, n), F32)],
        compiler_params=pltpu.CompilerParams(vmem_limit_bytes=VMEM_LIMIT),
        name="qkv_sample",
    )(x, gains, w)


def _attn_p_body(*refs, has_prev, n_heads):
    if has_prev:
        q_ref, kp_ref, kc_ref, vp_ref, vc_ref, bias_ref, ones_ref, o_ref, m_ref, l_ref = refs
        first_step = (pl.program_id(2) == 0).astype(jnp.int32)
    else:
        q_ref, kc_ref, vc_ref, bias_ref, ones_ref, o_ref, m_ref, l_ref = refs
    n_sub = q_ref.shape[2] // QB
    per = LANES // HEAD_DIM
    n_grp = n_heads // per
    lane = lax.broadcasted_iota(jnp.int32, (QB, LANES), 1)
    lane_head = lax.broadcasted_iota(jnp.int32, (1, LANES), 1) // HEAD_DIM

    def tiles(ref_prev, ref_cur, sub, cols):
        own = ref_cur[0, 0, sub * QB:(sub + 1) * QB, cols]
        if not has_prev:
            return [own]
        before = ref_prev[0, 0, :, cols] if sub == 0 else ref_cur[0, 0, (sub - 1) * QB:sub * QB, cols]
        return [before, own]

    def scores(item):
        sub, grp = item
        cols = slice(grp * LANES, (grp + 1) * LANES)
        q = q_ref[0, 0, sub * QB:(sub + 1) * QB, cols]
        qq = jnp.concatenate([q * jnp.where(lane_head == hh, 1.0, 0.0).astype(BF16) for hh in range(per)], axis=0)
        k = jnp.concatenate(tiles(kp_ref if has_prev else None, kc_ref, sub, cols), axis=0)
        first = first_step if (has_prev and sub == 0) else 0
        return _dot_nt(qq, k) + bias_ref[first, grp]

    items = [(sub, grp) for sub in range(n_sub) for grp in range(n_grp)]
    stats = [[jnp.zeros((QB, LANES), F32), jnp.ones((QB, LANES), F32)]
             for _ in range(n_sub)]
    queue = [scores(it) for it in items[:ATTN_AHEAD]]
    for idx, (sub, grp) in enumerate(items):
        s = queue.pop(0)
        if idx + ATTN_AHEAD < len(items):
            queue.append(scores(items[idx + ATTN_AHEAD]))
        cols = slice(grp * LANES, (grp + 1) * LANES)
        m = jnp.max(s, axis=-1, keepdims=True)
        e = jnp.exp(s - m).astype(BF16)
        v1 = jnp.concatenate([jnp.concatenate([v, ones_ref[...]], axis=1)
                              for v in tiles(vp_ref if has_prev else None, vc_ref, sub, cols)], axis=0)
        ol = _dot(e, v1)
        o, l = ol[:, :LANES], ol[:, LANES:]
        out = o[0:QB]
        for hh in range(per):
            rows = slice(hh * QB, (hh + 1) * QB)
            if hh > 0:
                out = jnp.where(lane_head == hh, o[rows], out)
            stats[sub][0] = jnp.where(lane == grp * per + hh, m[rows], stats[sub][0])
            stats[sub][1] = jnp.where(lane == grp * per + hh, l[rows], stats[sub][1])
        o_ref[0, 0, sub * QB:(sub + 1) * QB, cols] = out.astype(o_ref.dtype)
    for sub in range(n_sub):
        m_ref[0, 0, sub * QB:(sub + 1) * QB, :] = stats[sub][0]
        l_ref[0, 0, sub * QB:(sub + 1) * QB, :] = stats[sub][1]


def _attn_p(q, k, v, bias, g, n_heads):
    out_dims = q.shape[:3]
    has_prev = q.shape[2] > QB
    assert bias.shape[0] == (2 if has_prev else 1)
    if not has_prev:
        q, k, v = (a.reshape(a.shape[0], 1, a.shape[1] * a.shape[2], a.shape[3]) for a in (q, k, v))
    batch, dil, sub, e = q.shape
    nb = sub // QB
    ones = jnp.ones((QB, LANES), BF16)
    n_sub = ATTN_BLOCKS_PER_STEP if nb % ATTN_BLOCKS_PER_STEP == 0 else 1
    blk = (1, 1, n_sub * QB, e)
    cur = lambda b, r, n: (b, r, n, 0)
    prev = lambda b, r, n: (b, r, jnp.maximum(n * n_sub - 1, 0), 0)
    if has_prev:
        one = (1, 1, QB, e)
        in_specs = [pl.BlockSpec(blk, cur), pl.BlockSpec(one, prev), pl.BlockSpec(blk, cur),
                    pl.BlockSpec(one, prev), pl.BlockSpec(blk, cur)]
        args = (q, k, k, v, v)
    else:
        in_specs = [pl.BlockSpec(blk, cur)] * 3
        args = (q, k, v)
    outs = pl.pallas_call(
        functools.partial(_attn_p_body, has_prev=has_prev, n_heads=n_heads),
        grid=(batch, dil, nb // n_sub),
        in_specs=in_specs + [_resident(bias.shape), _resident(ones.shape)],
        out_specs=[pl.BlockSpec(blk, cur)] + [pl.BlockSpec((1, 1, n_sub * QB, LANES), cur)] * 2,
        out_shape=[jax.ShapeDtypeStruct((batch, dil, sub, e), BF16)]
                  + [jax.ShapeDtypeStruct((batch, dil, sub, LANES), F32)] * 2,
        compiler_params=_params("parallel", "parallel", "arbitrary"),
        name=f"attn_prompt_g{g}",
    )(*args, bias, ones)
    return [a.reshape(out_dims + a.shape[3:]) for a in outs]


def _expand_heads(w, e_ref):
    hi = w.astype(BF16)
    lo = (w - hi.astype(F32)).astype(BF16)
    return _dot(hi, e_ref[...]) + _dot(lo, e_ref[...])


def _merge_groups(os_, ms, ls, e_ref):
    top = functools.reduce(jnp.maximum, ms)
    es = [jnp.exp(m - top) for m in ms]
    den = functools.reduce(lambda a, b: a + b, [e * l for e, l in zip(es, ls)])
    acc = None
    for o, e in zip(os_, es):
        term = _expand_heads(e / den, e_ref) * o
        acc = term if acc is None else acc + term
    return acc


def _merge_body(x_ref, *refs, dils, tm):
    ng = len(dils)
    o_refs, m_refs, l_refs = refs[:ng], refs[ng:2 * ng], refs[2 * ng:3 * ng]
    e_ref, wo_ref, g_ref, out_ref, obuf_ref, sbuf_ref = refs[3 * ng:]
    os_, ms, ls = [], [], []
    for gi, dil in enumerate(dils):
        if dil == 1:
            os_.append(o_refs[gi][0, 0].astype(F32))
            ms.append(m_refs[gi][0, 0])
            ls.append(l_refs[gi][0, 0])
        else:
            n_cc = obuf_ref.shape[1]
            for r in range(dil):
                rows = pl.ds(r, tm // dil, stride=dil)
                for cc in range(n_cc):
                    obuf_ref[gi, cc, rows, :] = o_refs[gi][0, r, :, cc * LANES:(cc + 1) * LANES].astype(F32)
                sbuf_ref[0, gi, rows, :] = m_refs[gi][0, r]
                sbuf_ref[1, gi, rows, :] = l_refs[gi][0, r]
            os_.append(jnp.concatenate([obuf_ref[gi, cc] for cc in range(n_cc)], axis=1))
            ms.append(sbuf_ref[0, gi])
            ls.append(sbuf_ref[1, gi])
    o = _merge_groups(os_, ms, ls, e_ref)
    y = _dot(o.astype(BF16), wo_ref[...])
    out_ref[0] = x_ref[0] + _rms(y, g_ref[1:2])


def _merge(x, os_, ms, ls, expand, w_o, gains, tm):
    b, s, d = x.shape
    dils = tuple(o.shape[1] for o in os_)
    e = os_[0].shape[-1]
    grp = lambda i, j: (i, 0, j, 0)
    return pl.pallas_call(
        functools.partial(_merge_body, dils=dils, tm=tm),
        grid=(b, s // tm),
        in_specs=[pl.BlockSpec((1, tm, d), lambda i, j: (i, j, 0))]
                 + [pl.BlockSpec((1, dil, tm // dil, e), grp) for dil in dils]
                 + [pl.BlockSpec((1, dil, tm // dil, LANES), grp) for dil in dils] * 2
                 + [_resident(expand.shape), _resident(w_o.shape), _resident(gains.shape)],
        out_specs=pl.BlockSpec((1, tm, d), lambda i, j: (i, j, 0)),
        out_shape=jax.ShapeDtypeStruct((b, s, d), F32),
        scratch_shapes=[pltpu.VMEM((len(dils), e // LANES, tm, LANES), F32),
                        pltpu.VMEM((2, len(dils), tm, LANES), F32)],
        compiler_params=_params("parallel", "parallel"),
        name="attn_merge",
    )(x, *os_, *ms, *ls, expand, w_o, gains)


class _WindowAttnChunk:
    def __init__(self, b, q, knew, vnew, c_ref, bias_c, bias_n, cout_ref, *, dil, n_steps, nt, hc):
        self.b, self.q, self.knew, self.vnew, self.c_ref, self.cout_ref = b, q.astype(BF16), knew, vnew, c_ref, cout_ref
        self.bias_c, self.bias_n, self.dil, self.n_steps, self.nt = bias_c, bias_n, dil, n_steps, nt
        self.heads = [slice(hh * HEAD_DIM, (hh + 1) * HEAD_DIM) for hh in range(hc)]
        self.lane = lax.broadcasted_iota(jnp.int32, (nt, LANES), 1)

    def logits(self):
        nt, dil = self.nt, self.dil
        dn = lax.broadcasted_iota(jnp.int32, (nt, LANES), 0) - lax.rem(self.lane, nt)
        valid_n = (self.lane // nt == self.b) & (dn >= 0) & (lax.rem(dn, dil) == 0) & (dn <= dil * self.n_steps)
        self.s_c = [_dot(self.q[:, sl], self.c_ref[0, 0, hh].astype(BF16)) + self.bias_c[hh]
                    for hh, sl in enumerate(self.heads)]
        self.s_n = [jnp.where(valid_n, _dot(self.q[:, sl], self.knew[sl, :].astype(BF16)) + self.bias_n[hh], NEG_INF)
                    for hh, sl in enumerate(self.heads)]

    def softmax(self):
        ms = [jnp.maximum(jnp.max(a, axis=-1, keepdims=True), jnp.max(c, axis=-1, keepdims=True))
              for a, c in zip(self.s_c, self.s_n)]
        e_c = [jnp.exp(a - m) for a, m in zip(self.s_c, ms)]
        e_n = [jnp.exp(c - m) for c, m in zip(self.s_n, ms)]
        ls = [jnp.sum(a, axis=-1, keepdims=True) + jnp.sum(c, axis=-1, keepdims=True) for a, c in zip(e_c, e_n)]
        self.p_c = [(e / l).astype(BF16) for e, l in zip(e_c, ls)]
        self.p_n = [(e / l).astype(BF16) for e, l in zip(e_n, ls)]
        self.lse = jnp.zeros((self.nt, LANES), F32)
        for hh, (m, l) in enumerate(zip(ms, ls)):
            self.lse = jnp.where(self.lane == hh, m + jnp.log(l), self.lse)

    def output(self):
        os_ = [_dot_nt(self.c_ref[0, 1, hh].astype(BF16), self.p_c[hh])
               + _dot_nt(self.vnew[sl, :].astype(BF16), self.p_n[hh]) for hh, sl in enumerate(self.heads)]
        return jnp.concatenate(os_, axis=0), self.lse

    def shift(self):
        n, nt = self.c_ref.shape[-1], self.nt
        tail = lax.broadcasted_iota(jnp.int32, (HEAD_DIM, LANES), 1) >= LANES - nt
        lanes_to_tail = LANES - nt - self.b * nt
        for hh, sl in enumerate(self.heads):
            for kv, new in enumerate((self.knew, self.vnew)):
                rolled = pltpu.roll(self.c_ref[0, kv, hh], n - nt, axis=1)
                placed = pltpu.roll(new[sl, :], lanes_to_tail, axis=1)
                if n > LANES:
                    self.cout_ref[0, kv, hh, :, 0:n - LANES] = rolled[:, 0:n - LANES]
                self.cout_ref[0, kv, hh, :, n - LANES:n] = jnp.where(tail, placed, rolled[:, n - LANES:n])


def _ffn_attn_s_body(x_ref, g_ref, wup_ref, wdn_ref, q_ref, kvt_ref, *refs, groups, nt, hc, f_chunk, n_chunks):
    ng = len(groups)
    c_refs, bc_refs, bn_refs = refs[:ng], refs[ng:2 * ng], refs[2 * ng:3 * ng]
    y_ref = refs[3 * ng]
    o_refs, lse_refs, cout_refs = (refs[3 * ng + 1 + k * ng:3 * ng + 1 + (k + 1) * ng] for k in range(3))
    h_ref, acc_ref = refs[-2:]
    i, j = pl.program_id(0), pl.program_id(1)

    @pl.when(j == 0)
    def _():
        h_ref[...] = _rms(x_ref[...], g_ref[2:3]).astype(BF16)
        acc_ref[...] = jnp.zeros_like(acc_ref)

    cw = hc * HEAD_DIM
    e = cw * n_chunks
    attn = []
    for g, (dil, n_steps) in enumerate(groups):
        knew = kvt_ref[pl.ds(pl.multiple_of((2 * g) * e + j * cw, cw), cw), :]
        vnew = kvt_ref[pl.ds(pl.multiple_of((2 * g + 1) * e + j * cw, cw), cw), :]
        bias_c = [bc_refs[g][j * hc + hh] for hh in range(hc)]
        bias_n = [bn_refs[g][j * hc + hh] for hh in range(hc)]
        attn.append(_WindowAttnChunk(i, q_ref[i, g, j], knew, vnew, c_refs[g], bias_c, bias_n, cout_refs[g],
                                     dil=dil, n_steps=n_steps, nt=nt, hc=hc))
    hidden = pl.ds(pl.multiple_of(j * f_chunk, f_chunk), f_chunk)
    up = jnp.maximum(_dot(h_ref[...], wup_ref[0, :, hidden]), 0.0)
    for a in attn:
        a.logits()
    acc_ref[...] += _dot((up * up).astype(BF16), wdn_ref[0, hidden, :])
    for a in attn:
        a.softmax()
    for g, a in enumerate(attn):
        o_refs[g][0, j], lse_refs[g][0, j] = a.output()
    for a in attn:
        a.shift()

    @pl.when(j == n_chunks - 1)
    def _():
        y_ref[...] = x_ref[...] + _rms(acc_ref[...], g_ref[3:4])


def _ffn_attn_s(x, gains, w_up, w_down, layer, q, kvt_new, caches, bias_cs, bias_ns, groups, nt, tm):
    n, d = x.shape
    n_chunks, f_chunk = q.shape[2], w_up.shape[2] // q.shape[2]
    b, _, n_heads, _, _ = caches[0].shape
    hc = n_heads // n_chunks
    cw = hc * HEAD_DIM
    assert n // tm == b and kvt_new.shape[1] == LANES == b * nt and hc * n_chunks == n_heads
    assert all(c.shape[-1] == dil * n_steps for c, (dil, n_steps) in zip(caches, groups))
    row = lambda i, j: (i, 0)
    win = lambda i, j: (i, 0, j, 0, 0)
    per_row = lambda i, j: (i, 0, 0, 0)
    win_specs = [pl.BlockSpec((1, 2, hc, HEAD_DIM, c.shape[-1]), win) for c in caches]
    ng = len(caches)
    outs = pl.pallas_call(
        functools.partial(_ffn_attn_s_body, groups=groups, nt=nt, hc=hc, f_chunk=f_chunk, n_chunks=n_chunks),
        grid=(b, n_chunks),
        in_specs=[pl.BlockSpec((tm, d), row), _resident(gains.shape), _layer_weights(w_up, layer),
                  _layer_weights(w_down, layer), _resident(q.shape), _resident(kvt_new.shape)]
                 + win_specs + [_resident(a.shape) for a in bias_cs] + [_resident(a.shape) for a in bias_ns],
        out_specs=[pl.BlockSpec((tm, d), row)]
                  + [pl.BlockSpec((1, n_chunks, cw, nt), per_row)] * ng
                  + [pl.BlockSpec((1, n_chunks, nt, LANES), per_row)] * ng + win_specs,
        out_shape=[jax.ShapeDtypeStruct((n, d), F32)]
                  + [jax.ShapeDtypeStruct((b, n_chunks, cw, nt), F32)] * ng
                  + [jax.ShapeDtypeStruct((b, n_chunks, nt, LANES), F32)] * ng
                  + [jax.ShapeDtypeStruct(c.shape, F32) for c in caches],
        scratch_shapes=[pltpu.VMEM((tm, d), BF16), pltpu.VMEM((tm, d), F32)],
        compiler_params=_params("parallel", "arbitrary"),
        name="ffn_with_sample_attn",
    )(x, gains, w_up, w_down, q, kvt_new, *caches, *bias_cs, *bias_ns)
    return outs[0], outs[1:1 + ng], outs[1 + ng:1 + 2 * ng], outs[1 + 2 * ng:]


def _conv_tail(y, x, lng_ref, lnb_ref, w2_ref, b2_ref, g_ref):
    mu = jnp.mean(y, axis=-1, keepdims=True)
    yc = y - mu
    var = jnp.mean(yc * yc, axis=-1, keepdims=True)
    yn = yc * lax.rsqrt(var + EPS) * lng_ref[...] + lnb_ref[...]
    z = _dot((yn * jax.nn.sigmoid(yn)).astype(BF16), w2_ref[...]) + b2_ref[...]
    return x + _rms(z, g_ref[1:2])


def _glu(x, g_ref, w1_ref, b1_ref):
    d = x.shape[1]
    a = _dot(_rms(x, g_ref[0:1]).astype(BF16), w1_ref[...]) + b1_ref[...]
    return a[:, :d] * jax.nn.sigmoid(a[:, d:])


def _conv_p_body(x_ref, g_ref, w1_ref, b1_ref, wdw_ref, bdw_ref, lng_ref, lnb_ref, w2_ref, b2_ref,
                 o_ref, ul_ref, ush_ref, y_ref, *, ts):
    j = pl.program_id(1)
    n_cb = y_ref.shape[0]
    rows_ext = CARRY_ROWS + ts

    @pl.when(j == 0)
    def _():
        ush_ref[0, :, 0:CARRY_ROWS, :] = jnp.zeros((n_cb, CARRY_ROWS, LANES), F32)

    x = x_ref[0]
    u = _glu(x, g_ref, w1_ref, b1_ref)
    for cb in range(n_cb):
        ush_ref[0, cb, CARRY_ROWS:, :] = u[:, cb * LANES:(cb + 1) * LANES]
    n_sh = rows_ext - SUBLANES
    lead = CARRY_ROWS - (CONV_WIDTH - 1)

    def column_block(cb, carry):
        for m in range(1, SUBLANES):
            for r0 in range(0, n_sh, CONV_COPY_ROWS):
                r1 = min(r0 + CONV_COPY_ROWS, n_sh)
                ush_ref[m, cb, r0:r1, :] = ush_ref[0, cb, r0 + m:r1 + m, :]

        def rows(c, carry):
            r0 = pl.multiple_of(c * CONV_CHUNK_ROWS, CONV_CHUNK_ROWS)
            acc = jnp.broadcast_to(bdw_ref[cb], (CONV_CHUNK_ROWS, LANES))
            for k in range(CONV_WIDTH):
                a, m = divmod(lead + k, SUBLANES)
                acc = acc + wdw_ref[cb, k:k + 1, :] * ush_ref[m, cb, pl.ds(r0 + a * SUBLANES, CONV_CHUNK_ROWS), :]
            y_ref[cb, pl.ds(r0, CONV_CHUNK_ROWS), :] = acc
            return carry

        return lax.fori_loop(0, ts // CONV_CHUNK_ROWS, rows, carry)

    lax.fori_loop(0, n_cb, column_block, 0)
    y = jnp.concatenate([y_ref[cb] for cb in range(n_cb)], axis=1)
    o_ref[0] = _conv_tail(y, x, lng_ref, lnb_ref, w2_ref, b2_ref, g_ref)
    last = [ush_ref[0, cb, ts:ts + CARRY_ROWS, :] for cb in range(n_cb)]
    for cb in range(n_cb):
        ush_ref[0, cb, 0:CARRY_ROWS, :] = last[cb]
    ul_ref[0] = jnp.concatenate(last, axis=1)


def _conv_p(x, gains, w1, b1, wdw, bdw, lng, lnb, w2, b2, ts):
    b, s, d = x.shape
    n_cb = d // LANES
    wdw_cb = wdw.reshape(wdw.shape[0], n_cb, LANES).transpose(1, 0, 2)
    bdw_cb = bdw.reshape(n_cb, 1, LANES)
    consts = (gains, w1, b1, wdw_cb, bdw_cb, lng, lnb, w2, b2)
    return pl.pallas_call(
        functools.partial(_conv_p_body, ts=ts),
        grid=(b, s // ts),
        in_specs=[pl.BlockSpec((1, ts, d), lambda i, j: (i, j, 0))] + [_resident(a.shape) for a in consts],
        out_specs=[pl.BlockSpec((1, ts, d), lambda i, j: (i, j, 0)),
                   pl.BlockSpec((1, CARRY_ROWS, d), lambda i, j: (i, 0, 0))],
        out_shape=[jax.ShapeDtypeStruct((b, s, d), F32), jax.ShapeDtypeStruct((b, CARRY_ROWS, d), F32)],
        scratch_shapes=[pltpu.VMEM((SUBLANES, n_cb, CARRY_ROWS + ts, LANES), F32), pltpu.VMEM((n_cb, ts, LANES), F32)],
        compiler_params=_params("parallel", "arbitrary"),
        name="conv_prompt",
    )(x, *consts)


def _conv_s_body(x_ref, st_ref, g_ref, w1_ref, b1_ref, wdw_ref, bdw_ref, lng_ref, lnb_ref, w2_ref, b2_ref,
                 o_ref, ns_ref, *, nb, nt):
    x = x_ref[...]
    u = _glu(x, g_ref, w1_ref, b1_ref)
    n_prev = CONV_WIDTH - 1
    ext = [st_ref[i * nb:(i + 1) * nb, :] for i in range(n_prev)]
    ext += [u[t * nb:(t + 1) * nb, :] for t in range(nt)]
    ys = []
    for t in range(nt):
        y = None
        for k in range(CONV_WIDTH):
            term = wdw_ref[k:k + 1, :] * ext[t + k]
            y = term if y is None else y + term
        ys.append(y + bdw_ref[...])
    o_ref[...] = _conv_tail(jnp.concatenate(ys, axis=0), x, lng_ref, lnb_ref, w2_ref, b2_ref, g_ref)
    for i in range(n_prev):
        ns_ref[i * nb:(i + 1) * nb, :] = ext[nt + i]


def _conv_s(x, state, gains, w1, b1, wdw, bdw, lng, lnb, w2, b2, nb, nt):
    return pl.pallas_call(
        functools.partial(_conv_s_body, nb=nb, nt=nt),
        out_shape=[jax.ShapeDtypeStruct(x.shape, F32), jax.ShapeDtypeStruct(state.shape, F32)],
        compiler_params=pltpu.CompilerParams(vmem_limit_bytes=VMEM_LIMIT),
        name="conv_sample",
    )(x, state, gains, w1, b1, wdw, bdw, lng, lnb, w2, b2)


def _t5_bucket(dist):
    max_exact = N_BUCKETS // 2
    df = jnp.maximum(dist, 1).astype(F32)
    large = max_exact + (jnp.log(df / max_exact) / math.log(MAX_DISTANCE / max_exact)
                         * (N_BUCKETS - max_exact)).astype(jnp.int32)
    large = jnp.minimum(large, N_BUCKETS - 1)
    return jnp.where(dist < max_exact, dist, large)


def _bias_by_step(rel_bias, g, dil, n_steps, n_heads):
    tab = rel_bias[:, g * n_heads:(g + 1) * n_heads]
    bucket = _t5_bucket(dil * jnp.arange(n_steps + 1, dtype=jnp.int32))
    onehot = (bucket[:, None] == jnp.arange(N_BUCKETS, dtype=jnp.int32)[None, :]).astype(F32)
    return jnp.dot(onehot, tab, precision=lax.Precision.HIGHEST)


def _band_bias(by_step, n_steps, nk):
    off = nk - QB
    period = 2 * nk
    heads = by_step.shape[1]
    base = jnp.concatenate([by_step[::-1].T, jnp.full((heads, period - n_steps - 1), NEG_INF, F32)], axis=1)
    w = jnp.roll(base, off - n_steps, axis=1)
    r = period - 1
    flat = jnp.tile(w, (1, -(-(QB * r) // period)))[:, :QB * r]
    band = flat.reshape(heads, QB, r)[:, :, :nk]
    if off > 0:
        hidden = (np.arange(nk) < off)[None, None, :]
        band = jnp.stack([band, jnp.where(hidden, NEG_INF, band)])
    else:
        band = band[None]
    per = LANES // HEAD_DIM
    return band.reshape(band.shape[0], heads // per, per * QB, nk)


def _window_bias(by_step, dil, n_steps, nt):
    base = jnp.repeat(by_step[n_steps:0:-1].T, dil, axis=1)
    bias = jnp.stack([jnp.roll(base, t, axis=1) for t in range(nt)], axis=1)
    n = dil * n_steps
    delta = n + np.arange(nt)[:, None] - np.arange(n)[None, :]
    attended = (delta % dil == 0) & (delta <= dil * n_steps)
    return jnp.where(attended[None], bias, NEG_INF)


def _new_key_bias(by_step, dil, n_steps, nt):
    zero = jnp.zeros_like(by_step[0])
    rows = []
    for t in range(nt):
        cols = []
        for tp in range(nt):
            s, rem = divmod(t - tp, dil)
            cols.append(by_step[s] if (t >= tp and rem == 0 and s <= n_steps) else zero)
        rows.append(jnp.stack(cols, axis=1))
    return jnp.tile(jnp.stack(rows, axis=1), (1, 1, LANES // nt))


def _time_major(a):
    b, t, d = a.shape
    return a.transpose(1, 0, 2).reshape(t * b, d)


def _batch_major(a, b):
    return a.reshape(-1, b, a.shape[1]).transpose(1, 0, 2)


def _kv_window_layout(c):
    return c.transpose(0, 2, 3, 4, 1)


def _kv_window_unlayout(c):
    return c.transpose(0, 4, 1, 2, 3)


def kernel(x_prompt, x_sample, state_pool, cache_kv_g0, cache_kv_g1, cache_kv_g2, state_conv, norm_gains, rel_bias, pool_w, pool_scale, attn_w_qkv, attn_w_o, conv_w_pw1, conv_b_pw1, conv_w_dw, conv_b_dw, conv_ln_g, conv_ln_b, conv_w_pw2, conv_b_pw2, ffn_w_up, ffn_w_down):
    batch, seq, d = x_prompt.shape
    dec_b, dec_t, _ = x_sample.shape
    depth = norm_gains.shape[0]
    n_heads = attn_w_o.shape[1] // HEAD_DIM
    e_dim = n_heads * HEAD_DIM
    n_groups = len(ATTN_GROUPS)
    dils = tuple(dil for _, dil in ATTN_GROUPS)
    caches_in = (cache_kv_g0, cache_kv_g1, cache_kv_g2)
    n_tok = batch * seq
    n_dec = dec_b * dec_t

    xp = x_prompt
    xs = _time_major(x_sample)

    expand = jnp.asarray(np.kron(np.eye(LANES, n_heads), np.ones((1, HEAD_DIM))), BF16)

    w_up, w_down = ffn_w_up.astype(BF16), ffn_w_down.astype(BF16)

    pool_p, pool_s, conv_p, conv_s = [], [], [], []
    kv_p = [[] for _ in ATTN_GROUPS]
    kv_s = [[] for _ in ATTN_GROUPS]
    for i in range(depth):
        kind, j = i % 3, i // 3
        gains = norm_gains[i]
        if kind == 0:
            pw, ps = pool_w[j].astype(BF16), pool_scale[j][None, :]
            xp, h_last = _pool_p(xp, gains, pw, ps, w_up, w_down, i, ts=512)
            pool_p.append(h_last[:, POOL_CARRY - POOL_STATE:])
            xs, new_state = _pool_s(xs, _time_major(state_pool[j]), gains, pw, ps,
                                    nb=dec_b, nt=dec_t, pos0=PAST_LEN)
            pool_s.append(_batch_major(new_state, dec_b))
            xs = _ffn(xs, gains, w_up, w_down, i, tm=n_dec)
            continue
        elif kind == 1:
            w_qkv = attn_w_qkv[j].astype(BF16)
            w_o = attn_w_o[j].astype(BF16)
            by_step = [_bias_by_step(rel_bias, g, dil, win // dil, n_heads)
                       for g, (win, dil) in enumerate(ATTN_GROUPS)]
            keeps = [min(win, seq) for win, _ in ATTN_GROUPS]
            qkv, kvts = [], []
            for lo, hi in QKV_GROUP_SPLIT:
                outs = _qkv_p(xp, gains, w_qkv, lo, dils[lo:hi], keeps[lo:hi], e_dim, tm=512)
                qkv += outs[:3 * (hi - lo)]
                kvts += outs[3 * (hi - lo):]
            os_, ms, ls = [], [], []
            for g, (win, dil) in enumerate(ATTN_GROUPS):
                n_steps = win // dil
                nk = 2 * QB if seq // dil > QB else QB
                bias = _band_bias(by_step[g], n_steps, nk)
                o, m, l = _attn_p(qkv[3 * g], qkv[3 * g + 1], qkv[3 * g + 2], bias, g, n_heads)
                os_.append(o)
                ms.append(m)
                ls.append(l)
                kv_p[g].append(_kv_window_unlayout(kvts[g].reshape(batch, 2, n_heads, HEAD_DIM, keeps[g])))
            xp = _merge(xp, os_, ms, ls, expand, w_o, gains, tm=512)
            xs_b = _batch_major(xs, dec_b)
            xs_flat = xs_b.reshape(n_dec, d)
            n_chunks = ffn_w_up.shape[2] // FFN_CHUNK
            hc = n_heads // n_chunks
            q_s, kvt_new = _qkv_s(xs_flat, gains, w_qkv, n_groups)
            q_s = q_s.reshape(dec_b, dec_t, n_groups, n_chunks, hc * HEAD_DIM)
            q_s = q_s.transpose(0, 2, 3, 1, 4)
            groups = tuple((dil, win // dil) for win, dil in ATTN_GROUPS)
            caches = [_kv_window_layout(c[j]) for c in caches_in]
            bias_cs = [_window_bias(by_step[g], dil, n_steps, dec_t) for g, (dil, n_steps) in enumerate(groups)]
            bias_ns = [_new_key_bias(by_step[g], dil, n_steps, dec_t) for g, (dil, n_steps) in enumerate(groups)]
            xp, os_, lses, new_caches = _ffn_attn_s(xp.reshape(n_tok, d), gains, w_up, w_down, i, q_s, kvt_new,
                                                    caches, bias_cs, bias_ns, groups, dec_t, tm=n_tok // dec_b)
            xp = xp.reshape(batch, seq, d)
            ms, ls = [], []
            for g in range(n_groups):
                kv_s[g].append(_kv_window_unlayout(new_caches[g]))
                lse = lses[g][..., :hc].transpose(0, 2, 1, 3).reshape(1, 1, n_dec, n_heads)
                ms.append(jnp.pad(lse, ((0, 0), (0, 0), (0, 0), (0, LANES - n_heads))))
                ls.append(jnp.ones((1, 1, n_dec, LANES), F32))
            os_ = [o.transpose(0, 3, 1, 2).reshape(1, 1, n_dec, e_dim) for o in os_]
            xs_flat = _merge(xs_flat[None], os_, ms, ls, expand, w_o, gains, tm=n_dec)[0]
            xs = _time_major(xs_flat.reshape(dec_b, dec_t, d))
            xs = _ffn(xs, gains, w_up, w_down, i, tm=n_dec)
            continue
        else:
            cw = (conv_w_pw1[j].astype(BF16), conv_b_pw1[j][None, :], conv_w_dw[j], conv_b_dw[j][None, :],
                  conv_ln_g[j][None, :], conv_ln_b[j][None, :], conv_w_pw2[j].astype(BF16), conv_b_pw2[j][None, :])
            xp, u_last = _conv_p(xp, gains, *cw, ts=512)
            conv_p.append(u_last[:, CARRY_ROWS - (CONV_WIDTH - 1):])
            xs, new_state = _conv_s(xs, _time_major(state_conv[j]), gains, *cw, nb=dec_b, nt=dec_t)
            conv_s.append(_batch_major(new_state, dec_b))
        xp = _ffn(xp.reshape(n_tok, d), gains, w_up, w_down, i, tm=1024).reshape(batch, seq, d)
        xs = _ffn(xs, gains, w_up, w_down, i, tm=n_dec)
    return (xp, _batch_major(xs, dec_b),
            jnp.stack(pool_p), jnp.stack(pool_s),
            jnp.stack(kv_p[0]), jnp.stack(kv_s[0]),
            jnp.stack(kv_p[1]), jnp.stack(kv_s[1]),
            jnp.stack(kv_p[2]), jnp.stack(kv_s[2]),
            jnp.stack(conv_p), jnp.stack(conv_s))
```

```python
import functools
import math

import jax
import jax.numpy as jnp
import numpy as np
from jax import lax
from jax.experimental import pallas as pl
from jax.experimental.pallas import tpu as pltpu

F32 = jnp.float32
BF16 = jnp.bfloat16

EPS = 1e-6
NEG_INF = -1e30
POOL_WINDOWS = (2, 4, 8, 16)
POOL_STATE = max(POOL_WINDOWS) - 1
PAST_LEN = 8192
ATTN_GROUPS = ((128, 1), (512, 4), (2048, 16))
HEAD_DIM = 64
QB = 128
ATTN_BLOCKS_PER_STEP = 4
ATTN_AHEAD = 4
N_BUCKETS = 32
MAX_DISTANCE = 2048
CONV_WIDTH = 31
CARRY_ROWS = 32
POOL_CARRY = 16
LANES = 128
SUBLANES = 8
CONV_CHUNK_ROWS = 128
CONV_COPY_ROWS = 128

VMEM_LIMIT = 60 * 1024 * 1024
QKV_GROUP_SPLIT = ((0, 2), (2, 3))
MLP_CHUNK = 1024
FFN_CHUNK = 1024

NT_DIMS = (((1,), (1,)), ((), ()))


def _params(*sem):
    return pltpu.CompilerParams(dimension_semantics=sem, vmem_limit_bytes=VMEM_LIMIT)


def _resident(shape):
    nd = len(shape)
    return pl.BlockSpec(shape, lambda *_: (0,) * nd, pipeline_mode=pl.Buffered(1))


def _rms(x, g):
    return x * lax.rsqrt(jnp.mean(x * x, axis=-1, keepdims=True) + EPS) * g


def _dot(a, b):
    return jnp.dot(a, b, preferred_element_type=F32)


def _dot_nt(a, b):
    return lax.dot_general(a, b, NT_DIMS, preferred_element_type=F32)


def _mlp(x, g_ref, wup_ref, wdn_ref):
    h = _rms(x, g_ref[2:3]).astype(BF16)
    acc = jnp.zeros_like(x)
    for c in range(wup_ref.shape[2] // MLP_CHUNK):
        cols = slice(c * MLP_CHUNK, (c + 1) * MLP_CHUNK)
        a = jnp.maximum(_dot(h, wup_ref[0, :, cols]), 0.0)
        acc = acc + _dot((a * a).astype(BF16), wdn_ref[0, cols, :])
    return x + _rms(acc, g_ref[3:4])


def _ffn_body(x_ref, g_ref, wup_ref, wdn_ref, o_ref):
    o_ref[...] = _mlp(x_ref[...], g_ref, wup_ref, wdn_ref)


def _layer_weights(w, layer):
    nd = w.ndim - 1
    return pl.BlockSpec((1,) + w.shape[1:], lambda *_: (layer,) + (0,) * nd, pipeline_mode=pl.Buffered(1))


def _ffn(x, gains, w_up, w_down, layer, tm):
    n, d = x.shape
    return pl.pallas_call(
        _ffn_body,
        grid=(n // tm,),
        in_specs=[pl.BlockSpec((tm, d), lambda i: (i, 0)), _resident(gains.shape),
                  _layer_weights(w_up, layer), _layer_weights(w_down, layer)],
        out_specs=pl.BlockSpec((tm, d), lambda i: (i, 0)),
        out_shape=jax.ShapeDtypeStruct((n, d), F32),
        compiler_params=_params("parallel"),
        name="ffn",
    )(x, gains, w_up, w_down)


def _pool_p_body(x_ref, g_ref, pw_ref, ps_ref, wup_ref, wdn_ref, o_ref, hl_ref, carry_ref, *, ts):
    j = pl.program_id(1)

    @pl.when(j == 0)
    def _():
        carry_ref[...] = jnp.zeros_like(carry_ref)

    x = x_ref[0]
    h = _rms(x, g_ref[0:1])
    ext = jnp.concatenate([carry_ref[...], h], axis=0)
    gw = pw_ref.shape[1]
    assert POOL_WINDOWS == tuple(2 << gi for gi in range(len(POOL_WINDOWS)))
    sums = []
    s = ext
    for gi in range(len(POOL_WINDOWS)):
        s = s + pltpu.roll(s, 1 << gi, axis=0)
        sums.append(s[:, :gw])
        if gi + 1 < len(POOL_WINDOWS):
            s = s[:, gw:]
    pos = j * ts + lax.broadcasted_iota(jnp.int32, (ts, 1), 0)
    ys = []
    for gi, (w, s) in enumerate(zip(POOL_WINDOWS, sums)):
        cols = slice(gi * gw, (gi + 1) * gw)
        inv_cnt = 1.0 / jnp.minimum(pos + 1, w).astype(F32)
        pooled = s[POOL_CARRY:, :] * inv_cnt - h[:, cols]
        ys.append(_dot(pooled.astype(BF16), pw_ref[gi]))
    y = jnp.concatenate(ys, axis=1) * ps_ref[...]
    carry_ref[...] = h[ts - POOL_CARRY:, :]
    hl_ref[0] = h[ts - POOL_CARRY:, :]
    o_ref[0] = _mlp(x + _rms(y, g_ref[1:2]), g_ref, wup_ref, wdn_ref)


def _pool_p(x, gains, pw, ps, w_up, w_down, layer, ts):
    b, s, d = x.shape
    return pl.pallas_call(
        functools.partial(_pool_p_body, ts=ts),
        grid=(b, s // ts),
        in_specs=[pl.BlockSpec((1, ts, d), lambda i, j: (i, j, 0)), _resident(gains.shape),
                  _resident(pw.shape), _resident(ps.shape),
                  _layer_weights(w_up, layer), _layer_weights(w_down, layer)],
        out_specs=[pl.BlockSpec((1, ts, d), lambda i, j: (i, j, 0)),
                   pl.BlockSpec((1, POOL_CARRY, d), lambda i, j: (i, 0, 0))],
        out_shape=[jax.ShapeDtypeStruct((b, s, d), F32),
                   jax.ShapeDtypeStruct((b, POOL_CARRY, d), F32)],
        scratch_shapes=[pltpu.VMEM((POOL_CARRY, d), F32)],
        compiler_params=_params("parallel", "arbitrary"),
        name="pool_mlp_prompt",
    )(x, gains, pw, ps, w_up, w_down)


def _pool_s_body(x_ref, st_ref, g_ref, pw_ref, ps_ref, o_ref, ns_ref, *, nb, nt, pos0):
    x = x_ref[...]
    h = _rms(x, g_ref[0:1])
    ext = [st_ref[i * nb:(i + 1) * nb, :] for i in range(POOL_STATE)]
    ext += [h[t * nb:(t + 1) * nb, :] for t in range(nt)]
    gw = pw_ref.shape[1]
    for t in range(nt):
        ys = []
        for gi, w in enumerate(POOL_WINDOWS):
            cols = slice(gi * gw, (gi + 1) * gw)
            end = POOL_STATE + t
            acc = ext[end][:, cols]
            for i in range(1, min(w, end + 1)):
                acc = acc + ext[end - i][:, cols]
            pooled = acc / float(min(pos0 + t + 1, w)) - ext[end][:, cols]
            ys.append(_dot(pooled.astype(BF16), pw_ref[gi]))
        y = jnp.concatenate(ys, axis=1) * ps_ref[...]
        o_ref[t * nb:(t + 1) * nb, :] = x[t * nb:(t + 1) * nb, :] + _rms(y, g_ref[1:2])
    for i in range(POOL_STATE):
        ns_ref[i * nb:(i + 1) * nb, :] = ext[nt + i]


def _pool_s(x, state, gains, pw, ps, nb, nt, pos0):
    return pl.pallas_call(
        functools.partial(_pool_s_body, nb=nb, nt=nt, pos0=pos0),
        out_shape=[jax.ShapeDtypeStruct(x.shape, F32), jax.ShapeDtypeStruct(state.shape, F32)],
        compiler_params=pltpu.CompilerParams(vmem_limit_bytes=VMEM_LIMIT),
        name="pool_sample",
    )(x, state, gains, pw, ps)


def _kept_tiles(seq, keep, tm):
    rows = min(keep, tm)
    tiles = keep // rows
    return seq // tm - tiles, tiles, rows


def _qkv_p_body(x_ref, g_ref, w_ref, *refs, dils, keeps, seq, tm):
    ng = len(dils)
    out_refs, kvt_refs, ybuf_ref = refs[:3 * ng], refs[3 * ng:4 * ng], refs[-1]
    e = out_refs[0].shape[-1]
    j = pl.program_id(1)
    h = _rms(x_ref[0], g_ref[0:1]).astype(BF16)

    for g, dil in enumerate(dils):
        first, _, rows = _kept_tiles(seq, keeps[g], tm)
        for c in range(3):
            idx = g * 3 + c
            y = _dot(h, w_ref[:, idx * e:(idx + 1) * e])
            if c == 0:
                y = y * (HEAD_DIM ** -0.5)
            else:
                def keep_window(y=y, g=g, c=c, rows=rows):
                    kvt_refs[g][0, (c - 1) * e:c * e, :] = y[tm - rows:, :].T

                if first == 0:
                    keep_window()
                else:
                    pl.when(j >= first)(keep_window)
            if dil == 1:
                out_refs[idx][0, 0] = y.astype(BF16)
            else:
                slot = idx % 2
                for cc in range(e // LANES):
                    ybuf_ref[slot, cc] = y[:, cc * LANES:(cc + 1) * LANES]
                for r in range(dil):
                    out_refs[idx][0, r] = jnp.concatenate(
                        [ybuf_ref[slot, cc, pl.ds(r, tm // dil, stride=dil), :] for cc in range(e // LANES)],
                        axis=1).astype(BF16)


def _qkv_p(x, gains, w, first_group, dils, keeps, e, tm):
    b, s, d = x.shape
    width = len(dils) * 3 * e
    assert (first_group * 3 * e) % width == 0
    w_spec = pl.BlockSpec((d, width), lambda i, j: (0, first_group * 3 * e // width), pipeline_mode=pl.Buffered(1))
    out_specs, out_shape = [], []
    for dil in dils:
        for _ in range(3):
            out_specs.append(pl.BlockSpec((1, dil, tm // dil, e), lambda i, j: (i, 0, j, 0)))
            out_shape.append(jax.ShapeDtypeStruct((b, dil, s // dil, e), BF16))
    for keep in keeps:
        first, _, rows = _kept_tiles(s, keep, tm)
        out_specs.append(pl.BlockSpec((1, 2 * e, rows), lambda i, j, first=first: (i, 0, jnp.maximum(j - first, 0))))
        out_shape.append(jax.ShapeDtypeStruct((b, 2 * e, keep), F32))
    return pl.pallas_call(
        functools.partial(_qkv_p_body, dils=dils, keeps=tuple(keeps), seq=s, tm=tm),
        grid=(b, s // tm),
        in_specs=[pl.BlockSpec((1, tm, d), lambda i, j: (i, j, 0)), _resident(gains.shape), w_spec],
        out_specs=out_specs,
        out_shape=out_shape,
        scratch_shapes=[pltpu.VMEM((2, e // LANES, tm, LANES), F32)],
        compiler_params=_params("parallel", "arbitrary"),
        name="qkv_prompt",
    )(x, gains, w)


def _qkv_s_body(x_ref, g_ref, w_ref, q_ref, kvt_ref, *, n_groups):
    e = w_ref.shape[1] // (3 * n_groups)
    h = _rms(x_ref[...], g_ref[0:1]).astype(BF16)
    for g in range(n_groups):
        q_ref[:, g * e:(g + 1) * e] = _dot(h, w_ref[:, 3 * g * e:(3 * g + 1) * e]) * (HEAD_DIM ** -0.5)
        for c in (1, 2):
            kv = _dot(h, w_ref[:, (3 * g + c) * e:(3 * g + c + 1) * e])
            kvt_ref[(2 * g + c - 1) * e:(2 * g + c) * e, :] = kv.T


def _qkv_s(x, gains, w, n_groups):
    n = x.shape[0]
    e = w.shape[1] // (3 * n_groups)
    return pl.pallas_call(
        functools.partial(_qkv_s_body, n_groups=n_groups),
        out_shape=[jax.ShapeDtypeStruct((n, n_groups * e), F32), jax.ShapeDtypeStruct((n_groups * 2 * e, n), F32)],
        compiler_params=pltpu.CompilerParams(vmem_limit_bytes=VMEM_LIMIT),
        name="qkv_sample",
    )(x, gains, w)


def _attn_p_body(*refs, has_prev, n_heads):
    if has_prev:
        q_ref, kp_ref, kc_ref, vp_ref, vc_ref, bias_ref, ones_ref, o_ref, m_ref, l_ref = refs
        first_step = (pl.program_id(2) == 0).astype(jnp.int32)
    else:
        q_ref, kc_ref, vc_ref, bias_ref, ones_ref, o_ref, m_ref, l_ref = refs
    n_sub = q_ref.shape[2] // QB
    per = LANES // HEAD_DIM
    n_grp = n_heads // per
    lane = lax.broadcasted_iota(jnp.int32, (QB, LANES), 1)
    lane_head = lax.broadcasted_iota(jnp.int32, (1, LANES), 1) // HEAD_DIM

    def tiles(ref_prev, ref_cur, sub, cols):
        own = ref_cur[0, 0, sub * QB:(sub + 1) * QB, cols]
        if not has_prev:
            return [own]
        before = ref_prev[0, 0, :, cols] if sub == 0 else ref_cur[0, 0, (sub - 1) * QB:sub * QB, cols]
        return [before, own]

    def scores(item):
        sub, grp = item
        cols = slice(grp * LANES, (grp + 1) * LANES)
        q = q_ref[0, 0, sub * QB:(sub + 1) * QB, cols]
        qq = jnp.concatenate([q * jnp.where(lane_head == hh, 1.0, 0.0).astype(BF16) for hh in range(per)], axis=0)
        k = jnp.concatenate(tiles(kp_ref if has_prev else None, kc_ref, sub, cols), axis=0)
        first = first_step if (has_prev and sub == 0) else 0
        return _dot_nt(qq, k) + bias_ref[first, grp]

    items = [(sub, grp) for sub in range(n_sub) for grp in range(n_grp)]
    stats = [[jnp.zeros((QB, LANES), F32), jnp.ones((QB, LANES), F32)]
             for _ in range(n_sub)]
    queue = [scores(it) for it in items[:ATTN_AHEAD]]
    for idx, (sub, grp) in enumerate(items):
        s = queue.pop(0)
        if idx + ATTN_AHEAD < len(items):
            queue.append(scores(items[idx + ATTN_AHEAD]))
        cols = slice(grp * LANES, (grp + 1) * LANES)
        m = jnp.max(s, axis=-1, keepdims=True)
        e = jnp.exp(s - m).astype(BF16)
        v1 = jnp.concatenate([jnp.concatenate([v, ones_ref[...]], axis=1)
                              for v in tiles(vp_ref if has_prev else None, vc_ref, sub, cols)], axis=0)
        ol = _dot(e, v1)
        o, l = ol[:, :LANES], ol[:, LANES:]
        out = o[0:QB]
        for hh in range(per):
            rows = slice(hh * QB, (hh + 1) * QB)
            if hh > 0:
                out = jnp.where(lane_head == hh, o[rows], out)
            stats[sub][0] = jnp.where(lane == grp * per + hh, m[rows], stats[sub][0])
            stats[sub][1] = jnp.where(lane == grp * per + hh, l[rows], stats[sub][1])
        o_ref[0, 0, sub * QB:(sub + 1) * QB, cols] = out.astype(o_ref.dtype)
    for sub in range(n_sub):
        m_ref[0, 0, sub * QB:(sub + 1) * QB, :] = stats[sub][0]
        l_ref[0, 0, sub * QB:(sub + 1) * QB, :] = stats[sub][1]


def _attn_p(q, k, v, bias, g, n_heads):
    out_dims = q.shape[:3]
    has_prev = q.shape[2] > QB
    assert bias.shape[0] == (2 if has_prev else 1)
    if not has_prev:
        q, k, v = (a.reshape(a.shape[0], 1, a.shape[1] * a.shape[2], a.shape[3]) for a in (q, k, v))
    batch, dil, sub, e = q.shape
    nb = sub // QB
    ones = jnp.ones((QB, LANES), BF16)
    n_sub = ATTN_BLOCKS_PER_STEP if nb % ATTN_BLOCKS_PER_STEP == 0 else 1
    blk = (1, 1, n_sub * QB, e)
    cur = lambda b, r, n: (b, r, n, 0)
    prev = lambda b, r, n: (b, r, jnp.maximum(n * n_sub - 1, 0), 0)
    if has_prev:
        one = (1, 1, QB, e)
        in_specs = [pl.BlockSpec(blk, cur), pl.BlockSpec(one, prev), pl.BlockSpec(blk, cur),
                    pl.BlockSpec(one, prev), pl.BlockSpec(blk, cur)]
        args = (q, k, k, v, v)
    else:
        in_specs = [pl.BlockSpec(blk, cur)] * 3
        args = (q, k, v)
    outs = pl.pallas_call(
        functools.partial(_attn_p_body, has_prev=has_prev, n_heads=n_heads),
        grid=(batch, dil, nb // n_sub),
        in_specs=in_specs + [_resident(bias.shape), _resident(ones.shape)],
        out_specs=[pl.BlockSpec(blk, cur)] + [pl.BlockSpec((1, 1, n_sub * QB, LANES), cur)] * 2,
        out_shape=[jax.ShapeDtypeStruct((batch, dil, sub, e), BF16)]
                  + [jax.ShapeDtypeStruct((batch, dil, sub, LANES), F32)] * 2,
        compiler_params=_params("parallel", "parallel", "arbitrary"),
        name=f"attn_prompt_g{g}",
    )(*args, bias, ones)
    return [a.reshape(out_dims + a.shape[3:]) for a in outs]


def _expand_heads(w, e_ref):
    hi = w.astype(BF16)
    lo = (w - hi.astype(F32)).astype(BF16)
    return _dot(jnp.concatenate([hi, lo], axis=1), e_ref[...])


def _merge_groups(os_, ms, ls, e_ref):
    top = functools.reduce(jnp.maximum, ms)
    es = [jnp.exp(m - top) for m in ms]
    den = functools.reduce(lambda a, b: a + b, [e * l for e, l in zip(es, ls)])
    acc = None
    for o, e in zip(os_, es):
        term = _expand_heads(e / den, e_ref) * o
        acc = term if acc is None else acc + term
    return acc


def _merge_body(x_ref, *refs, dils, tm):
    ng = len(dils)
    o_refs, m_refs, l_refs = refs[:ng], refs[ng:2 * ng], refs[2 * ng:3 * ng]
    e_ref, wo_ref, g_ref, out_ref, obuf_ref, sbuf_ref = refs[3 * ng:]
    os_, ms, ls = [], [], []
    for gi, dil in enumerate(dils):
        if dil == 1:
            os_.append(o_refs[gi][0, 0].astype(F32))
            ms.append(m_refs[gi][0, 0])
            ls.append(l_refs[gi][0, 0])
        else:
            n_cc = obuf_ref.shape[1]
            for r in range(dil):
                rows = pl.ds(r, tm // dil, stride=dil)
                for cc in range(n_cc):
                    obuf_ref[gi, cc, rows, :] = o_refs[gi][0, r, :, cc * LANES:(cc + 1) * LANES].astype(F32)
                sbuf_ref[0, gi, rows, :] = m_refs[gi][0, r]
                sbuf_ref[1, gi, rows, :] = l_refs[gi][0, r]
            os_.append(jnp.concatenate([obuf_ref[gi, cc] for cc in range(n_cc)], axis=1))
            ms.append(sbuf_ref[0, gi])
            ls.append(sbuf_ref[1, gi])
    o = _merge_groups(os_, ms, ls, e_ref)
    y = _dot(o.astype(BF16), wo_ref[...])
    out_ref[0] = x_ref[0] + _rms(y, g_ref[1:2])


def _merge(x, os_, ms, ls, expand, w_o, gains, tm):
    b, s, d = x.shape
    dils = tuple(o.shape[1] for o in os_)
    e = os_[0].shape[-1]
    grp = lambda i, j: (i, 0, j, 0)
    return pl.pallas_call(
        functools.partial(_merge_body, dils=dils, tm=tm),
        grid=(b, s // tm),
        in_specs=[pl.BlockSpec((1, tm, d), lambda i, j: (i, j, 0))]
                 + [pl.BlockSpec((1, dil, tm // dil, e), grp) for dil in dils]
                 + [pl.BlockSpec((1, dil, tm // dil, LANES), grp) for dil in dils] * 2
                 + [_resident(expand.shape), _resident(w_o.shape), _resident(gains.shape)],
        out_specs=pl.BlockSpec((1, tm, d), lambda i, j: (i, j, 0)),
        out_shape=jax.ShapeDtypeStruct((b, s, d), F32),
        scratch_shapes=[pltpu.VMEM((len(dils), e // LANES, tm, LANES), F32),
                        pltpu.VMEM((2, len(dils), tm, LANES), F32)],
        compiler_params=_params("parallel", "parallel"),
        name="attn_merge",
    )(x, *os_, *ms, *ls, expand, w_o, gains)


class _WindowAttnChunk:
    def __init__(self, b, q, knew, vnew, c_ref, bias_c, bias_n, cout_ref, *, dil, n_steps, nt, hc):
        self.b, self.q, self.knew, self.vnew, self.c_ref, self.cout_ref = b, q.astype(BF16), knew, vnew, c_ref, cout_ref
        self.bias_c, self.bias_n, self.dil, self.n_steps, self.nt = bias_c, bias_n, dil, n_steps, nt
        self.heads = [slice(hh * HEAD_DIM, (hh + 1) * HEAD_DIM) for hh in range(hc)]
        self.lane = lax.broadcasted_iota(jnp.int32, (nt, LANES), 1)

    def logits(self):
        nt, dil = self.nt, self.dil
        dn = lax.broadcasted_iota(jnp.int32, (nt, LANES), 0) - lax.rem(self.lane, nt)
        valid_n = (self.lane // nt == self.b) & (dn >= 0) & (lax.rem(dn, dil) == 0) & (dn <= dil * self.n_steps)
        self.s_c = [_dot(self.q[:, sl], self.c_ref[0, 0, hh].astype(BF16)) + self.bias_c[hh]
                    for hh, sl in enumerate(self.heads)]
        self.s_n = [jnp.where(valid_n, _dot(self.q[:, sl], self.knew[sl, :].astype(BF16)) + self.bias_n[hh], NEG_INF)
                    for hh, sl in enumerate(self.heads)]

    def softmax(self):
        ms = [jnp.maximum(jnp.max(a, axis=-1, keepdims=True), jnp.max(c, axis=-1, keepdims=True))
              for a, c in zip(self.s_c, self.s_n)]
        e_c = [jnp.exp(a - m) for a, m in zip(self.s_c, ms)]
        e_n = [jnp.exp(c - m) for c, m in zip(self.s_n, ms)]
        ls = [jnp.sum(a, axis=-1, keepdims=True) + jnp.sum(c, axis=-1, keepdims=True) for a, c in zip(e_c, e_n)]
        self.p_c = [(e / l).astype(BF16) for e, l in zip(e_c, ls)]
        self.p_n = [(e / l).astype(BF16) for e, l in zip(e_n, ls)]
        self.lse = jnp.zeros((self.nt, LANES), F32)
        for hh, (m, l) in enumerate(zip(ms, ls)):
            self.lse = jnp.where(self.lane == hh, m + jnp.log(l), self.lse)

    def output(self):
        os_ = [_dot_nt(self.c_ref[0, 1, hh].astype(BF16), self.p_c[hh])
               + _dot_nt(self.vnew[sl, :].astype(BF16), self.p_n[hh]) for hh, sl in enumerate(self.heads)]
        return jnp.concatenate(os_, axis=0), self.lse

    def shift(self):
        n, nt = self.c_ref.shape[-1], self.nt
        tail = lax.broadcasted_iota(jnp.int32, (HEAD_DIM, LANES), 1) >= LANES - nt
        lanes_to_tail = LANES - nt - self.b * nt
        for hh, sl in enumerate(self.heads):
            for kv, new in enumerate((self.knew, self.vnew)):
                rolled = pltpu.roll(self.c_ref[0, kv, hh], n - nt, axis=1)
                placed = pltpu.roll(new[sl, :], lanes_to_tail, axis=1)
                if n > LANES:
                    self.cout_ref[0, kv, hh, :, 0:n - LANES] = rolled[:, 0:n - LANES]
                self.cout_ref[0, kv, hh, :, n - LANES:n] = jnp.where(tail, placed, rolled[:, n - LANES:n])


def _ffn_attn_s_body(x_ref, g_ref, wup_ref, wdn_ref, q_ref, kvt_ref, *refs, groups, nt, hc, f_chunk, n_chunks):
    ng = len(groups)
    c_refs, bc_refs, bn_refs = refs[:ng], refs[ng:2 * ng], refs[2 * ng:3 * ng]
    y_ref = refs[3 * ng]
    o_refs, lse_refs, cout_refs = (refs[3 * ng + 1 + k * ng:3 * ng + 1 + (k + 1) * ng] for k in range(3))
    h_ref, acc_ref = refs[-2:]
    i, j = pl.program_id(0), pl.program_id(1)

    @pl.when(j == 0)
    def _():
        h_ref[...] = _rms(x_ref[...], g_ref[2:3]).astype(BF16)
        acc_ref[...] = jnp.zeros_like(acc_ref)

    cw = hc * HEAD_DIM
    e = cw * n_chunks
    attn = []
    for g, (dil, n_steps) in enumerate(groups):
        knew = kvt_ref[pl.ds(pl.multiple_of((2 * g) * e + j * cw, cw), cw), :]
        vnew = kvt_ref[pl.ds(pl.multiple_of((2 * g + 1) * e + j * cw, cw), cw), :]
        bias_c = [bc_refs[g][j * hc + hh] for hh in range(hc)]
        bias_n = [bn_refs[g][j * hc + hh] for hh in range(hc)]
        attn.append(_WindowAttnChunk(i, q_ref[i, g, j], knew, vnew, c_refs[g], bias_c, bias_n, cout_refs[g],
                                     dil=dil, n_steps=n_steps, nt=nt, hc=hc))
    hidden = pl.ds(pl.multiple_of(j * f_chunk, f_chunk), f_chunk)
    up = jnp.maximum(_dot(h_ref[...], wup_ref[0, :, hidden]), 0.0)
    for a in attn:
        a.logits()
    acc_ref[...] += _dot((up * up).astype(BF16), wdn_ref[0, hidden, :])
    for a in attn:
        a.softmax()
    for g, a in enumerate(attn):
        o_refs[g][0, j], lse_refs[g][0, j] = a.output()
    for a in attn:
        a.shift()

    @pl.when(j == n_chunks - 1)
    def _():
        y_ref[...] = x_ref[...] + _rms(acc_ref[...], g_ref[3:4])


def _ffn_attn_s(x, gains, w_up, w_down, layer, q, kvt_new, caches, bias_cs, bias_ns, groups, nt, tm):
    n, d = x.shape
    n_chunks, f_chunk = q.shape[2], w_up.shape[2] // q.shape[2]
    b, _, n_heads, _, _ = caches[0].shape
    hc = n_heads // n_chunks
    cw = hc * HEAD_DIM
    assert n // tm == b and kvt_new.shape[1] == LANES == b * nt and hc * n_chunks == n_heads
    assert all(c.shape[-1] == dil * n_steps for c, (dil, n_steps) in zip(caches, groups))
    row = lambda i, j: (i, 0)
    win = lambda i, j: (i, 0, j, 0, 0)
    per_row = lambda i, j: (i, 0, 0, 0)
    win_specs = [pl.BlockSpec((1, 2, hc, HEAD_DIM, c.shape[-1]), win) for c in caches]
    ng = len(caches)
    outs = pl.pallas_call(
        functools.partial(_ffn_attn_s_body, groups=groups, nt=nt, hc=hc, f_chunk=f_chunk, n_chunks=n_chunks),
        grid=(b, n_chunks),
        in_specs=[pl.BlockSpec((tm, d), row), _resident(gains.shape), _layer_weights(w_up, layer),
                  _layer_weights(w_down, layer), _resident(q.shape), _resident(kvt_new.shape)]
                 + win_specs + [_resident(a.shape) for a in bias_cs] + [_resident(a.shape) for a in bias_ns],
        out_specs=[pl.BlockSpec((tm, d), row)]
                  + [pl.BlockSpec((1, n_chunks, cw, nt), per_row)] * ng
                  + [pl.BlockSpec((1, n_chunks, nt, LANES), per_row)] * ng + win_specs,
        out_shape=[jax.ShapeDtypeStruct((n, d), F32)]
                  + [jax.ShapeDtypeStruct((b, n_chunks, cw, nt), F32)] * ng
                  + [jax.ShapeDtypeStruct((b, n_chunks, nt, LANES), F32)] * ng
                  + [jax.ShapeDtypeStruct(c.shape, F32) for c in caches],
        scratch_shapes=[pltpu.VMEM((tm, d), BF16), pltpu.VMEM((tm, d), F32)],
        compiler_params=_params("parallel", "arbitrary"),
        name="ffn_with_sample_attn",
    )(x, gains, w_up, w_down, q, kvt_new, *caches, *bias_cs, *bias_ns)
    return outs[0], outs[1:1 + ng], outs[1 + ng:1 + 2 * ng], outs[1 + 2 * ng:]


def _conv_tail(y, x, lng_ref, lnb_ref, w2_ref, b2_ref, g_ref):
    mu = jnp.mean(y, axis=-1, keepdims=True)
    yc = y - mu
    var = jnp.mean(yc * yc, axis=-1, keepdims=True)
    yn = yc * lax.rsqrt(var + EPS) * lng_ref[...] + lnb_ref[...]
    z = _dot((yn * jax.nn.sigmoid(yn)).astype(BF16), w2_ref[...]) + b2_ref[...]
    return x + _rms(z, g_ref[1:2])


def _glu(x, g_ref, w1_ref, b1_ref):
    d = x.shape[1]
    a = _dot(_rms(x, g_ref[0:1]).astype(BF16), w1_ref[...]) + b1_ref[...]
    return a[:, :d] * jax.nn.sigmoid(a[:, d:])


def _conv_p_body(x_ref, g_ref, w1_ref, b1_ref, wdw_ref, bdw_ref, lng_ref, lnb_ref, w2_ref, b2_ref,
                 o_ref, ul_ref, ush_ref, y_ref, *, ts):
    j = pl.program_id(1)
    n_cb = y_ref.shape[0]
    rows_ext = CARRY_ROWS + ts

    @pl.when(j == 0)
    def _():
        ush_ref[0, :, 0:CARRY_ROWS, :] = jnp.zeros((n_cb, CARRY_ROWS, LANES), F32)

    x = x_ref[0]
    u = _glu(x, g_ref, w1_ref, b1_ref)
    for cb in range(n_cb):
        ush_ref[0, cb, CARRY_ROWS:, :] = u[:, cb * LANES:(cb + 1) * LANES]
    n_sh = rows_ext - SUBLANES
    lead = CARRY_ROWS - (CONV_WIDTH - 1)

    def column_block(cb, carry):
        for m in range(1, SUBLANES):
            for r0 in range(0, n_sh, CONV_COPY_ROWS):
                r1 = min(r0 + CONV_COPY_ROWS, n_sh)
                ush_ref[m, cb, r0:r1, :] = ush_ref[0, cb, r0 + m:r1 + m, :]

        def rows(c, carry):
            r0 = pl.multiple_of(c * CONV_CHUNK_ROWS, CONV_CHUNK_ROWS)
            acc = jnp.broadcast_to(bdw_ref[cb], (CONV_CHUNK_ROWS, LANES))
            for k in range(CONV_WIDTH):
                a, m = divmod(lead + k, SUBLANES)
                acc = acc + wdw_ref[cb, k:k + 1, :] * ush_ref[m, cb, pl.ds(r0 + a * SUBLANES, CONV_CHUNK_ROWS), :]
            y_ref[cb, pl.ds(r0, CONV_CHUNK_ROWS), :] = acc
            return carry

        return lax.fori_loop(0, ts // CONV_CHUNK_ROWS, rows, carry)

    lax.fori_loop(0, n_cb, column_block, 0)
    y = jnp.concatenate([y_ref[cb] for cb in range(n_cb)], axis=1)
    o_ref[0] = _conv_tail(y, x, lng_ref, lnb_ref, w2_ref, b2_ref, g_ref)
    last = [ush_ref[0, cb, ts:ts + CARRY_ROWS, :] for cb in range(n_cb)]
    for cb in range(n_cb):
        ush_ref[0, cb, 0:CARRY_ROWS, :] = last[cb]
    ul_ref[0] = jnp.concatenate(last, axis=1)


def _conv_p(x, gains, w1, b1, wdw, bdw, lng, lnb, w2, b2, ts):
    b, s, d = x.shape
    n_cb = d // LANES
    wdw_cb = wdw.reshape(wdw.shape[0], n_cb, LANES).transpose(1, 0, 2)
    bdw_cb = bdw.reshape(n_cb, 1, LANES)
    consts = (gains, w1, b1, wdw_cb, bdw_cb, lng, lnb, w2, b2)
    return pl.pallas_call(
        functools.partial(_conv_p_body, ts=ts),
        grid=(b, s // ts),
        in_specs=[pl.BlockSpec((1, ts, d), lambda i, j: (i, j, 0))] + [_resident(a.shape) for a in consts],
        out_specs=[pl.BlockSpec((1, ts, d), lambda i, j: (i, j, 0)),
                   pl.BlockSpec((1, CARRY_ROWS, d), lambda i, j: (i, 0, 0))],
        out_shape=[jax.ShapeDtypeStruct((b, s, d), F32), jax.ShapeDtypeStruct((b, CARRY_ROWS, d), F32)],
        scratch_shapes=[pltpu.VMEM((SUBLANES, n_cb, CARRY_ROWS + ts, LANES), F32), pltpu.VMEM((n_cb, ts, LANES), F32)],
        compiler_params=_params("parallel", "arbitrary"),
        name="conv_prompt",
    )(x, *consts)


def _conv_s_body(x_ref, st_ref, g_ref, w1_ref, b1_ref, wdw_ref, bdw_ref, lng_ref, lnb_ref, w2_ref, b2_ref,
                 o_ref, ns_ref, *, nb, nt):
    x = x_ref[...]
    u = _glu(x, g_ref, w1_ref, b1_ref)
    n_prev = CONV_WIDTH - 1
    ext = [st_ref[i * nb:(i + 1) * nb, :] for i in range(n_prev)]
    ext += [u[t * nb:(t + 1) * nb, :] for t in range(nt)]
    ys = []
    for t in range(nt):
        y = None
        for k in range(CONV_WIDTH):
            term = wdw_ref[k:k + 1, :] * ext[t + k]
            y = term if y is None else y + term
        ys.append(y + bdw_ref[...])
    o_ref[...] = _conv_tail(jnp.concatenate(ys, axis=0), x, lng_ref, lnb_ref, w2_ref, b2_ref, g_ref)
    for i in range(n_prev):
        ns_ref[i * nb:(i + 1) * nb, :] = ext[nt + i]


def _conv_s(x, state, gains, w1, b1, wdw, bdw, lng, lnb, w2, b2, nb, nt):
    return pl.pallas_call(
        functools.partial(_conv_s_body, nb=nb, nt=nt),
        out_shape=[jax.ShapeDtypeStruct(x.shape, F32), jax.ShapeDtypeStruct(state.shape, F32)],
        compiler_params=pltpu.CompilerParams(vmem_limit_bytes=VMEM_LIMIT),
        name="conv_sample",
    )(x, state, gains, w1, b1, wdw, bdw, lng, lnb, w2, b2)


def _t5_bucket(dist):
    max_exact = N_BUCKETS // 2
    df = jnp.maximum(dist, 1).astype(F32)
    large = max_exact + (jnp.log(df / max_exact) / math.log(MAX_DISTANCE / max_exact)
                         * (N_BUCKETS - max_exact)).astype(jnp.int32)
    large = jnp.minimum(large, N_BUCKETS - 1)
    return jnp.where(dist < max_exact, dist, large)


def _bias_by_step(rel_bias, g, dil, n_steps, n_heads):
    tab = rel_bias[:, g * n_heads:(g + 1) * n_heads]
    bucket = _t5_bucket(dil * jnp.arange(n_steps + 1, dtype=jnp.int32))
    onehot = (bucket[:, None] == jnp.arange(N_BUCKETS, dtype=jnp.int32)[None, :]).astype(F32)
    return jnp.dot(onehot, tab, precision=lax.Precision.HIGHEST)


def _band_bias(by_step, n_steps, nk):
    off = nk - QB
    period = 2 * nk
    heads = by_step.shape[1]
    base = jnp.concatenate([by_step[::-1].T, jnp.full((heads, period - n_steps - 1), NEG_INF, F32)], axis=1)
    w = jnp.roll(base, off - n_steps, axis=1)
    r = period - 1
    flat = jnp.tile(w, (1, -(-(QB * r) // period)))[:, :QB * r]
    band = flat.reshape(heads, QB, r)[:, :, :nk]
    if off > 0:
        hidden = (np.arange(nk) < off)[None, None, :]
        band = jnp.stack([band, jnp.where(hidden, NEG_INF, band)])
    else:
        band = band[None]
    per = LANES // HEAD_DIM
    return band.reshape(band.shape[0], heads // per, per * QB, nk)


def _window_bias(by_step, dil, n_steps, nt):
    base = jnp.repeat(by_step[n_steps:0:-1].T, dil, axis=1)
    bias = jnp.stack([jnp.roll(base, t, axis=1) for t in range(nt)], axis=1)
    n = dil * n_steps
    delta = n + np.arange(nt)[:, None] - np.arange(n)[None, :]
    attended = (delta % dil == 0) & (delta <= dil * n_steps)
    return jnp.where(attended[None], bias, NEG_INF)


def _new_key_bias(by_step, dil, n_steps, nt):
    zero = jnp.zeros_like(by_step[0])
    rows = []
    for t in range(nt):
        cols = []
        for tp in range(nt):
            s, rem = divmod(t - tp, dil)
            cols.append(by_step[s] if (t >= tp and rem == 0 and s <= n_steps) else zero)
        rows.append(jnp.stack(cols, axis=1))
    return jnp.tile(jnp.stack(rows, axis=1), (1, 1, LANES // nt))


def _time_major(a):
    b, t, d = a.shape
    return a.transpose(1, 0, 2).reshape(t * b, d)


def _batch_major(a, b):
    return a.reshape(-1, b, a.shape[1]).transpose(1, 0, 2)


def _kv_window_layout(c):
    return c.transpose(0, 2, 3, 4, 1)


def _kv_window_unlayout(c):
    return c.transpose(0, 4, 1, 2, 3)


def kernel(x_prompt, x_sample, state_pool, cache_kv_g0, cache_kv_g1, cache_kv_g2, state_conv, norm_gains, rel_bias, pool_w, pool_scale, attn_w_qkv, attn_w_o, conv_w_pw1, conv_b_pw1, conv_w_dw, conv_b_dw, conv_ln_g, conv_ln_b, conv_w_pw2, conv_b_pw2, ffn_w_up, ffn_w_down):
    batch, seq, d = x_prompt.shape
    dec_b, dec_t, _ = x_sample.shape
    depth = norm_gains.shape[0]
    n_heads = attn_w_o.shape[1] // HEAD_DIM
    e_dim = n_heads * HEAD_DIM
    n_groups = len(ATTN_GROUPS)
    dils = tuple(dil for _, dil in ATTN_GROUPS)
    caches_in = (cache_kv_g0, cache_kv_g1, cache_kv_g2)
    n_tok = batch * seq
    n_dec = dec_b * dec_t

    xp = x_prompt
    xs = _time_major(x_sample)

    expand = jnp.asarray(np.tile(np.kron(np.eye(LANES, n_heads), np.ones((1, HEAD_DIM))), (2, 1)), BF16)

    w_up, w_down = ffn_w_up.astype(BF16), ffn_w_down.astype(BF16)

    pool_p, pool_s, conv_p, conv_s = [], [], [], []
    kv_p = [[] for _ in ATTN_GROUPS]
    kv_s = [[] for _ in ATTN_GROUPS]
    for i in range(depth):
        kind, j = i % 3, i // 3
        gains = norm_gains[i]
        if kind == 0:
            pw, ps = pool_w[j].astype(BF16), pool_scale[j][None, :]
            xp, h_last = _pool_p(xp, gains, pw, ps, w_up, w_down, i, ts=512)
            pool_p.append(h_last[:, POOL_CARRY - POOL_STATE:])
            xs, new_state = _pool_s(xs, _time_major(state_pool[j]), gains, pw, ps,
                                    nb=dec_b, nt=dec_t, pos0=PAST_LEN)
            pool_s.append(_batch_major(new_state, dec_b))
            xs = _ffn(xs, gains, w_up, w_down, i, tm=n_dec)
            continue
        elif kind == 1:
            w_qkv = attn_w_qkv[j].astype(BF16)
            w_o = attn_w_o[j].astype(BF16)
            by_step = [_bias_by_step(rel_bias, g, dil, win // dil, n_heads)
                       for g, (win, dil) in enumerate(ATTN_GROUPS)]
            keeps = [min(win, seq) for win, _ in ATTN_GROUPS]
            qkv, kvts = [], []
            for lo, hi in QKV_GROUP_SPLIT:
                outs = _qkv_p(xp, gains, w_qkv, lo, dils[lo:hi], keeps[lo:hi], e_dim, tm=512)
                qkv += outs[:3 * (hi - lo)]
                kvts += outs[3 * (hi - lo):]
            os_, ms, ls = [], [], []
            for g, (win, dil) in enumerate(ATTN_GROUPS):
                n_steps = win // dil
                nk = 2 * QB if seq // dil > QB else QB
                bias = _band_bias(by_step[g], n_steps, nk)
                o, m, l = _attn_p(qkv[3 * g], qkv[3 * g + 1], qkv[3 * g + 2], bias, g, n_heads)
                os_.append(o)
                ms.append(m)
                ls.append(l)
                kv_p[g].append(_kv_window_unlayout(kvts[g].reshape(batch, 2, n_heads, HEAD_DIM, keeps[g])))
            xp = _merge(xp, os_, ms, ls, expand, w_o, gains, tm=512)
            xs_b = _batch_major(xs, dec_b)
            xs_flat = xs_b.reshape(n_dec, d)
            n_chunks = ffn_w_up.shape[2] // FFN_CHUNK
            hc = n_heads // n_chunks
            q_s, kvt_new = _qkv_s(xs_flat, gains, w_qkv, n_groups)
            q_s = q_s.reshape(dec_b, dec_t, n_groups, n_chunks, hc * HEAD_DIM)
            q_s = q_s.transpose(0, 2, 3, 1, 4)
            groups = tuple((dil, win // dil) for win, dil in ATTN_GROUPS)
            caches = [_kv_window_layout(c[j]) for c in caches_in]
            bias_cs = [_window_bias(by_step[g], dil, n_steps, dec_t) for g, (dil, n_steps) in enumerate(groups)]
            bias_ns = [_new_key_bias(by_step[g], dil, n_steps, dec_t) for g, (dil, n_steps) in enumerate(groups)]
            xp, os_, lses, new_caches = _ffn_attn_s(xp.reshape(n_tok, d), gains, w_up, w_down, i, q_s, kvt_new,
                                                    caches, bias_cs, bias_ns, groups, dec_t, tm=n_tok // dec_b)
            xp = xp.reshape(batch, seq, d)
            ms, ls = [], []
            for g in range(n_groups):
                kv_s[g].append(_kv_window_unlayout(new_caches[g]))
                lse = lses[g][..., :hc].transpose(0, 2, 1, 3).reshape(1, 1, n_dec, n_heads)
                ms.append(jnp.pad(lse, ((0, 0), (0, 0), (0, 0), (0, LANES - n_heads))))
                ls.append(jnp.ones((1, 1, n_dec, LANES), F32))
            os_ = [o.transpose(0, 3, 1, 2).reshape(1, 1, n_dec, e_dim) for o in os_]
            xs_flat = _merge(xs_flat[None], os_, ms, ls, expand, w_o, gains, tm=n_dec)[0]
            xs = _time_major(xs_flat.reshape(dec_b, dec_t, d))
            xs = _ffn(xs, gains, w_up, w_down, i, tm=n_dec)
            continue
        else:
            cw = (conv_w_pw1[j].astype(BF16), conv_b_pw1[j][None, :], conv_w_dw[j], conv_b_dw[j][None, :],
                  conv_ln_g[j][None, :], conv_ln_b[j][None, :], conv_w_pw2[j].astype(BF16), conv_b_pw2[j][None, :])
            xp, u_last = _conv_p(xp, gains, *cw, ts=512)
            conv_p.append(u_last[:, CARRY_ROWS - (CONV_WIDTH - 1):])
            xs, new_state = _conv_s(xs, _time_major(state_conv[j]), gains, *cw, nb=dec_b, nt=dec_t)
            conv_s.append(_batch_major(new_state, dec_b))
        xp = _ffn(xp.reshape(n_tok, d), gains, w_up, w_down, i, tm=1024).reshape(batch, seq, d)
        xs = _ffn(xs, gains, w_up, w_down, i, tm=n_dec)
    return (xp, _batch_major(xs, dec_b),
            jnp.stack(pool_p), jnp.stack(pool_s),
            jnp.stack(kv_p[0]), jnp.stack(kv_s[0]),
            jnp.stack(kv_p[1]), jnp.stack(kv_s[1]),
            jnp.stack(kv_p[2]), jnp.stack(kv_s[2]),
            jnp.stack(conv_p), jnp.stack(conv_s))
```

```python
import functools
import math

import jax
import jax.numpy as jnp
import numpy as np
from jax import lax
from jax.experimental import pallas as pl
from jax.experimental.pallas import tpu as pltpu

F32 = jnp.float32
BF16 = jnp.bfloat16

EPS = 1e-6
NEG_INF = -1e30
POOL_WINDOWS = (2, 4, 8, 16)
POOL_STATE = max(POOL_WINDOWS) - 1
PAST_LEN = 8192
ATTN_GROUPS = ((128, 1), (512, 4), (2048, 16))
HEAD_DIM = 64
QB = 128
ATTN_BLOCKS_PER_STEP = 4
ATTN_AHEAD = 4
N_BUCKETS = 32
MAX_DISTANCE = 2048
CONV_WIDTH = 31
CARRY_ROWS = 32
POOL_CARRY = 16
LANES = 128
SUBLANES = 8
CONV_CHUNK_ROWS = 128
CONV_COPY_ROWS = 128

VMEM_LIMIT = 60 * 1024 * 1024
QKV_GROUP_SPLIT = ((0, 2), (2, 3))
MLP_CHUNK = 1024
FFN_CHUNK = 1024

NT_DIMS = (((1,), (1,)), ((), ()))


def _params(*sem):
    return pltpu.CompilerParams(dimension_semantics=sem, vmem_limit_bytes=VMEM_LIMIT)


def _resident(shape):
    nd = len(shape)
    return pl.BlockSpec(shape, lambda *_: (0,) * nd, pipeline_mode=pl.Buffered(1))


def _rms(x, g):
    return x * lax.rsqrt(jnp.mean(x * x, axis=-1, keepdims=True) + EPS) * g


def _dot(a, b):
    return jnp.dot(a, b, preferred_element_type=F32)


def _dot_nt(a, b):
    return lax.dot_general(a, b, NT_DIMS, preferred_element_type=F32)


def _mlp(x, g_ref, wup_ref, wdn_ref):
    h = _rms(x, g_ref[2:3]).astype(BF16)
    acc = jnp.zeros_like(x)
    for c in range(wup_ref.shape[2] // MLP_CHUNK):
        cols = slice(c * MLP_CHUNK, (c + 1) * MLP_CHUNK)
        a = jnp.maximum(_dot(h, wup_ref[0, :, cols]), 0.0)
        acc = acc + _dot((a * a).astype(BF16), wdn_ref[0, cols, :])
    return x + _rms(acc, g_ref[3:4])


def _mlp_guest(first_step, xs_ref, g_ref, wup_ref, wdn_ref, os_ref):
    @pl.when(first_step)
    def _():
        os_ref[...] = _mlp(xs_ref[...], g_ref, wup_ref, wdn_ref)


def _ffn_body(x_ref, xs_ref, g_ref, wup_ref, wdn_ref, o_ref, os_ref):
    _mlp_guest(pl.program_id(0) == 0, xs_ref, g_ref, wup_ref, wdn_ref, os_ref)
    o_ref[...] = _mlp(x_ref[...], g_ref, wup_ref, wdn_ref)


def _ffn_s_body(x_ref, g_ref, wup_ref, wdn_ref, o_ref):
    o_ref[...] = _mlp(x_ref[...], g_ref, wup_ref, wdn_ref)


def _layer_weights(w, layer):
    nd = w.ndim - 1
    return pl.BlockSpec((1,) + w.shape[1:], lambda *_: (layer,) + (0,) * nd, pipeline_mode=pl.Buffered(1))


def _ffn(x, xs, gains, w_up, w_down, layer, tm):
    n, d = x.shape
    return pl.pallas_call(
        _ffn_body,
        grid=(n // tm,),
        in_specs=[pl.BlockSpec((tm, d), lambda i: (i, 0)), _resident(xs.shape), _resident(gains.shape),
                  _layer_weights(w_up, layer), _layer_weights(w_down, layer)],
        out_specs=[pl.BlockSpec((tm, d), lambda i: (i, 0)), pl.BlockSpec(xs.shape, lambda i: (0, 0))],
        out_shape=[jax.ShapeDtypeStruct((n, d), F32), jax.ShapeDtypeStruct(xs.shape, F32)],
        compiler_params=_params("arbitrary"),
        name="ffn",
    )(x, xs, gains, w_up, w_down)


def _ffn_s(x, gains, w_up, w_down, layer):
    return pl.pallas_call(
        _ffn_s_body,
        grid=(1,),
        in_specs=[_resident(x.shape), _resident(gains.shape), _layer_weights(w_up, layer), _layer_weights(w_down, layer)],
        out_specs=pl.BlockSpec(x.shape, lambda i: (0, 0)),
        out_shape=jax.ShapeDtypeStruct(x.shape, F32),
        compiler_params=_params("arbitrary"),
        name="ffn_sample",
    )(x, gains, w_up, w_down)


def _pool_p_body(x_ref, xs_ref, g_ref, pw_ref, ps_ref, wup_ref, wdn_ref, o_ref, hl_ref, os_ref, carry_ref, *, ts):
    j = pl.program_id(1)
    _mlp_guest((pl.program_id(0) == 0) & (j == 0), xs_ref, g_ref, wup_ref, wdn_ref, os_ref)

    @pl.when(j == 0)
    def _():
        carry_ref[...] = jnp.zeros_like(carry_ref)

    x = x_ref[0]
    h = _rms(x, g_ref[0:1])
    ext = jnp.concatenate([carry_ref[...], h], axis=0)
    gw = pw_ref.shape[1]
    assert POOL_WINDOWS == tuple(2 << gi for gi in range(len(POOL_WINDOWS)))
    sums = []
    s = ext
    for gi in range(len(POOL_WINDOWS)):
        s = s + pltpu.roll(s, 1 << gi, axis=0)
        sums.append(s[:, :gw])
        if gi + 1 < len(POOL_WINDOWS):
            s = s[:, gw:]
    pos = j * ts + lax.broadcasted_iota(jnp.int32, (ts, 1), 0)
    ys = []
    for gi, (w, s) in enumerate(zip(POOL_WINDOWS, sums)):
        cols = slice(gi * gw, (gi + 1) * gw)
        inv_cnt = 1.0 / jnp.minimum(pos + 1, w).astype(F32)
        pooled = s[POOL_CARRY:, :] * inv_cnt - h[:, cols]
        ys.append(_dot(pooled.astype(BF16), pw_ref[gi]))
    y = jnp.concatenate(ys, axis=1) * ps_ref[...]
    carry_ref[...] = h[ts - POOL_CARRY:, :]
    hl_ref[0] = h[ts - POOL_CARRY:, :]
    o_ref[0] = _mlp(x + _rms(y, g_ref[1:2]), g_ref, wup_ref, wdn_ref)


def _pool_p(x, xs, gains, pw, ps, w_up, w_down, layer, ts):
    b, s, d = x.shape
    return pl.pallas_call(
        functools.partial(_pool_p_body, ts=ts),
        grid=(b, s // ts),
        in_specs=[pl.BlockSpec((1, ts, d), lambda i, j: (i, j, 0)), _resident(xs.shape), _resident(gains.shape),
                  _resident(pw.shape), _resident(ps.shape),
                  _layer_weights(w_up, layer), _layer_weights(w_down, layer)],
        out_specs=[pl.BlockSpec((1, ts, d), lambda i, j: (i, j, 0)),
                   pl.BlockSpec((1, POOL_CARRY, d), lambda i, j: (i, 0, 0)),
                   pl.BlockSpec(xs.shape, lambda i, j: (0, 0))],
        out_shape=[jax.ShapeDtypeStruct((b, s, d), F32),
                   jax.ShapeDtypeStruct((b, POOL_CARRY, d), F32),
                   jax.ShapeDtypeStruct(xs.shape, F32)],
        scratch_shapes=[pltpu.VMEM((POOL_CARRY, d), F32)],
        compiler_params=_params("arbitrary", "arbitrary"),
        name="pool_mlp_prompt",
    )(x, xs, gains, pw, ps, w_up, w_down)


def _pool_s_body(x_ref, st_ref, g_ref, pw_ref, ps_ref, o_ref, ns_ref, *, nb, nt, pos0):
    x = x_ref[...]
    h = _rms(x, g_ref[0:1])
    ext = [st_ref[i * nb:(i + 1) * nb, :] for i in range(POOL_STATE)]
    ext += [h[t * nb:(t + 1) * nb, :] for t in range(nt)]
    gw = pw_ref.shape[1]
    for t in range(nt):
        ys = []
        for gi, w in enumerate(POOL_WINDOWS):
            cols = slice(gi * gw, (gi + 1) * gw)
            end = POOL_STATE + t
            acc = ext[end][:, cols]
            for i in range(1, min(w, end + 1)):
                acc = acc + ext[end - i][:, cols]
            pooled = acc / float(min(pos0 + t + 1, w)) - ext[end][:, cols]
            ys.append(_dot(pooled.astype(BF16), pw_ref[gi]))
        y = jnp.concatenate(ys, axis=1) * ps_ref[...]
        o_ref[t * nb:(t + 1) * nb, :] = x[t * nb:(t + 1) * nb, :] + _rms(y, g_ref[1:2])
    for i in range(POOL_STATE):
        ns_ref[i * nb:(i + 1) * nb, :] = ext[nt + i]


def _pool_s(x, state, gains, pw, ps, nb, nt, pos0):
    return pl.pallas_call(
        functools.partial(_pool_s_body, nb=nb, nt=nt, pos0=pos0),
        out_shape=[jax.ShapeDtypeStruct(x.shape, F32), jax.ShapeDtypeStruct(state.shape, F32)],
        compiler_params=pltpu.CompilerParams(vmem_limit_bytes=VMEM_LIMIT),
        name="pool_sample",
    )(x, state, gains, pw, ps)


def _kept_tiles(seq, keep, tm):
    rows = min(keep, tm)
    tiles = keep // rows
    return seq // tm - tiles, tiles, rows


def _qkv_p_body(x_ref, g_ref, w_ref, *refs, dils, keeps, seq, tm):
    ng = len(dils)
    out_refs, kvt_refs, ybuf_ref = refs[:3 * ng], refs[3 * ng:4 * ng], refs[-1]
    e = out_refs[0].shape[-1]
    j = pl.program_id(1)
    h = _rms(x_ref[0], g_ref[0:1]).astype(BF16)

    for g, dil in enumerate(dils):
        first, _, rows = _kept_tiles(seq, keeps[g], tm)
        for c in range(3):
            idx = g * 3 + c
            y = _dot(h, w_ref[:, idx * e:(idx + 1) * e])
            if c == 0:
                y = y * (HEAD_DIM ** -0.5)
            else:
                def keep_window(y=y, g=g, c=c, rows=rows):
                    kvt_refs[g][0, (c - 1) * e:c * e, :] = y[tm - rows:, :].T

                if first == 0:
                    keep_window()
                else:
                    pl.when(j >= first)(keep_window)
            if dil == 1:
                out_refs[idx][0, 0] = y.astype(BF16)
            else:
                slot = idx % 2
                for cc in range(e // LANES):
                    ybuf_ref[slot, cc] = y[:, cc * LANES:(cc + 1) * LANES]
                for r in range(dil):
                    out_refs[idx][0, r] = jnp.concatenate(
                        [ybuf_ref[slot, cc, pl.ds(r, tm // dil, stride=dil), :] for cc in range(e // LANES)],
                        axis=1).astype(BF16)


def _qkv_p(x, gains, w, first_group, dils, keeps, e, tm):
    b, s, d = x.shape
    width = len(dils) * 3 * e
    assert (first_group * 3 * e) % width == 0
    w_spec = pl.BlockSpec((d, width), lambda i, j: (0, first_group * 3 * e // width), pipeline_mode=pl.Buffered(1))
    out_specs, out_shape = [], []
    for dil in dils:
        for _ in range(3):
            out_specs.append(pl.BlockSpec((1, dil, tm // dil, e), lambda i, j: (i, 0, j, 0)))
            out_shape.append(jax.ShapeDtypeStruct((b, dil, s // dil, e), BF16))
    for keep in keeps:
        first, _, rows = _kept_tiles(s, keep, tm)
        out_specs.append(pl.BlockSpec((1, 2 * e, rows), lambda i, j, first=first: (i, 0, jnp.maximum(j - first, 0))))
        out_shape.append(jax.ShapeDtypeStruct((b, 2 * e, keep), F32))
    return pl.pallas_call(
        functools.partial(_qkv_p_body, dils=dils, keeps=tuple(keeps), seq=s, tm=tm),
        grid=(b, s // tm),
        in_specs=[pl.BlockSpec((1, tm, d), lambda i, j: (i, j, 0)), _resident(gains.shape), w_spec],
        out_specs=out_specs,
        out_shape=out_shape,
        scratch_shapes=[pltpu.VMEM((2, e // LANES, tm, LANES), F32)],
        compiler_params=_params("parallel", "arbitrary"),
        name="qkv_prompt",
    )(x, gains, w)


def _qkv_s_body(x_ref, g_ref, w_ref, q_ref, kvt_ref, *, n_groups):
    e = w_ref.shape[1] // (3 * n_groups)
    h = _rms(x_ref[...], g_ref[0:1]).astype(BF16)
    for g in range(n_groups):
        q_ref[:, g * e:(g + 1) * e] = _dot(h, w_ref[:, 3 * g * e:(3 * g + 1) * e]) * (HEAD_DIM ** -0.5)
        for c in (1, 2):
            kv = _dot(h, w_ref[:, (3 * g + c) * e:(3 * g + c + 1) * e])
            kvt_ref[(2 * g + c - 1) * e:(2 * g + c) * e, :] = kv.T


def _qkv_s(x, gains, w, n_groups):
    n = x.shape[0]
    e = w.shape[1] // (3 * n_groups)
    return pl.pallas_call(
        functools.partial(_qkv_s_body, n_groups=n_groups),
        out_shape=[jax.ShapeDtypeStruct((n, n_groups * e), F32), jax.ShapeDtypeStruct((n_groups * 2 * e, n), F32)],
        compiler_params=pltpu.CompilerParams(vmem_limit_bytes=VMEM_LIMIT),
        name="qkv_sample",
    )(x, gains, w)


def _attn_p_body(*refs, has_prev, n_heads):
    if has_prev:
        q_ref, kp_ref, kc_ref, vp_ref, vc_ref, bias_ref, ones_ref, o_ref, m_ref, l_ref = refs
        first_step = (pl.program_id(2) == 0).astype(jnp.int32)
    else:
        q_ref, kc_ref, vc_ref, bias_ref, ones_ref, o_ref, m_ref, l_ref = refs
    n_sub = q_ref.shape[2] // QB
    per = LANES // HEAD_DIM
    n_grp = n_heads // per
    lane = lax.broadcasted_iota(jnp.int32, (QB, LANES), 1)
    lane_head = lax.broadcasted_iota(jnp.int32, (1, LANES), 1) // HEAD_DIM

    def tiles(ref_prev, ref_cur, sub, cols):
        own = ref_cur[0, 0, sub * QB:(sub + 1) * QB, cols]
        if not has_prev:
            return [own]
        before = ref_prev[0, 0, :, cols] if sub == 0 else ref_cur[0, 0, (sub - 1) * QB:sub * QB, cols]
        return [before, own]

    def scores(item):
        sub, grp = item
        cols = slice(grp * LANES, (grp + 1) * LANES)
        q = q_ref[0, 0, sub * QB:(sub + 1) * QB, cols]
        qq = jnp.concatenate([q * jnp.where(lane_head == hh, 1.0, 0.0).astype(BF16) for hh in range(per)], axis=0)
        k = jnp.concatenate(tiles(kp_ref if has_prev else None, kc_ref, sub, cols), axis=0)
        first = first_step if (has_prev and sub == 0) else 0
        return _dot_nt(qq, k) + bias_ref[first, grp]

    items = [(sub, grp) for sub in range(n_sub) for grp in range(n_grp)]
    stats = [[jnp.zeros((QB, LANES), F32), jnp.ones((QB, LANES), F32)]
             for _ in range(n_sub)]
    queue = [scores(it) for it in items[:ATTN_AHEAD]]
    for idx, (sub, grp) in enumerate(items):
        s = queue.pop(0)
        if idx + ATTN_AHEAD < len(items):
            queue.append(scores(items[idx + ATTN_AHEAD]))
        cols = slice(grp * LANES, (grp + 1) * LANES)
        m = jnp.max(s, axis=-1, keepdims=True)
        e = jnp.exp(s - m).astype(BF16)
        v1 = jnp.concatenate([jnp.concatenate([v, ones_ref[...]], axis=1)
                              for v in tiles(vp_ref if has_prev else None, vc_ref, sub, cols)], axis=0)
        ol = _dot(e, v1)
        o, l = ol[:, :LANES], ol[:, LANES:]
        out = o[0:QB]
        for hh in range(per):
            rows = slice(hh * QB, (hh + 1) * QB)
            if hh > 0:
                out = jnp.where(lane_head == hh, o[rows], out)
            stats[sub][0] = jnp.where(lane == grp * per + hh, m[rows], stats[sub][0])
            stats[sub][1] = jnp.where(lane == grp * per + hh, l[rows], stats[sub][1])
        o_ref[0, 0, sub * QB:(sub + 1) * QB, cols] = out.astype(o_ref.dtype)
    for sub in range(n_sub):
        m_ref[0, 0, sub * QB:(sub + 1) * QB, :] = stats[sub][0]
        l_ref[0, 0, sub * QB:(sub + 1) * QB, :] = stats[sub][1]


def _attn_p(q, k, v, bias, g, n_heads):
    out_dims = q.shape[:3]
    has_prev = q.shape[2] > QB
    assert bias.shape[0] == (2 if has_prev else 1)
    if not has_prev:
        q, k, v = (a.reshape(a.shape[0], 1, a.shape[1] * a.shape[2], a.shape[3]) for a in (q, k, v))
    batch, dil, sub, e = q.shape
    nb = sub // QB
    ones = jnp.ones((QB, LANES), BF16)
    n_sub = ATTN_BLOCKS_PER_STEP if nb % ATTN_BLOCKS_PER_STEP == 0 else 1
    blk = (1, 1, n_sub * QB, e)
    cur = lambda b, r, n: (b, r, n, 0)
    prev = lambda b, r, n: (b, r, jnp.maximum(n * n_sub - 1, 0), 0)
    if has_prev:
        one = (1, 1, QB, e)
        in_specs = [pl.BlockSpec(blk, cur), pl.BlockSpec(one, prev), pl.BlockSpec(blk, cur),
                    pl.BlockSpec(one, prev), pl.BlockSpec(blk, cur)]
        args = (q, k, k, v, v)
    else:
        in_specs = [pl.BlockSpec(blk, cur)] * 3
        args = (q, k, v)
    outs = pl.pallas_call(
        functools.partial(_attn_p_body, has_prev=has_prev, n_heads=n_heads),
        grid=(batch, dil, nb // n_sub),
        in_specs=in_specs + [_resident(bias.shape), _resident(ones.shape)],
        out_specs=[pl.BlockSpec(blk, cur)] + [pl.BlockSpec((1, 1, n_sub * QB, LANES), cur)] * 2,
        out_shape=[jax.ShapeDtypeStruct((batch, dil, sub, e), BF16)]
                  + [jax.ShapeDtypeStruct((batch, dil, sub, LANES), F32)] * 2,
        compiler_params=_params("parallel", "parallel", "arbitrary"),
        name=f"attn_prompt_g{g}",
    )(*args, bias, ones)
    return [a.reshape(out_dims + a.shape[3:]) for a in outs]


def _expand_heads(w, e_ref):
    hi = w.astype(BF16)
    lo = (w - hi.astype(F32)).astype(BF16)
    return _dot(jnp.concatenate([hi, lo], axis=1), e_ref[...])


def _merge_groups(os_, ms, ls, e_ref):
    top = functools.reduce(jnp.maximum, ms)
    es = [jnp.exp(m - top) for m in ms]
    den = functools.reduce(lambda a, b: a + b, [e * l for e, l in zip(es, ls)])
    acc = None
    for o, e in zip(os_, es):
        term = _expand_heads(e / den, e_ref) * o
        acc = term if acc is None else acc + term
    return acc


def _merge_body(x_ref, *refs, dils, tm):
    ng = len(dils)
    o_refs, m_refs, l_refs = refs[:ng], refs[ng:2 * ng], refs[2 * ng:3 * ng]
    e_ref, wo_ref, g_ref, out_ref, obuf_ref, sbuf_ref = refs[3 * ng:]
    os_, ms, ls = [], [], []
    for gi, dil in enumerate(dils):
        if dil == 1:
            os_.append(o_refs[gi][0, 0].astype(F32))
            ms.append(m_refs[gi][0, 0])
            ls.append(l_refs[gi][0, 0])
        else:
            n_cc = obuf_ref.shape[1]
            for r in range(dil):
                rows = pl.ds(r, tm // dil, stride=dil)
                for cc in range(n_cc):
                    obuf_ref[gi, cc, rows, :] = o_refs[gi][0, r, :, cc * LANES:(cc + 1) * LANES].astype(F32)
                sbuf_ref[0, gi, rows, :] = m_refs[gi][0, r]
                sbuf_ref[1, gi, rows, :] = l_refs[gi][0, r]
            os_.append(jnp.concatenate([obuf_ref[gi, cc] for cc in range(n_cc)], axis=1))
            ms.append(sbuf_ref[0, gi])
            ls.append(sbuf_ref[1, gi])
    o = _merge_groups(os_, ms, ls, e_ref)
    y = _dot(o.astype(BF16), wo_ref[...])
    out_ref[0] = x_ref[0] + _rms(y, g_ref[1:2])


def _merge(x, os_, ms, ls, expand, w_o, gains, tm):
    b, s, d = x.shape
    dils = tuple(o.shape[1] for o in os_)
    e = os_[0].shape[-1]
    grp = lambda i, j: (i, 0, j, 0)
    return pl.pallas_call(
        functools.partial(_merge_body, dils=dils, tm=tm),
        grid=(b, s // tm),
        in_specs=[pl.BlockSpec((1, tm, d), lambda i, j: (i, j, 0))]
                 + [pl.BlockSpec((1, dil, tm // dil, e), grp) for dil in dils]
                 + [pl.BlockSpec((1, dil, tm // dil, LANES), grp) for dil in dils] * 2
                 + [_resident(expand.shape), _resident(w_o.shape), _resident(gains.shape)],
        out_specs=pl.BlockSpec((1, tm, d), lambda i, j: (i, j, 0)),
        out_shape=jax.ShapeDtypeStruct((b, s, d), F32),
        scratch_shapes=[pltpu.VMEM((len(dils), e // LANES, tm, LANES), F32),
                        pltpu.VMEM((2, len(dils), tm, LANES), F32)],
        compiler_params=_params("parallel", "parallel"),
        name="attn_merge",
    )(x, *os_, *ms, *ls, expand, w_o, gains)


class _WindowAttnChunk:
    def __init__(self, b, q, knew, vnew, c_ref, bias_c, bias_n, cout_ref, *, dil, n_steps, nt, hc):
        self.b, self.q, self.knew, self.vnew, self.c_ref, self.cout_ref = b, q.astype(BF16), knew, vnew, c_ref, cout_ref
        self.bias_c, self.bias_n, self.dil, self.n_steps, self.nt = bias_c, bias_n, dil, n_steps, nt
        self.heads = [slice(hh * HEAD_DIM, (hh + 1) * HEAD_DIM) for hh in range(hc)]
        self.lane = lax.broadcasted_iota(jnp.int32, (nt, LANES), 1)

    def logits(self):
        nt, dil = self.nt, self.dil
        dn = lax.broadcasted_iota(jnp.int32, (nt, LANES), 0) - lax.rem(self.lane, nt)
        valid_n = (self.lane // nt == self.b) & (dn >= 0) & (lax.rem(dn, dil) == 0) & (dn <= dil * self.n_steps)
        self.s_c = [_dot(self.q[:, sl], self.c_ref[0, 0, hh].astype(BF16)) + self.bias_c[hh]
                    for hh, sl in enumerate(self.heads)]
        self.s_n = [jnp.where(valid_n, _dot(self.q[:, sl], self.knew[sl, :].astype(BF16)) + self.bias_n[hh], NEG_INF)
                    for hh, sl in enumerate(self.heads)]

    def softmax(self):
        ms = [jnp.maximum(jnp.max(a, axis=-1, keepdims=True), jnp.max(c, axis=-1, keepdims=True))
              for a, c in zip(self.s_c, self.s_n)]
        e_c = [jnp.exp(a - m) for a, m in zip(self.s_c, ms)]
        e_n = [jnp.exp(c - m) for c, m in zip(self.s_n, ms)]
        ls = [jnp.sum(a, axis=-1, keepdims=True) + jnp.sum(c, axis=-1, keepdims=True) for a, c in zip(e_c, e_n)]
        self.p_c = [(e / l).astype(BF16) for e, l in zip(e_c, ls)]
        self.p_n = [(e / l).astype(BF16) for e, l in zip(e_n, ls)]
        self.lse = jnp.zeros((self.nt, LANES), F32)
        for hh, (m, l) in enumerate(zip(ms, ls)):
            self.lse = jnp.where(self.lane == hh, m + jnp.log(l), self.lse)

    def output(self):
        os_ = [_dot_nt(self.c_ref[0, 1, hh].astype(BF16), self.p_c[hh])
               + _dot_nt(self.vnew[sl, :].astype(BF16), self.p_n[hh]) for hh, sl in enumerate(self.heads)]
        return jnp.concatenate(os_, axis=0), self.lse

    def shift(self):
        n, nt = self.c_ref.shape[-1], self.nt
        tail = lax.broadcasted_iota(jnp.int32, (HEAD_DIM, LANES), 1) >= LANES - nt
        lanes_to_tail = LANES - nt - self.b * nt
        for hh, sl in enumerate(self.heads):
            for kv, new in enumerate((self.knew, self.vnew)):
                rolled = pltpu.roll(self.c_ref[0, kv, hh], n - nt, axis=1)
                placed = pltpu.roll(new[sl, :], lanes_to_tail, axis=1)
                if n > LANES:
                    self.cout_ref[0, kv, hh, :, 0:n - LANES] = rolled[:, 0:n - LANES]
                self.cout_ref[0, kv, hh, :, n - LANES:n] = jnp.where(tail, placed, rolled[:, n - LANES:n])


def _ffn_attn_s_body(x_ref, g_ref, wup_ref, wdn_ref, q_ref, kvt_ref, *refs, groups, nt, hc, f_chunk, n_chunks):
    ng = len(groups)
    c_refs, bc_refs, bn_refs = refs[:ng], refs[ng:2 * ng], refs[2 * ng:3 * ng]
    y_ref = refs[3 * ng]
    o_refs, lse_refs, cout_refs = (refs[3 * ng + 1 + k * ng:3 * ng + 1 + (k + 1) * ng] for k in range(3))
    h_ref, acc_ref = refs[-2:]
    i, j = pl.program_id(0), pl.program_id(1)

    @pl.when(j == 0)
    def _():
        h_ref[...] = _rms(x_ref[...], g_ref[2:3]).astype(BF16)
        acc_ref[...] = jnp.zeros_like(acc_ref)

    cw = hc * HEAD_DIM
    e = cw * n_chunks
    attn = []
    for g, (dil, n_steps) in enumerate(groups):
        knew = kvt_ref[pl.ds(pl.multiple_of((2 * g) * e + j * cw, cw), cw), :]
        vnew = kvt_ref[pl.ds(pl.multiple_of((2 * g + 1) * e + j * cw, cw), cw), :]
        bias_c = [bc_refs[g][j * hc + hh] for hh in range(hc)]
        bias_n = [bn_refs[g][j * hc + hh] for hh in range(hc)]
        attn.append(_WindowAttnChunk(i, q_ref[i, g, j], knew, vnew, c_refs[g], bias_c, bias_n, cout_refs[g],
                                     dil=dil, n_steps=n_steps, nt=nt, hc=hc))
    hidden = pl.ds(pl.multiple_of(j * f_chunk, f_chunk), f_chunk)
    up = jnp.maximum(_dot(h_ref[...], wup_ref[0, :, hidden]), 0.0)
    for a in attn:
        a.logits()
    acc_ref[...] += _dot((up * up).astype(BF16), wdn_ref[0, hidden, :])
    for a in attn:
        a.softmax()
    for g, a in enumerate(attn):
        o_refs[g][0, j], lse_refs[g][0, j] = a.output()
    for a in attn:
        a.shift()

    @pl.when(j == n_chunks - 1)
    def _():
        y_ref[...] = x_ref[...] + _rms(acc_ref[...], g_ref[3:4])


def _ffn_attn_s(x, gains, w_up, w_down, layer, q, kvt_new, caches, bias_cs, bias_ns, groups, nt, tm):
    n, d = x.shape
    n_chunks, f_chunk = q.shape[2], w_up.shape[2] // q.shape[2]
    b, _, n_heads, _, _ = caches[0].shape
    hc = n_heads // n_chunks
    cw = hc * HEAD_DIM
    assert n // tm == b and kvt_new.shape[1] == LANES == b * nt and hc * n_chunks == n_heads
    assert all(c.shape[-1] == dil * n_steps for c, (dil, n_steps) in zip(caches, groups))
    row = lambda i, j: (i, 0)
    win = lambda i, j: (i, 0, j, 0, 0)
    per_row = lambda i, j: (i, 0, 0, 0)
    win_specs = [pl.BlockSpec((1, 2, hc, HEAD_DIM, c.shape[-1]), win) for c in caches]
    ng = len(caches)
    outs = pl.pallas_call(
        functools.partial(_ffn_attn_s_body, groups=groups, nt=nt, hc=hc, f_chunk=f_chunk, n_chunks=n_chunks),
        grid=(b, n_chunks),
        in_specs=[pl.BlockSpec((tm, d), row), _resident(gains.shape), _layer_weights(w_up, layer),
                  _layer_weights(w_down, layer), _resident(q.shape), _resident(kvt_new.shape)]
                 + win_specs + [_resident(a.shape) for a in bias_cs] + [_resident(a.shape) for a in bias_ns],
        out_specs=[pl.BlockSpec((tm, d), row)]
                  + [pl.BlockSpec((1, n_chunks, cw, nt), per_row)] * ng
                  + [pl.BlockSpec((1, n_chunks, nt, LANES), per_row)] * ng + win_specs,
        out_shape=[jax.ShapeDtypeStruct((n, d), F32)]
                  + [jax.ShapeDtypeStruct((b, n_chunks, cw, nt), F32)] * ng
                  + [jax.ShapeDtypeStruct((b, n_chunks, nt, LANES), F32)] * ng
                  + [jax.ShapeDtypeStruct(c.shape, F32) for c in caches],
        scratch_shapes=[pltpu.VMEM((tm, d), BF16), pltpu.VMEM((tm, d), F32)],
        compiler_params=_params("parallel", "arbitrary"),
        name="ffn_with_sample_attn",
    )(x, gains, w_up, w_down, q, kvt_new, *caches, *bias_cs, *bias_ns)
    return outs[0], outs[1:1 + ng], outs[1 + ng:1 + 2 * ng], outs[1 + 2 * ng:]


def _conv_tail(y, x, lng_ref, lnb_ref, w2_ref, b2_ref, g_ref):
    mu = jnp.mean(y, axis=-1, keepdims=True)
    yc = y - mu
    var = jnp.mean(yc * yc, axis=-1, keepdims=True)
    yn = yc * lax.rsqrt(var + EPS) * lng_ref[...] + lnb_ref[...]
    z = _dot((yn * jax.nn.sigmoid(yn)).astype(BF16), w2_ref[...]) + b2_ref[...]
    return x + _rms(z, g_ref[1:2])


def _glu(x, g_ref, w1_ref, b1_ref):
    d = x.shape[1]
    a = _dot(_rms(x, g_ref[0:1]).astype(BF16), w1_ref[...]) + b1_ref[...]
    return a[:, :d] * jax.nn.sigmoid(a[:, d:])


def _conv_p_body(x_ref, g_ref, w1_ref, b1_ref, wdw_ref, bdw_ref, lng_ref, lnb_ref, w2_ref, b2_ref,
                 o_ref, ul_ref, ush_ref, y_ref, *, ts):
    j = pl.program_id(1)
    n_cb = y_ref.shape[0]
    rows_ext = CARRY_ROWS + ts

    @pl.when(j == 0)
    def _():
        ush_ref[0, :, 0:CARRY_ROWS, :] = jnp.zeros((n_cb, CARRY_ROWS, LANES), F32)

    x = x_ref[0]
    u = _glu(x, g_ref, w1_ref, b1_ref)
    for cb in range(n_cb):
        ush_ref[0, cb, CARRY_ROWS:, :] = u[:, cb * LANES:(cb + 1) * LANES]
    n_sh = rows_ext - SUBLANES
    lead = CARRY_ROWS - (CONV_WIDTH - 1)

    def column_block(cb, carry):
        for m in range(1, SUBLANES):
            for r0 in range(0, n_sh, CONV_COPY_ROWS):
                r1 = min(r0 + CONV_COPY_ROWS, n_sh)
                ush_ref[m, cb, r0:r1, :] = ush_ref[0, cb, r0 + m:r1 + m, :]

        def rows(c, carry):
            r0 = pl.multiple_of(c * CONV_CHUNK_ROWS, CONV_CHUNK_ROWS)
            acc = jnp.broadcast_to(bdw_ref[cb], (CONV_CHUNK_ROWS, LANES))
            for k in range(CONV_WIDTH):
                a, m = divmod(lead + k, SUBLANES)
                acc = acc + wdw_ref[cb, k:k + 1, :] * ush_ref[m, cb, pl.ds(r0 + a * SUBLANES, CONV_CHUNK_ROWS), :]
            y_ref[cb, pl.ds(r0, CONV_CHUNK_ROWS), :] = acc
            return carry

        return lax.fori_loop(0, ts // CONV_CHUNK_ROWS, rows, carry)

    lax.fori_loop(0, n_cb, column_block, 0)
    y = jnp.concatenate([y_ref[cb] for cb in range(n_cb)], axis=1)
    o_ref[0] = _conv_tail(y, x, lng_ref, lnb_ref, w2_ref, b2_ref, g_ref)
    last = [ush_ref[0, cb, ts:ts + CARRY_ROWS, :] for cb in range(n_cb)]
    for cb in range(n_cb):
        ush_ref[0, cb, 0:CARRY_ROWS, :] = last[cb]
    ul_ref[0] = jnp.concatenate(last, axis=1)


def _conv_p(x, gains, w1, b1, wdw, bdw, lng, lnb, w2, b2, ts):
    b, s, d = x.shape
    n_cb = d // LANES
    wdw_cb = wdw.reshape(wdw.shape[0], n_cb, LANES).transpose(1, 0, 2)
    bdw_cb = bdw.reshape(n_cb, 1, LANES)
    consts = (gains, w1, b1, wdw_cb, bdw_cb, lng, lnb, w2, b2)
    return pl.pallas_call(
        functools.partial(_conv_p_body, ts=ts),
        grid=(b, s // ts),
        in_specs=[pl.BlockSpec((1, ts, d), lambda i, j: (i, j, 0))] + [_resident(a.shape) for a in consts],
        out_specs=[pl.BlockSpec((1, ts, d), lambda i, j: (i, j, 0)),
                   pl.BlockSpec((1, CARRY_ROWS, d), lambda i, j: (i, 0, 0))],
        out_shape=[jax.ShapeDtypeStruct((b, s, d), F32), jax.ShapeDtypeStruct((b, CARRY_ROWS, d), F32)],
        scratch_shapes=[pltpu.VMEM((SUBLANES, n_cb, CARRY_ROWS + ts, LANES), F32), pltpu.VMEM((n_cb, ts, LANES), F32)],
        compiler_params=_params("parallel", "arbitrary"),
        name="conv_prompt",
    )(x, *consts)


def _conv_s_body(x_ref, st_ref, g_ref, w1_ref, b1_ref, wdw_ref, bdw_ref, lng_ref, lnb_ref, w2_ref, b2_ref,
                 o_ref, ns_ref, *, nb, nt):
    x = x_ref[...]
    u = _glu(x, g_ref, w1_ref, b1_ref)
    n_prev = CONV_WIDTH - 1
    ext = [st_ref[i * nb:(i + 1) * nb, :] for i in range(n_prev)]
    ext += [u[t * nb:(t + 1) * nb, :] for t in range(nt)]
    ys = []
    for t in range(nt):
        y = None
        for k in range(CONV_WIDTH):
            term = wdw_ref[k:k + 1, :] * ext[t + k]
            y = term if y is None else y + term
        ys.append(y + bdw_ref[...])
    o_ref[...] = _conv_tail(jnp.concatenate(ys, axis=0), x, lng_ref, lnb_ref, w2_ref, b2_ref, g_ref)
    for i in range(n_prev):
        ns_ref[i * nb:(i + 1) * nb, :] = ext[nt + i]


def _conv_s(x, state, gains, w1, b1, wdw, bdw, lng, lnb, w2, b2, nb, nt):
    return pl.pallas_call(
        functools.partial(_conv_s_body, nb=nb, nt=nt),
        out_shape=[jax.ShapeDtypeStruct(x.shape, F32), jax.ShapeDtypeStruct(state.shape, F32)],
        compiler_params=pltpu.CompilerParams(vmem_limit_bytes=VMEM_LIMIT),
        name="conv_sample",
    )(x, state, gains, w1, b1, wdw, bdw, lng, lnb, w2, b2)


def _t5_bucket(dist):
    max_exact = N_BUCKETS // 2
    df = jnp.maximum(dist, 1).astype(F32)
    large = max_exact + (jnp.log(df / max_exact) / math.log(MAX_DISTANCE / max_exact)
                         * (N_BUCKETS - max_exact)).astype(jnp.int32)
    large = jnp.minimum(large, N_BUCKETS - 1)
    return jnp.where(dist < max_exact, dist, large)


def _bias_by_step(rel_bias, g, dil, n_steps, n_heads):
    tab = rel_bias[:, g * n_heads:(g + 1) * n_heads]
    bucket = _t5_bucket(dil * jnp.arange(n_steps + 1, dtype=jnp.int32))
    onehot = (bucket[:, None] == jnp.arange(N_BUCKETS, dtype=jnp.int32)[None, :]).astype(F32)
    return jnp.dot(onehot, tab, precision=lax.Precision.HIGHEST)


def _band_bias(by_step, n_steps, nk):
    off = nk - QB
    period = 2 * nk
    heads = by_step.shape[1]
    base = jnp.concatenate([by_step[::-1].T, jnp.full((heads, period - n_steps - 1), NEG_INF, F32)], axis=1)
    w = jnp.roll(base, off - n_steps, axis=1)
    r = period - 1
    flat = jnp.tile(w, (1, -(-(QB * r) // period)))[:, :QB * r]
    band = flat.reshape(heads, QB, r)[:, :, :nk]
    if off > 0:
        hidden = (np.arange(nk) < off)[None, None, :]
        band = jnp.stack([band, jnp.where(hidden, NEG_INF, band)])
    else:
        band = band[None]
    per = LANES // HEAD_DIM
    return band.reshape(band.shape[0], heads // per, per * QB, nk)


def _window_bias(by_step, dil, n_steps, nt):
    base = jnp.repeat(by_step[n_steps:0:-1].T, dil, axis=1)
    bias = jnp.stack([jnp.roll(base, t, axis=1) for t in range(nt)], axis=1)
    n = dil * n_steps
    delta = n + np.arange(nt)[:, None] - np.arange(n)[None, :]
    attended = (delta % dil == 0) & (delta <= dil * n_steps)
    return jnp.where(attended[None], bias, NEG_INF)


def _new_key_bias(by_step, dil, n_steps, nt):
    zero = jnp.zeros_like(by_step[0])
    rows = []
    for t in range(nt):
        cols = []
        for tp in range(nt):
            s, rem = divmod(t - tp, dil)
            cols.append(by_step[s] if (t >= tp and rem == 0 and s <= n_steps) else zero)
        rows.append(jnp.stack(cols, axis=1))
    return jnp.tile(jnp.stack(rows, axis=1), (1, 1, LANES // nt))


def _time_major(a):
    b, t, d = a.shape
    return a.transpose(1, 0, 2).reshape(t * b, d)


def _batch_major(a, b):
    return a.reshape(-1, b, a.shape[1]).transpose(1, 0, 2)


def _kv_window_layout(c):
    return c.transpose(0, 2, 3, 4, 1)


def _kv_window_unlayout(c):
    return c.transpose(0, 4, 1, 2, 3)


def kernel(x_prompt, x_sample, state_pool, cache_kv_g0, cache_kv_g1, cache_kv_g2, state_conv, norm_gains, rel_bias, pool_w, pool_scale, attn_w_qkv, attn_w_o, conv_w_pw1, conv_b_pw1, conv_w_dw, conv_b_dw, conv_ln_g, conv_ln_b, conv_w_pw2, conv_b_pw2, ffn_w_up, ffn_w_down):
    batch, seq, d = x_prompt.shape
    dec_b, dec_t, _ = x_sample.shape
    depth = norm_gains.shape[0]
    n_heads = attn_w_o.shape[1] // HEAD_DIM
    e_dim = n_heads * HEAD_DIM
    n_groups = len(ATTN_GROUPS)
    dils = tuple(dil for _, dil in ATTN_GROUPS)
    caches_in = (cache_kv_g0, cache_kv_g1, cache_kv_g2)
    n_tok = batch * seq
    n_dec = dec_b * dec_t

    xp = x_prompt
    xs = _time_major(x_sample)

    expand = jnp.asarray(np.tile(np.kron(np.eye(LANES, n_heads), np.ones((1, HEAD_DIM))), (2, 1)), BF16)

    w_up, w_down = ffn_w_up.astype(BF16), ffn_w_down.astype(BF16)

    pool_p, pool_s, conv_p, conv_s = [], [], [], []
    kv_p = [[] for _ in ATTN_GROUPS]
    kv_s = [[] for _ in ATTN_GROUPS]
    for i in range(depth):
        kind, j = i % 3, i // 3
        gains = norm_gains[i]
        if kind == 0:
            pw, ps = pool_w[j].astype(BF16), pool_scale[j][None, :]
            xs, new_state = _pool_s(xs, _time_major(state_pool[j]), gains, pw, ps,
                                    nb=dec_b, nt=dec_t, pos0=PAST_LEN)
            pool_s.append(_batch_major(new_state, dec_b))
            xp, h_last, xs = _pool_p(xp, xs, gains, pw, ps, w_up, w_down, i, ts=512)
            pool_p.append(h_last[:, POOL_CARRY - POOL_STATE:])
            continue
        elif kind == 1:
            w_qkv = attn_w_qkv[j].astype(BF16)
            w_o = attn_w_o[j].astype(BF16)
            by_step = [_bias_by_step(rel_bias, g, dil, win // dil, n_heads)
                       for g, (win, dil) in enumerate(ATTN_GROUPS)]
            keeps = [min(win, seq) for win, _ in ATTN_GROUPS]
            qkv, kvts = [], []
            for lo, hi in QKV_GROUP_SPLIT:
                outs = _qkv_p(xp, gains, w_qkv, lo, dils[lo:hi], keeps[lo:hi], e_dim, tm=512)
                qkv += outs[:3 * (hi - lo)]
                kvts += outs[3 * (hi - lo):]
            os_, ms, ls = [], [], []
            for g, (win, dil) in enumerate(ATTN_GROUPS):
                n_steps = win // dil
                nk = 2 * QB if seq // dil > QB else QB
                bias = _band_bias(by_step[g], n_steps, nk)
                o, m, l = _attn_p(qkv[3 * g], qkv[3 * g + 1], qkv[3 * g + 2], bias, g, n_heads)
                os_.append(o)
                ms.append(m)
                ls.append(l)
                kv_p[g].append(_kv_window_unlayout(kvts[g].reshape(batch, 2, n_heads, HEAD_DIM, keeps[g])))
            xp = _merge(xp, os_, ms, ls, expand, w_o, gains, tm=512)
            xs_b = _batch_major(xs, dec_b)
            xs_flat = xs_b.reshape(n_dec, d)
            n_chunks = ffn_w_up.shape[2] // FFN_CHUNK
            hc = n_heads // n_chunks
            q_s, kvt_new = _qkv_s(xs_flat, gains, w_qkv, n_groups)
            q_s = q_s.reshape(dec_b, dec_t, n_groups, n_chunks, hc * HEAD_DIM)
            q_s = q_s.transpose(0, 2, 3, 1, 4)
            groups = tuple((dil, win // dil) for win, dil in ATTN_GROUPS)
            caches = [_kv_window_layout(c[j]) for c in caches_in]
            bias_cs = [_window_bias(by_step[g], dil, n_steps, dec_t) for g, (dil, n_steps) in enumerate(groups)]
            bias_ns = [_new_key_bias(by_step[g], dil, n_steps, dec_t) for g, (dil, n_steps) in enumerate(groups)]
            xp, os_, lses, new_caches = _ffn_attn_s(xp.reshape(n_tok, d), gains, w_up, w_down, i, q_s, kvt_new,
                                                    caches, bias_cs, bias_ns, groups, dec_t, tm=n_tok // dec_b)
            xp = xp.reshape(batch, seq, d)
            ms, ls = [], []
            for g in range(n_groups):
                kv_s[g].append(_kv_window_unlayout(new_caches[g]))
                lse = lses[g][..., :hc].transpose(0, 2, 1, 3).reshape(1, 1, n_dec, n_heads)
                ms.append(jnp.pad(lse, ((0, 0), (0, 0), (0, 0), (0, LANES - n_heads))))
                ls.append(jnp.ones((1, 1, n_dec, LANES), F32))
            os_ = [o.transpose(0, 3, 1, 2).reshape(1, 1, n_dec, e_dim) for o in os_]
            xs_flat = _merge(xs_flat[None], os_, ms, ls, expand, w_o, gains, tm=n_dec)[0]
            xs = _time_major(xs_flat.reshape(dec_b, dec_t, d))
            xs = _ffn_s(xs, gains, w_up, w_down, i)
            continue
        else:
            cw = (conv_w_pw1[j].astype(BF16), conv_b_pw1[j][None, :], conv_w_dw[j], conv_b_dw[j][None, :],
                  conv_ln_g[j][None, :], conv_ln_b[j][None, :], conv_w_pw2[j].astype(BF16), conv_b_pw2[j][None, :])
            xp, u_last = _conv_p(xp, gains, *cw, ts=512)
            conv_p.append(u_last[:, CARRY_ROWS - (CONV_WIDTH - 1):])
            xs, new_state = _conv_s(xs, _time_major(state_conv[j]), gains, *cw, nb=dec_b, nt=dec_t)
            conv_s.append(_batch_major(new_state, dec_b))
        xp, xs = _ffn(xp.reshape(n_tok, d), xs, gains, w_up, w_down, i, tm=1024)
        xp = xp.reshape(batch, seq, d)
    return (xp, _batch_major(xs, dec_b),
            jnp.stack(pool_p), jnp.stack(pool_s),
            jnp.stack(kv_p[0]), jnp.stack(kv_s[0]),
            jnp.stack(kv_p[1]), jnp.stack(kv_s[1]),
            jnp.stack(kv_p[2]), jnp.stack(kv_s[2]),
            jnp.stack(conv_p), jnp.stack(conv_s))
```

```python
import functools
import math

import jax
import jax.numpy as jnp
import numpy as np
from jax import lax
from jax.experimental import pallas as pl
from jax.experimental.pallas import tpu as pltpu

F32 = jnp.float32
BF16 = jnp.bfloat16

EPS = 1e-6
NEG_INF = -1e30
POOL_WINDOWS = (2, 4, 8, 16)
POOL_STATE = max(POOL_WINDOWS) - 1
PAST_LEN = 8192
ATTN_GROUPS = ((128, 1), (512, 4), (2048, 16))
HEAD_DIM = 64
QB = 128
ATTN_BLOCKS_PER_STEP = 4
ATTN_AHEAD = 4
N_BUCKETS = 32
MAX_DISTANCE = 2048
CONV_WIDTH = 31
CARRY_ROWS = 32
POOL_CARRY = 16
LANES = 128
SUBLANES = 8
CONV_CHUNK_ROWS = 128
CONV_COPY_ROWS = 128

VMEM_LIMIT = 60 * 1024 * 1024
QKV_GROUP_SPLIT = ((0, 2), (2, 3))
MLP_CHUNK = 1024
FFN_CHUNK = 1024

NT_DIMS = (((1,), (1,)), ((), ()))


def _params(*sem):
    return pltpu.CompilerParams(dimension_semantics=sem, vmem_limit_bytes=VMEM_LIMIT)


def _resident(shape):
    nd = len(shape)
    return pl.BlockSpec(shape, lambda *_: (0,) * nd, pipeline_mode=pl.Buffered(1))


def _rms(x, g):
    return x * lax.rsqrt(jnp.mean(x * x, axis=-1, keepdims=True) + EPS) * g


def _dot(a, b):
    return jnp.dot(a, b, preferred_element_type=F32)


def _dot_nt(a, b):
    return lax.dot_general(a, b, NT_DIMS, preferred_element_type=F32)


def _mlp(x, g_ref, wup_ref, wdn_ref):
    h = _rms(x, g_ref[2:3]).astype(BF16)
    acc = jnp.zeros_like(x)
    for c in range(wup_ref.shape[2] // MLP_CHUNK):
        cols = slice(c * MLP_CHUNK, (c + 1) * MLP_CHUNK)
        a = jnp.maximum(_dot(h, wup_ref[0, :, cols]), 0.0)
        acc = acc + _dot((a * a).astype(BF16), wdn_ref[0, cols, :])
    return x + _rms(acc, g_ref[3:4])


def _mlp_guest(first_step, xs_ref, g_ref, wup_ref, wdn_ref, os_ref):
    @pl.when(first_step)
    def _():
        os_ref[...] = _mlp(xs_ref[...], g_ref, wup_ref, wdn_ref)


def _ffn_body(x_ref, xs_ref, g_ref, wup_ref, wdn_ref, o_ref, os_ref):
    _mlp_guest(pl.program_id(0) == 0, xs_ref, g_ref, wup_ref, wdn_ref, os_ref)
    o_ref[...] = _mlp(x_ref[...], g_ref, wup_ref, wdn_ref)


def _ffn_s_body(x_ref, g_ref, wup_ref, wdn_ref, o_ref):
    o_ref[...] = _mlp(x_ref[...], g_ref, wup_ref, wdn_ref)


def _layer_weights(w, layer):
    nd = w.ndim - 1
    return pl.BlockSpec((1,) + w.shape[1:], lambda *_: (layer,) + (0,) * nd, pipeline_mode=pl.Buffered(1))


def _ffn(x, xs, gains, w_up, w_down, layer, tm):
    n, d = x.shape
    return pl.pallas_call(
        _ffn_body,
        grid=(n // tm,),
        in_specs=[pl.BlockSpec((tm, d), lambda i: (i, 0)), _resident(xs.shape), _resident(gains.shape),
                  _layer_weights(w_up, layer), _layer_weights(w_down, layer)],
        out_specs=[pl.BlockSpec((tm, d), lambda i: (i, 0)), pl.BlockSpec(xs.shape, lambda i: (0, 0))],
        out_shape=[jax.ShapeDtypeStruct((n, d), F32), jax.ShapeDtypeStruct(xs.shape, F32)],
        compiler_params=_params("arbitrary"),
        name="ffn",
    )(x, xs, gains, w_up, w_down)


def _ffn_s(x, gains, w_up, w_down, layer):
    return pl.pallas_call(
        _ffn_s_body,
        grid=(1,),
        in_specs=[_resident(x.shape), _resident(gains.shape), _layer_weights(w_up, layer), _layer_weights(w_down, layer)],
        out_specs=pl.BlockSpec(x.shape, lambda i: (0, 0)),
        out_shape=jax.ShapeDtypeStruct(x.shape, F32),
        compiler_params=_params("arbitrary"),
        name="ffn_sample",
    )(x, gains, w_up, w_down)


def _pool_p_body(x_ref, xs_ref, g_ref, pw_ref, ps_ref, wup_ref, wdn_ref, o_ref, hl_ref, os_ref, carry_ref, *, ts):
    j = pl.program_id(1)
    _mlp_guest((pl.program_id(0) == 0) & (j == 0), xs_ref, g_ref, wup_ref, wdn_ref, os_ref)

    @pl.when(j == 0)
    def _():
        carry_ref[...] = jnp.zeros_like(carry_ref)

    x = x_ref[0]
    h = _rms(x, g_ref[0:1])
    ext = jnp.concatenate([carry_ref[...], h], axis=0)
    gw = pw_ref.shape[1]
    assert POOL_WINDOWS == tuple(2 << gi for gi in range(len(POOL_WINDOWS)))
    sums = []
    s = ext
    for gi in range(len(POOL_WINDOWS)):
        s = s + pltpu.roll(s, 1 << gi, axis=0)
        sums.append(s[:, :gw])
        if gi + 1 < len(POOL_WINDOWS):
            s = s[:, gw:]
    pos = j * ts + lax.broadcasted_iota(jnp.int32, (ts, 1), 0)
    ys = []
    for gi, (w, s) in enumerate(zip(POOL_WINDOWS, sums)):
        cols = slice(gi * gw, (gi + 1) * gw)
        inv_cnt = 1.0 / jnp.minimum(pos + 1, w).astype(F32)
        pooled = s[POOL_CARRY:, :] * inv_cnt - h[:, cols]
        ys.append(_dot(pooled.astype(BF16), pw_ref[gi]))
    y = jnp.concatenate(ys, axis=1) * ps_ref[...]
    carry_ref[...] = h[ts - POOL_CARRY:, :]
    hl_ref[0] = h[ts - POOL_CARRY:, :]
    o_ref[0] = _mlp(x + _rms(y, g_ref[1:2]), g_ref, wup_ref, wdn_ref)


def _pool_p(x, xs, gains, pw, ps, w_up, w_down, layer, ts):
    b, s, d = x.shape
    return pl.pallas_call(
        functools.partial(_pool_p_body, ts=ts),
        grid=(b, s // ts),
        in_specs=[pl.BlockSpec((1, ts, d), lambda i, j: (i, j, 0)), _resident(xs.shape), _resident(gains.shape),
                  _resident(pw.shape), _resident(ps.shape),
                  _layer_weights(w_up, layer), _layer_weights(w_down, layer)],
        out_specs=[pl.BlockSpec((1, ts, d), lambda i, j: (i, j, 0)),
                   pl.BlockSpec((1, POOL_CARRY, d), lambda i, j: (i, 0, 0)),
                   pl.BlockSpec(xs.shape, lambda i, j: (0, 0))],
        out_shape=[jax.ShapeDtypeStruct((b, s, d), F32),
                   jax.ShapeDtypeStruct((b, POOL_CARRY, d), F32),
                   jax.ShapeDtypeStruct(xs.shape, F32)],
        scratch_shapes=[pltpu.VMEM((POOL_CARRY, d), F32)],
        compiler_params=_params("arbitrary", "arbitrary"),
        name="pool_mlp_prompt",
    )(x, xs, gains, pw, ps, w_up, w_down)


def _pool_s_body(x_ref, st_ref, g_ref, pw_ref, ps_ref, o_ref, ns_ref, *, nb, nt, pos0):
    x = x_ref[...]
    h = _rms(x, g_ref[0:1])
    ext = [st_ref[i * nb:(i + 1) * nb, :] for i in range(POOL_STATE)]
    ext += [h[t * nb:(t + 1) * nb, :] for t in range(nt)]
    gw = pw_ref.shape[1]
    for t in range(nt):
        ys = []
        for gi, w in enumerate(POOL_WINDOWS):
            cols = slice(gi * gw, (gi + 1) * gw)
            end = POOL_STATE + t
            acc = ext[end][:, cols]
            for i in range(1, min(w, end + 1)):
                acc = acc + ext[end - i][:, cols]
            pooled = acc / float(min(pos0 + t + 1, w)) - ext[end][:, cols]
            ys.append(_dot(pooled.astype(BF16), pw_ref[gi]))
        y = jnp.concatenate(ys, axis=1) * ps_ref[...]
        o_ref[t * nb:(t + 1) * nb, :] = x[t * nb:(t + 1) * nb, :] + _rms(y, g_ref[1:2])
    for i in range(POOL_STATE):
        ns_ref[i * nb:(i + 1) * nb, :] = ext[nt + i]


def _pool_s(x, state, gains, pw, ps, nb, nt, pos0):
    return pl.pallas_call(
        functools.partial(_pool_s_body, nb=nb, nt=nt, pos0=pos0),
        out_shape=[jax.ShapeDtypeStruct(x.shape, F32), jax.ShapeDtypeStruct(state.shape, F32)],
        compiler_params=pltpu.CompilerParams(vmem_limit_bytes=VMEM_LIMIT),
        name="pool_sample",
    )(x, state, gains, pw, ps)


def _kept_tiles(seq, keep, tm):
    rows = min(keep, tm)
    tiles = keep // rows
    return seq // tm - tiles, tiles, rows


def _qkv_p_body(x_ref, g_ref, w_ref, *refs, dils, keeps, seq, tm):
    ng = len(dils)
    out_refs, kvt_refs, ybuf_ref = refs[:3 * ng], refs[3 * ng:4 * ng], refs[-1]
    e = out_refs[0].shape[-1]
    j = pl.program_id(1)
    h = _rms(x_ref[0], g_ref[0:1]).astype(BF16)

    for g, dil in enumerate(dils):
        first, _, rows = _kept_tiles(seq, keeps[g], tm)
        for c in range(3):
            idx = g * 3 + c
            y = _dot(h, w_ref[:, idx * e:(idx + 1) * e])
            if c == 0:
                y = y * (HEAD_DIM ** -0.5)
            else:
                def keep_window(y=y, g=g, c=c, rows=rows):
                    kvt_refs[g][0, (c - 1) * e:c * e, :] = y[tm - rows:, :].T

                if first == 0:
                    keep_window()
                else:
                    pl.when(j >= first)(keep_window)
            if dil == 1:
                out_refs[idx][0, 0] = y.astype(BF16)
            else:
                slot = idx % 2
                for cc in range(e // LANES):
                    ybuf_ref[slot, cc] = y[:, cc * LANES:(cc + 1) * LANES]
                for r in range(dil):
                    out_refs[idx][0, r] = jnp.concatenate(
                        [ybuf_ref[slot, cc, pl.ds(r, tm // dil, stride=dil), :] for cc in range(e // LANES)],
                        axis=1).astype(BF16)


def _qkv_p(x, gains, w, first_group, dils, keeps, e, tm):
    b, s, d = x.shape
    width = len(dils) * 3 * e
    assert (first_group * 3 * e) % width == 0
    w_spec = pl.BlockSpec((d, width), lambda i, j: (0, first_group * 3 * e // width), pipeline_mode=pl.Buffered(1))
    out_specs, out_shape = [], []
    for dil in dils:
        for _ in range(3):
            out_specs.append(pl.BlockSpec((1, dil, tm // dil, e), lambda i, j: (i, 0, j, 0)))
            out_shape.append(jax.ShapeDtypeStruct((b, dil, s // dil, e), BF16))
    for keep in keeps:
        first, _, rows = _kept_tiles(s, keep, tm)
        out_specs.append(pl.BlockSpec((1, 2 * e, rows), lambda i, j, first=first: (i, 0, jnp.maximum(j - first, 0))))
        out_shape.append(jax.ShapeDtypeStruct((b, 2 * e, keep), F32))
    return pl.pallas_call(
        functools.partial(_qkv_p_body, dils=dils, keeps=tuple(keeps), seq=s, tm=tm),
        grid=(b, s // tm),
        in_specs=[pl.BlockSpec((1, tm, d), lambda i, j: (i, j, 0)), _resident(gains.shape), w_spec],
        out_specs=out_specs,
        out_shape=out_shape,
        scratch_shapes=[pltpu.VMEM((2, e // LANES, tm, LANES), F32)],
        compiler_params=_params("parallel", "arbitrary"),
        name="qkv_prompt",
    )(x, gains, w)


def _qkv_s_body(x_ref, g_ref, w_ref, q_ref, kvt_ref, *, n_groups):
    e = w_ref.shape[1] // (3 * n_groups)
    h = _rms(x_ref[...], g_ref[0:1]).astype(BF16)
    for g in range(n_groups):
        q_ref[:, g * e:(g + 1) * e] = _dot(h, w_ref[:, 3 * g * e:(3 * g + 1) * e]) * (HEAD_DIM ** -0.5)
        for c in (1, 2):
            kv = _dot(h, w_ref[:, (3 * g + c) * e:(3 * g + c + 1) * e])
            kvt_ref[(2 * g + c - 1) * e:(2 * g + c) * e, :] = kv.T


def _qkv_s(x, gains, w, n_groups):
    n = x.shape[0]
    e = w.shape[1] // (3 * n_groups)
    return pl.pallas_call(
        functools.partial(_qkv_s_body, n_groups=n_groups),
        out_shape=[jax.ShapeDtypeStruct((n, n_groups * e), F32), jax.ShapeDtypeStruct((n_groups * 2 * e, n), F32)],
        compiler_params=pltpu.CompilerParams(vmem_limit_bytes=VMEM_LIMIT),
        name="qkv_sample",
    )(x, gains, w)


def _attn_p_body(*refs, has_prev, n_heads):
    if has_prev:
        q_ref, kp_ref, kc_ref, vp_ref, vc_ref, bias_ref, ones_ref, o_ref, m_ref, l_ref = refs
        first_step = (pl.program_id(2) == 0).astype(jnp.int32)
    else:
        q_ref, kc_ref, vc_ref, bias_ref, ones_ref, o_ref, m_ref, l_ref = refs
    n_sub = q_ref.shape[2] // QB
    per = LANES // HEAD_DIM
    n_grp = n_heads // per
    lane = lax.broadcasted_iota(jnp.int32, (QB, LANES), 1)
    lane_head = lax.broadcasted_iota(jnp.int32, (1, LANES), 1) // HEAD_DIM

    def tiles(ref_prev, ref_cur, sub, cols):
        own = ref_cur[0, 0, sub * QB:(sub + 1) * QB, cols]
        if not has_prev:
            return [own]
        before = ref_prev[0, 0, :, cols] if sub == 0 else ref_cur[0, 0, (sub - 1) * QB:sub * QB, cols]
        return [before, own]

    def scores(item):
        sub, grp = item
        cols = slice(grp * LANES, (grp + 1) * LANES)
        q = q_ref[0, 0, sub * QB:(sub + 1) * QB, cols]
        qq = jnp.concatenate([q * jnp.where(lane_head == hh, 1.0, 0.0).astype(BF16) for hh in range(per)], axis=0)
        k = jnp.concatenate(tiles(kp_ref if has_prev else None, kc_ref, sub, cols), axis=0)
        first = first_step if (has_prev and sub == 0) else 0
        return _dot_nt(qq, k) + bias_ref[first, grp]

    items = [(sub, grp) for sub in range(n_sub) for grp in range(n_grp)]
    stats = [[jnp.zeros((QB, LANES), F32), jnp.ones((QB, LANES), F32)]
             for _ in range(n_sub)]
    queue = [scores(it) for it in items[:ATTN_AHEAD]]
    for idx, (sub, grp) in enumerate(items):
        s = queue.pop(0)
        if idx + ATTN_AHEAD < len(items):
            queue.append(scores(items[idx + ATTN_AHEAD]))
        cols = slice(grp * LANES, (grp + 1) * LANES)
        m = jnp.max(s, axis=-1, keepdims=True)
        e = jnp.exp(s - m).astype(BF16)
        v1 = jnp.concatenate([jnp.concatenate([v, ones_ref[...]], axis=1)
                              for v in tiles(vp_ref if has_prev else None, vc_ref, sub, cols)], axis=0)
        ol = _dot(e, v1)
        o, l = ol[:, :LANES], ol[:, LANES:]
        out = o[0:QB]
        for hh in range(per):
            rows = slice(hh * QB, (hh + 1) * QB)
            if hh > 0:
                out = jnp.where(lane_head == hh, o[rows], out)
            stats[sub][0] = jnp.where(lane == grp * per + hh, m[rows], stats[sub][0])
            stats[sub][1] = jnp.where(lane == grp * per + hh, l[rows], stats[sub][1])
        o_ref[0, 0, sub * QB:(sub + 1) * QB, cols] = out.astype(o_ref.dtype)
    for sub in range(n_sub):
        m_ref[0, 0, sub * QB:(sub + 1) * QB, :] = stats[sub][0]
        l_ref[0, 0, sub * QB:(sub + 1) * QB, :] = stats[sub][1]


def _attn_p(q, k, v, bias, g, n_heads):
    out_dims = q.shape[:3]
    has_prev = q.shape[2] > QB
    assert bias.shape[0] == (2 if has_prev else 1)
    if not has_prev:
        q, k, v = (a.reshape(a.shape[0], 1, a.shape[1] * a.shape[2], a.shape[3]) for a in (q, k, v))
    batch, dil, sub, e = q.shape
    nb = sub // QB
    ones = jnp.ones((QB, LANES), BF16)
    n_sub = ATTN_BLOCKS_PER_STEP if nb % ATTN_BLOCKS_PER_STEP == 0 else 1
    blk = (1, 1, n_sub * QB, e)
    cur = lambda b, r, n: (b, r, n, 0)
    prev = lambda b, r, n: (b, r, jnp.maximum(n * n_sub - 1, 0), 0)
    if has_prev:
        one = (1, 1, QB, e)
        in_specs = [pl.BlockSpec(blk, cur), pl.BlockSpec(one, prev), pl.BlockSpec(blk, cur),
                    pl.BlockSpec(one, prev), pl.BlockSpec(blk, cur)]
        args = (q, k, k, v, v)
    else:
        in_specs = [pl.BlockSpec(blk, cur)] * 3
        args = (q, k, v)
    outs = pl.pallas_call(
        functools.partial(_attn_p_body, has_prev=has_prev, n_heads=n_heads),
        grid=(batch, dil, nb // n_sub),
        in_specs=in_specs + [_resident(bias.shape), _resident(ones.shape)],
        out_specs=[pl.BlockSpec(blk, cur)] + [pl.BlockSpec((1, 1, n_sub * QB, LANES), cur)] * 2,
        out_shape=[jax.ShapeDtypeStruct((batch, dil, sub, e), BF16)]
                  + [jax.ShapeDtypeStruct((batch, dil, sub, LANES), F32)] * 2,
        compiler_params=_params("parallel", "parallel", "arbitrary"),
        name=f"attn_prompt_g{g}",
    )(*args, bias, ones)
    return [a.reshape(out_dims + a.shape[3:]) for a in outs]


def _expand_heads(w, e_ref):
    hi = w.astype(BF16)
    lo = (w - hi.astype(F32)).astype(BF16)
    return _dot(jnp.concatenate([hi, lo], axis=1), e_ref[...])


def _merge_groups(os_, ms, ls, e_ref):
    top = functools.reduce(jnp.maximum, ms)
    es = [jnp.exp(m - top) for m in ms]
    den = functools.reduce(lambda a, b: a + b, [e * l for e, l in zip(es, ls)])
    acc = None
    for o, e in zip(os_, es):
        term = _expand_heads(e / den, e_ref) * o
        acc = term if acc is None else acc + term
    return acc


def _merge_body(x_ref, *refs, dils, tm):
    ng = len(dils)
    o_refs, m_refs, l_refs = refs[:ng], refs[ng:2 * ng], refs[2 * ng:3 * ng]
    e_ref, wo_ref, g_ref, out_ref, obuf_ref, sbuf_ref = refs[3 * ng:]
    os_, ms, ls = [], [], []
    for gi, dil in enumerate(dils):
        if dil == 1:
            os_.append(o_refs[gi][0, 0].astype(F32))
            ms.append(m_refs[gi][0, 0])
            ls.append(l_refs[gi][0, 0])
        else:
            n_cc = obuf_ref.shape[1]
            for r in range(dil):
                rows = pl.ds(r, tm // dil, stride=dil)
                for cc in range(n_cc):
                    obuf_ref[gi, cc, rows, :] = o_refs[gi][0, r, :, cc * LANES:(cc + 1) * LANES].astype(F32)
                sbuf_ref[0, gi, rows, :] = m_refs[gi][0, r]
                sbuf_ref[1, gi, rows, :] = l_refs[gi][0, r]
            os_.append(jnp.concatenate([obuf_ref[gi, cc] for cc in range(n_cc)], axis=1))
            ms.append(sbuf_ref[0, gi])
            ls.append(sbuf_ref[1, gi])
    o = _merge_groups(os_, ms, ls, e_ref)
    y = _dot(o.astype(BF16), wo_ref[...])
    out_ref[0] = x_ref[0] + _rms(y, g_ref[1:2])


def _merge(x, os_, ms, ls, expand, w_o, gains, tm):
    b, s, d = x.shape
    dils = tuple(o.shape[1] for o in os_)
    e = os_[0].shape[-1]
    grp = lambda i, j: (i, 0, j, 0)
    return pl.pallas_call(
        functools.partial(_merge_body, dils=dils, tm=tm),
        grid=(b, s // tm),
        in_specs=[pl.BlockSpec((1, tm, d), lambda i, j: (i, j, 0))]
                 + [pl.BlockSpec((1, dil, tm // dil, e), grp) for dil in dils]
                 + [pl.BlockSpec((1, dil, tm // dil, LANES), grp) for dil in dils] * 2
                 + [_resident(expand.shape), _resident(w_o.shape), _resident(gains.shape)],
        out_specs=pl.BlockSpec((1, tm, d), lambda i, j: (i, j, 0)),
        out_shape=jax.ShapeDtypeStruct((b, s, d), F32),
        scratch_shapes=[pltpu.VMEM((len(dils), e // LANES, tm, LANES), F32),
                        pltpu.VMEM((2, len(dils), tm, LANES), F32)],
        compiler_params=_params("parallel", "parallel"),
        name="attn_merge",
    )(x, *os_, *ms, *ls, expand, w_o, gains)


class _WindowAttnChunk:
    def __init__(self, b, q, knew, vnew, c_ref, bias_c, bias_n, cout_ref, *, dil, n_steps, nt, hc):
        self.b, self.q, self.knew, self.vnew, self.c_ref, self.cout_ref = b, q.astype(BF16), knew, vnew, c_ref, cout_ref
        self.bias_c, self.bias_n, self.dil, self.n_steps, self.nt = bias_c, bias_n, dil, n_steps, nt
        self.heads = [slice(hh * HEAD_DIM, (hh + 1) * HEAD_DIM) for hh in range(hc)]
        self.lane = lax.broadcasted_iota(jnp.int32, (nt, LANES), 1)

    def logits(self):
        nt, dil = self.nt, self.dil
        dn = lax.broadcasted_iota(jnp.int32, (nt, LANES), 0) - lax.rem(self.lane, nt)
        valid_n = (self.lane // nt == self.b) & (dn >= 0) & (lax.rem(dn, dil) == 0) & (dn <= dil * self.n_steps)
        self.s_c = [_dot(self.q[:, sl], self.c_ref[0, 0, hh].astype(BF16)) + self.bias_c[hh]
                    for hh, sl in enumerate(self.heads)]
        self.s_n = [jnp.where(valid_n, _dot(self.q[:, sl], self.knew[sl, :].astype(BF16)) + self.bias_n[hh], NEG_INF)
                    for hh, sl in enumerate(self.heads)]

    def softmax(self):
        ms = [jnp.maximum(jnp.max(a, axis=-1, keepdims=True), jnp.max(c, axis=-1, keepdims=True))
              for a, c in zip(self.s_c, self.s_n)]
        e_c = [jnp.exp(a - m) for a, m in zip(self.s_c, ms)]
        e_n = [jnp.exp(c - m) for c, m in zip(self.s_n, ms)]
        ls = [jnp.sum(a, axis=-1, keepdims=True) + jnp.sum(c, axis=-1, keepdims=True) for a, c in zip(e_c, e_n)]
        self.p_c = [(e / l).astype(BF16) for e, l in zip(e_c, ls)]
        self.p_n = [(e / l).astype(BF16) for e, l in zip(e_n, ls)]
        self.lse = jnp.zeros((self.nt, LANES), F32)
        for hh, (m, l) in enumerate(zip(ms, ls)):
            self.lse = jnp.where(self.lane == hh, m + jnp.log(l), self.lse)

    def output(self):
        os_ = [_dot_nt(self.c_ref[0, 1, hh].astype(BF16), self.p_c[hh])
               + _dot_nt(self.vnew[sl, :].astype(BF16), self.p_n[hh]) for hh, sl in enumerate(self.heads)]
        return jnp.concatenate(os_, axis=0), self.lse

    def shift(self):
        n, nt = self.c_ref.shape[-1], self.nt
        tail = lax.broadcasted_iota(jnp.int32, (HEAD_DIM, LANES), 1) >= LANES - nt
        lanes_to_tail = LANES - nt - self.b * nt
        for hh, sl in enumerate(self.heads):
            for kv, new in enumerate((self.knew, self.vnew)):
                rolled = pltpu.roll(self.c_ref[0, kv, hh], n - nt, axis=1)
                placed = pltpu.roll(new[sl, :], lanes_to_tail, axis=1)
                if n > LANES:
                    self.cout_ref[0, kv, hh, :, 0:n - LANES] = rolled[:, 0:n - LANES]
                self.cout_ref[0, kv, hh, :, n - LANES:n] = jnp.where(tail, placed, rolled[:, n - LANES:n])


def _ffn_attn_s_body(x_ref, g_ref, wup_ref, wdn_ref, q_ref, kvt_ref, *refs, groups, nt, hc, f_chunk, n_chunks):
    ng = len(groups)
    c_refs, bc_refs, bn_refs = refs[:ng], refs[ng:2 * ng], refs[2 * ng:3 * ng]
    y_ref = refs[3 * ng]
    o_refs, lse_refs, cout_refs = (refs[3 * ng + 1 + k * ng:3 * ng + 1 + (k + 1) * ng] for k in range(3))
    h_ref, acc_ref = refs[-2:]
    i, j = pl.program_id(0), pl.program_id(1)

    @pl.when(j == 0)
    def _():
        h_ref[...] = _rms(x_ref[...], g_ref[2:3]).astype(BF16)
        acc_ref[...] = jnp.zeros_like(acc_ref)

    cw = hc * HEAD_DIM
    e = cw * n_chunks
    attn = []
    for g, (dil, n_steps) in enumerate(groups):
        knew = kvt_ref[pl.ds(pl.multiple_of((2 * g) * e + j * cw, cw), cw), :]
        vnew = kvt_ref[pl.ds(pl.multiple_of((2 * g + 1) * e + j * cw, cw), cw), :]
        bias_c = [bc_refs[g][j * hc + hh] for hh in range(hc)]
        bias_n = [bn_refs[g][j * hc + hh] for hh in range(hc)]
        attn.append(_WindowAttnChunk(i, q_ref[i, g, j], knew, vnew, c_refs[g], bias_c, bias_n, cout_refs[g],
                                     dil=dil, n_steps=n_steps, nt=nt, hc=hc))
    hidden = pl.ds(pl.multiple_of(j * f_chunk, f_chunk), f_chunk)
    up = jnp.maximum(_dot(h_ref[...], wup_ref[0, :, hidden]), 0.0)
    for a in attn:
        a.logits()
    acc_ref[...] += _dot((up * up).astype(BF16), wdn_ref[0, hidden, :])
    for a in attn:
        a.softmax()
    for g, a in enumerate(attn):
        o_refs[g][0, j], lse_refs[g][0, j] = a.output()
    for a in attn:
        a.shift()

    @pl.when(j == n_chunks - 1)
    def _():
        y_ref[...] = x_ref[...] + _rms(acc_ref[...], g_ref[3:4])


def _ffn_attn_s(x, gains, w_up, w_down, layer, q, kvt_new, caches, bias_cs, bias_ns, groups, nt, tm):
    n, d = x.shape
    n_chunks, f_chunk = q.shape[2], w_up.shape[2] // q.shape[2]
    b, _, n_heads, _, _ = caches[0].shape
    hc = n_heads // n_chunks
    cw = hc * HEAD_DIM
    assert n // tm == b and kvt_new.shape[1] == LANES == b * nt and hc * n_chunks == n_heads
    assert all(c.shape[-1] == dil * n_steps for c, (dil, n_steps) in zip(caches, groups))
    row = lambda i, j: (i, 0)
    win = lambda i, j: (i, 0, j, 0, 0)
    per_row = lambda i, j: (i, 0, 0, 0)
    win_specs = [pl.BlockSpec((1, 2, hc, HEAD_DIM, c.shape[-1]), win) for c in caches]
    ng = len(caches)
    outs = pl.pallas_call(
        functools.partial(_ffn_attn_s_body, groups=groups, nt=nt, hc=hc, f_chunk=f_chunk, n_chunks=n_chunks),
        grid=(b, n_chunks),
        in_specs=[pl.BlockSpec((tm, d), row), _resident(gains.shape), _layer_weights(w_up, layer),
                  _layer_weights(w_down, layer), _resident(q.shape), _resident(kvt_new.shape)]
                 + win_specs + [_resident(a.shape) for a in bias_cs] + [_resident(a.shape) for a in bias_ns],
        out_specs=[pl.BlockSpec((tm, d), row)]
                  + [pl.BlockSpec((1, n_chunks, cw, nt), per_row)] * ng
                  + [pl.BlockSpec((1, n_chunks, nt, LANES), per_row)] * ng + win_specs,
        out_shape=[jax.ShapeDtypeStruct((n, d), F32)]
                  + [jax.ShapeDtypeStruct((b, n_chunks, cw, nt), F32)] * ng
                  + [jax.ShapeDtypeStruct((b, n_chunks, nt, LANES), F32)] * ng
                  + [jax.ShapeDtypeStruct(c.shape, F32) for c in caches],
        scratch_shapes=[pltpu.VMEM((tm, d), BF16), pltpu.VMEM((tm, d), F32)],
        compiler_params=_params("parallel", "arbitrary"),
        name="ffn_with_sample_attn",
    )(x, gains, w_up, w_down, q, kvt_new, *caches, *bias_cs, *bias_ns)
    return outs[0], outs[1:1 + ng], outs[1 + ng:1 + 2 * ng], outs[1 + 2 * ng:]


def _conv_tail(y, x, lng_ref, lnb_ref, w2_ref, b2_ref, g_ref):
    mu = jnp.mean(y, axis=-1, keepdims=True)
    yc = y - mu
    var = jnp.mean(yc * yc, axis=-1, keepdims=True)
    yn = yc * lax.rsqrt(var + EPS) * lng_ref[...] + lnb_ref[...]
    z = _dot((yn * jax.nn.sigmoid(yn)).astype(BF16), w2_ref[...]) + b2_ref[...]
    return x + _rms(z, g_ref[1:2])


def _glu(x, g_ref, w1_ref, b1_ref):
    d = x.shape[1]
    a = _dot(_rms(x, g_ref[0:1]).astype(BF16), w1_ref[...]) + b1_ref[...]
    return a[:, :d] * jax.nn.sigmoid(a[:, d:])


def _conv_p_body(x_ref, g_ref, w1_ref, b1_ref, wdw_ref, bdw_ref, lng_ref, lnb_ref, w2_ref, b2_ref,
                 o_ref, ul_ref, ush_ref, y_ref, *, ts):
    j = pl.program_id(1)
    n_cb = y_ref.shape[0]
    rows_ext = CARRY_ROWS + ts

    @pl.when(j == 0)
    def _():
        ush_ref[0, :, 0:CARRY_ROWS, :] = jnp.zeros((n_cb, CARRY_ROWS, LANES), F32)

    x = x_ref[0]
    u = _glu(x, g_ref, w1_ref, b1_ref)
    for cb in range(n_cb):
        ush_ref[0, cb, CARRY_ROWS:, :] = u[:, cb * LANES:(cb + 1) * LANES]
    n_sh = rows_ext - SUBLANES
    lead = CARRY_ROWS - (CONV_WIDTH - 1)

    def column_block(cb, carry):
        for m in range(1, SUBLANES):
            for r0 in range(0, n_sh, CONV_COPY_ROWS):
                r1 = min(r0 + CONV_COPY_ROWS, n_sh)
                ush_ref[m, cb, r0:r1, :] = ush_ref[0, cb, r0 + m:r1 + m, :]

        def rows(c, carry):
            r0 = pl.multiple_of(c * CONV_CHUNK_ROWS, CONV_CHUNK_ROWS)
            acc = jnp.broadcast_to(bdw_ref[cb], (CONV_CHUNK_ROWS, LANES))
            for k in range(CONV_WIDTH):
                a, m = divmod(lead + k, SUBLANES)
                acc = acc + wdw_ref[cb, k:k + 1, :] * ush_ref[m, cb, pl.ds(r0 + a * SUBLANES, CONV_CHUNK_ROWS), :]
            y_ref[cb, pl.ds(r0, CONV_CHUNK_ROWS), :] = acc
            return carry

        return lax.fori_loop(0, ts // CONV_CHUNK_ROWS, rows, carry)

    lax.fori_loop(0, n_cb, column_block, 0)
    y = jnp.concatenate([y_ref[cb] for cb in range(n_cb)], axis=1)
    o_ref[0] = _conv_tail(y, x, lng_ref, lnb_ref, w2_ref, b2_ref, g_ref)
    last = [ush_ref[0, cb, ts:ts + CARRY_ROWS, :] for cb in range(n_cb)]
    for cb in range(n_cb):
        ush_ref[0, cb, 0:CARRY_ROWS, :] = last[cb]
    ul_ref[0] = jnp.concatenate(last, axis=1)


def _conv_p(x, gains, w1, b1, wdw, bdw, lng, lnb, w2, b2, ts):
    b, s, d = x.shape
    n_cb = d // LANES
    wdw_cb = wdw.reshape(wdw.shape[0], n_cb, LANES).transpose(1, 0, 2)
    bdw_cb = bdw.reshape(n_cb, 1, LANES)
    consts = (gains, w1, b1, wdw_cb, bdw_cb, lng, lnb, w2, b2)
    return pl.pallas_call(
        functools.partial(_conv_p_body, ts=ts),
        grid=(b, s // ts),
        in_specs=[pl.BlockSpec((1, ts, d), lambda i, j: (i, j, 0))] + [_resident(a.shape) for a in consts],
        out_specs=[pl.BlockSpec((1, ts, d), lambda i, j: (i, j, 0)),
                   pl.BlockSpec((1, CARRY_ROWS, d), lambda i, j: (i, 0, 0))],
        out_shape=[jax.ShapeDtypeStruct((b, s, d), F32), jax.ShapeDtypeStruct((b, CARRY_ROWS, d), F32)],
        scratch_shapes=[pltpu.VMEM((SUBLANES, n_cb, CARRY_ROWS + ts, LANES), F32), pltpu.VMEM((n_cb, ts, LANES), F32)],
        compiler_params=_params("parallel", "arbitrary"),
        name="conv_prompt",
    )(x, *consts)


def _conv_s_body(x_ref, st_ref, g_ref, w1_ref, b1_ref, wdw_ref, bdw_ref, lng_ref, lnb_ref, w2_ref, b2_ref,
                 o_ref, ns_ref, *, nb, nt):
    x = x_ref[...]
    u = _glu(x, g_ref, w1_ref, b1_ref)
    n_prev = CONV_WIDTH - 1
    ext = [st_ref[i * nb:(i + 1) * nb, :] for i in range(n_prev)]
    ext += [u[t * nb:(t + 1) * nb, :] for t in range(nt)]
    ys = []
    for t in range(nt):
        y = None
        for k in range(CONV_WIDTH):
            term = wdw_ref[k:k + 1, :] * ext[t + k]
            y = term if y is None else y + term
        ys.append(y + bdw_ref[...])
    o_ref[...] = _conv_tail(jnp.concatenate(ys, axis=0), x, lng_ref, lnb_ref, w2_ref, b2_ref, g_ref)
    for i in range(n_prev):
        ns_ref[i * nb:(i + 1) * nb, :] = ext[nt + i]


def _conv_s(x, state, gains, w1, b1, wdw, bdw, lng, lnb, w2, b2, nb, nt):
    return pl.pallas_call(
        functools.partial(_conv_s_body, nb=nb, nt=nt),
        out_shape=[jax.ShapeDtypeStruct(x.shape, F32), jax.ShapeDtypeStruct(state.shape, F32)],
        compiler_params=pltpu.CompilerParams(vmem_limit_bytes=VMEM_LIMIT),
        name="conv_sample",
    )(x, state, gains, w1, b1, wdw, bdw, lng, lnb, w2, b2)


def _t5_bucket(dist):
    max_exact = N_BUCKETS // 2
    df = jnp.maximum(dist, 1).astype(F32)
    large = max_exact + (jnp.log(df / max_exact) / math.log(MAX_DISTANCE / max_exact)
                         * (N_BUCKETS - max_exact)).astype(jnp.int32)
    large = jnp.minimum(large, N_BUCKETS - 1)
    return jnp.where(dist < max_exact, dist, large)


def _bias_by_step(rel_bias, g, dil, n_steps, n_heads):
    tab = rel_bias[:, g * n_heads:(g + 1) * n_heads]
    bucket = _t5_bucket(dil * jnp.arange(n_steps + 1, dtype=jnp.int32))
    onehot = (bucket[:, None] == jnp.arange(N_BUCKETS, dtype=jnp.int32)[None, :]).astype(F32)
    return jnp.dot(onehot, tab, precision=lax.Precision.HIGHEST)


def _band_bias(by_step, n_steps, nk):
    off = nk - QB
    period = 2 * nk
    heads = by_step.shape[1]
    base = jnp.concatenate([by_step[::-1].T, jnp.full((heads, period - n_steps - 1), NEG_INF, F32)], axis=1)
    w = jnp.roll(base, off - n_steps, axis=1)
    r = period - 1
    flat = jnp.tile(w, (1, -(-(QB * r) // period)))[:, :QB * r]
    band = flat.reshape(heads, QB, r)[:, :, :nk]
    if off > 0:
        hidden = (np.arange(nk) < off)[None, None, :]
        band = jnp.stack([band, jnp.where(hidden, NEG_INF, band)])
    else:
        band = band[None]
    per = LANES // HEAD_DIM
    return band.reshape(band.shape[0], heads // per, per * QB, nk)


def _window_bias(by_step, dil, n_steps, nt):
    base = jnp.repeat(by_step[n_steps:0:-1].T, dil, axis=1)
    bias = jnp.stack([jnp.roll(base, t, axis=1) for t in range(nt)], axis=1)
    n = dil * n_steps
    delta = n + np.arange(nt)[:, None] - np.arange(n)[None, :]
    attended = (delta % dil == 0) & (delta <= dil * n_steps)
    return jnp.where(attended[None], bias, NEG_INF)


def _new_key_bias(by_step, dil, n_steps, nt):
    zero = jnp.zeros_like(by_step[0])
    rows = []
    for t in range(nt):
        cols = []
        for tp in range(nt):
            s, rem = divmod(t - tp, dil)
            cols.append(by_step[s] if (t >= tp and rem == 0 and s <= n_steps) else zero)
        rows.append(jnp.stack(cols, axis=1))
    return jnp.tile(jnp.stack(rows, axis=1), (1, 1, LANES // nt))


def _time_major(a):
    b, t, d = a.shape
    return a.transpose(1, 0, 2).reshape(t * b, d)


def _batch_major(a, b):
    return a.reshape(-1, b, a.shape[1]).transpose(1, 0, 2)


def _kv_window_layout(c):
    return c.transpose(0, 2, 3, 4, 1)


def _kv_window_unlayout(c):
    return c.transpose(0, 4, 1, 2, 3)


def kernel(x_prompt, x_sample, state_pool, cache_kv_g0, cache_kv_g1, cache_kv_g2, state_conv, norm_gains, rel_bias, pool_w, pool_scale, attn_w_qkv, attn_w_o, conv_w_pw1, conv_b_pw1, conv_w_dw, conv_b_dw, conv_ln_g, conv_ln_b, conv_w_pw2, conv_b_pw2, ffn_w_up, ffn_w_down):
    batch, seq, d = x_prompt.shape
    dec_b, dec_t, _ = x_sample.shape
    depth = norm_gains.shape[0]
    n_heads = attn_w_o.shape[1] // HEAD_DIM
    e_dim = n_heads * HEAD_DIM
    n_groups = len(ATTN_GROUPS)
    dils = tuple(dil for _, dil in ATTN_GROUPS)
    caches_in = (cache_kv_g0, cache_kv_g1, cache_kv_g2)
    n_tok = batch * seq
    n_dec = dec_b * dec_t

    xp = x_prompt
    xs = _time_major(x_sample)

    expand = jnp.asarray(np.tile(np.kron(np.eye(LANES, n_heads), np.ones((1, HEAD_DIM))), (2, 1)), BF16)

    w_up, w_down = ffn_w_up.astype(BF16), ffn_w_down.astype(BF16)

    pool_p, pool_s, conv_p, conv_s = [], [], [], []
    kv_p = [[] for _ in ATTN_GROUPS]
    kv_s = [[] for _ in ATTN_GROUPS]
    for i in range(depth):
        kind, j = i % 3, i // 3
        gains = norm_gains[i]
        if kind == 0:
            pw, ps = pool_w[j].astype(BF16), pool_scale[j][None, :]
            xs, new_state = _pool_s(xs, _time_major(state_pool[j]), gains, pw, ps,
                                    nb=dec_b, nt=dec_t, pos0=PAST_LEN)
            pool_s.append(_batch_major(new_state, dec_b))
            xp, h_last, xs = _pool_p(xp, xs, gains, pw, ps, w_up, w_down, i, ts=1024)
            pool_p.append(h_last[:, POOL_CARRY - POOL_STATE:])
            continue
        elif kind == 1:
            w_qkv = attn_w_qkv[j].astype(BF16)
            w_o = attn_w_o[j].astype(BF16)
            by_step = [_bias_by_step(rel_bias, g, dil, win // dil, n_heads)
                       for g, (win, dil) in enumerate(ATTN_GROUPS)]
            keeps = [min(win, seq) for win, _ in ATTN_GROUPS]
            qkv, kvts = [], []
            for lo, hi in QKV_GROUP_SPLIT:
                outs = _qkv_p(xp, gains, w_qkv, lo, dils[lo:hi], keeps[lo:hi], e_dim, tm=512)
                qkv += outs[:3 * (hi - lo)]
                kvts += outs[3 * (hi - lo):]
            os_, ms, ls = [], [], []
            for g, (win, dil) in enumerate(ATTN_GROUPS):
                n_steps = win // dil
                nk = 2 * QB if seq // dil > QB else QB
                bias = _band_bias(by_step[g], n_steps, nk)
                o, m, l = _attn_p(qkv[3 * g], qkv[3 * g + 1], qkv[3 * g + 2], bias, g, n_heads)
                os_.append(o)
                ms.append(m)
                ls.append(l)
                kv_p[g].append(_kv_window_unlayout(kvts[g].reshape(batch, 2, n_heads, HEAD_DIM, keeps[g])))
            xp = _merge(xp, os_, ms, ls, expand, w_o, gains, tm=1024)
            xs_b = _batch_major(xs, dec_b)
            xs_flat = xs_b.reshape(n_dec, d)
            n_chunks = ffn_w_up.shape[2] // FFN_CHUNK
            hc = n_heads // n_chunks
            q_s, kvt_new = _qkv_s(xs_flat, gains, w_qkv, n_groups)
            q_s = q_s.reshape(dec_b, dec_t, n_groups, n_chunks, hc * HEAD_DIM)
            q_s = q_s.transpose(0, 2, 3, 1, 4)
            groups = tuple((dil, win // dil) for win, dil in ATTN_GROUPS)
            caches = [_kv_window_layout(c[j]) for c in caches_in]
            bias_cs = [_window_bias(by_step[g], dil, n_steps, dec_t) for g, (dil, n_steps) in enumerate(groups)]
            bias_ns = [_new_key_bias(by_step[g], dil, n_steps, dec_t) for g, (dil, n_steps) in enumerate(groups)]
            xp, os_, lses, new_caches = _ffn_attn_s(xp.reshape(n_tok, d), gains, w_up, w_down, i, q_s, kvt_new,
                                                    caches, bias_cs, bias_ns, groups, dec_t, tm=n_tok // dec_b)
            xp = xp.reshape(batch, seq, d)
            ms, ls = [], []
            for g in range(n_groups):
                kv_s[g].append(_kv_window_unlayout(new_caches[g]))
                lse = lses[g][..., :hc].transpose(0, 2, 1, 3).reshape(1, 1, n_dec, n_heads)
                ms.append(jnp.pad(lse, ((0, 0), (0, 0), (0, 0), (0, LANES - n_heads))))
                ls.append(jnp.ones((1, 1, n_dec, LANES), F32))
            os_ = [o.transpose(0, 3, 1, 2).reshape(1, 1, n_dec, e_dim) for o in os_]
            xs_flat = _merge(xs_flat[None], os_, ms, ls, expand, w_o, gains, tm=n_dec)[0]
            xs = _time_major(xs_flat.reshape(dec_b, dec_t, d))
            xs = _ffn_s(xs, gains, w_up, w_down, i)
            continue
        else:
            cw = (conv_w_pw1[j].astype(BF16), conv_b_pw1[j][None, :], conv_w_dw[j], conv_b_dw[j][None, :],
                  conv_ln_g[j][None, :], conv_ln_b[j][None, :], conv_w_pw2[j].astype(BF16), conv_b_pw2[j][None, :])
            xp, u_last = _conv_p(xp, gains, *cw, ts=512)
            conv_p.append(u_last[:, CARRY_ROWS - (CONV_WIDTH - 1):])
            xs, new_state = _conv_s(xs, _time_major(state_conv[j]), gains, *cw, nb=dec_b, nt=dec_t)
            conv_s.append(_batch_major(new_state, dec_b))
        xp, xs = _ffn(xp.reshape(n_tok, d), xs, gains, w_up, w_down, i, tm=1024)
        xp = xp.reshape(batch, seq, d)
    return (xp, _batch_major(xs, dec_b),
            jnp.stack(pool_p), jnp.stack(pool_s),
            jnp.stack(kv_p[0]), jnp.stack(kv_s[0]),
            jnp.stack(kv_p[1]), jnp.stack(kv_s[1]),
            jnp.stack(kv_p[2]), jnp.stack(kv_s[2]),
            jnp.stack(conv_p), jnp.stack(conv_s))
```
